```python
import math
import jax, jax.numpy as jnp
from jax import lax
import numpy as np

D_MODEL = 1024
BATCH = 8
SEQ = 4096
DEPTH = 2

ATTN_HEADS = 8
HEAD_DIM = 64
ATTN_W = ATTN_HEADS * 2 * HEAD_DIM
Q_BLOCK = 128
LRU_W = D_MODEL
LRU_BLOCKS = 8
LRU_BW = LRU_W // LRU_BLOCKS
LRU_C = 8.0
CONV_W = 4
N_BRANCH = 2
IN_COLS = 3 * ATTN_W + 2 * LRU_W + N_BRANCH * D_MODEL
PEER_HEADS = 8
N_KEYS = 128
N_EXPERTS = N_KEYS * N_KEYS
D_KEY = 256
D_KEY_HALF = D_KEY // 2
PEER_TOPK = 16
PEER_CHUNK = 128
DEEPNORM_ALPHA = (2.0 * DEPTH) ** 0.25
DEEPNORM_BETA = (8.0 * DEPTH) ** -0.25
LN_EPS = 1e-5
RMS_EPS = 1e-6

kernel_name = "hybrid_diffattn_rglru_peer_deepnorm"


def layer_norm(x, g, b):
    xf = x.astype(jnp.float32)
    mu = jnp.mean(xf, axis=-1, keepdims=True)
    var = jnp.mean(jnp.square(xf - mu), axis=-1, keepdims=True)
    y = (xf - mu) * lax.rsqrt(var + LN_EPS) * g.astype(jnp.float32) + b.astype(jnp.float32)
    return y.astype(x.dtype)


def alibi_slopes(n_heads):
    return jnp.exp2(-8.0 * jnp.arange(1, n_heads + 1, dtype=jnp.float32) / n_heads)


def diff_attention(q, k, v, lambda_qk, subln_g, lam_init):
    B, S = q.shape[0], q.shape[1]
    n_blk = S // Q_BLOCK
    scale = HEAD_DIM ** -0.5
    lq = lambda_qk.astype(jnp.float32)
    lam = jnp.exp(jnp.sum(lq[0] * lq[1])) - jnp.exp(jnp.sum(lq[2] * lq[3])) + lam_init
    slopes = alibi_slopes(ATTN_HEADS)
    k1, k2 = k[:, :, :, 0], k[:, :, :, 1]
    vf = v.astype(jnp.float32)
    pos = jnp.arange(S, dtype=jnp.int32)

    def to_blocks(t):
        return jnp.moveaxis(t.reshape(B, n_blk, Q_BLOCK, ATTN_HEADS, HEAD_DIM), 1, 0)

    def block(args):
        q1b, q2b, qpos = args
        rel = (qpos[:, None] - pos[None, :]).astype(jnp.float32)
        causal = pos[None, :] <= qpos[:, None]
        bias = -slopes[:, None, None] * rel

        def probs(qb, kk):
            s = jnp.einsum('bqhd,bkhd->bhqk', qb, kk).astype(jnp.float32) * scale + bias
            s = jnp.where(causal, s, -jnp.inf)
            return jax.nn.softmax(s, axis=-1)

        attn = probs(q1b, k1) - lam * probs(q2b, k2)
        return jnp.einsum('bhqk,bkhe->bqhe', attn, vf)

    o = lax.map(block, (to_blocks(q[:, :, :, 0]), to_blocks(q[:, :, :, 1]),
                        pos.reshape(n_blk, Q_BLOCK)))
    o = jnp.moveaxis(o, 0, 1).reshape(B, S, ATTN_HEADS, 2 * HEAD_DIM)
    o = o * lax.rsqrt(jnp.mean(jnp.square(o), axis=-1, keepdims=True) + RMS_EPS)
    o = o * subln_g.astype(jnp.float32) * (1.0 - lam_init)
    return o.reshape(B, S, ATTN_W).astype(v.dtype)


def causal_conv(xb, w, b):
    S = xb.shape[1]
    xp = jnp.pad(xb, ((0, 0), (CONV_W - 1, 0), (0, 0)))
    y = b + xp[:, 0:S] * w[0]
    for tap in range(1, CONV_W):
        y = y + xp[:, tap:tap + S] * w[tap]
    return y


def rg_lru(xb, gate_a_w, gate_a_b, gate_x_w, gate_x_b, lru_lambda):
    B, S, W = xb.shape
    xg = xb.reshape(B, S, LRU_BLOCKS, LRU_BW)
    r = jax.nn.sigmoid(jnp.einsum('bsgi,gij->bsgj', xg, gate_a_w).reshape(B, S, W) + gate_a_b)
    i = jax.nn.sigmoid(jnp.einsum('bsgi,gij->bsgj', xg, gate_x_w).reshape(B, S, W) + gate_x_b)
    log_a = -LRU_C * r.astype(jnp.float32) * jax.nn.softplus(-lru_lambda.astype(jnp.float32))
    a = jnp.exp(log_a)
    first = (jnp.arange(S) == 0)[None, :, None]
    mult = jnp.where(first, 1.0, jnp.sqrt(-jnp.expm1(2.0 * log_a)))
    u = mult * (i * xb).astype(jnp.float32)

    def combine(c1, c2):
        a1, b1 = c1
        a2, b2 = c2
        return a1 * a2, a2 * b1 + b2

    _, h = lax.associative_scan(combine, (a, u), axis=1)
    return h.astype(xb.dtype)


def token_mixer(x, w_in, lambda_qk, subln_g, conv_w, conv_b, gate_a_w, gate_a_b,
                gate_x_w, gate_x_b, lru_lambda, w_br_attn, w_br_lru, w_out, lam_init):
    B, S, _ = x.shape
    m = x @ w_in
    cuts = [ATTN_W, 2 * ATTN_W, 3 * ATTN_W, 3 * ATTN_W + LRU_W, 3 * ATTN_W + 2 * LRU_W]
    q, k, v, xr, gr, gates = jnp.split(m, cuts, axis=-1)
    q = q.reshape(B, S, ATTN_HEADS, 2, HEAD_DIM)
    k = k.reshape(B, S, ATTN_HEADS, 2, HEAD_DIM)
    v = v.reshape(B, S, ATTN_HEADS, 2 * HEAD_DIM)
    ya = diff_attention(q, k, v, lambda_qk, subln_g, lam_init)
    h = rg_lru(causal_conv(xr, conv_w, conv_b), gate_a_w, gate_a_b, gate_x_w, gate_x_b, lru_lambda)
    yr = h * jax.nn.gelu(gr)
    g = jax.nn.sigmoid(gates.reshape(B, S, N_BRANCH, D_MODEL))
    merged = g[:, :, 0] * (ya @ w_br_attn) + g[:, :, 1] * (yr @ w_br_lru)
    return merged @ w_out


def peer(x, peer_wq, peer_subkeys, peer_u, peer_v):
    B, S, D = x.shape
    T = B * S
    xt = x.reshape(T, D)
    q = (xt @ peer_wq).reshape(T, PEER_HEADS, 2, D_KEY_HALF)
    sc = jnp.einsum('thcd,hcnd->thcn', q, peer_subkeys).astype(jnp.float32)
    s1, i1 = lax.top_k(sc[:, :, 0], PEER_TOPK)
    s2, i2 = lax.top_k(sc[:, :, 1], PEER_TOPK)
    cand = (s1[..., :, None] + s2[..., None, :]).reshape(T, PEER_HEADS, PEER_TOPK * PEER_TOPK)
    top_s, top_c = lax.top_k(cand, PEER_TOPK)
    e = (jnp.take_along_axis(i1, top_c // PEER_TOPK, axis=-1) * N_KEYS
         + jnp.take_along_axis(i2, top_c % PEER_TOPK, axis=-1))
    g = jax.nn.softmax(top_s, axis=-1).astype(x.dtype)
    n_chunk = T // PEER_CHUNK
    HK = PEER_HEADS * PEER_TOPK

    def chunk(args):
        xc, ec, gc = args
        uc = jnp.take(peer_u, ec, axis=0)
        vc = jnp.take(peer_v, ec, axis=0)
        act = jax.nn.gelu(jnp.einsum('cd,ced->ce', xc, uc))
        return jnp.einsum('ce,ced->cd', gc * act, vc)

    out = lax.map(chunk, (xt.reshape(n_chunk, PEER_CHUNK, D),
                          e.reshape(n_chunk, PEER_CHUNK, HK),
                          g.reshape(n_chunk, PEER_CHUNK, HK)))
    return out.reshape(B, S, D)


def setup_inputs(seed: int = 0) -> dict:
    key = jax.random.key(seed)
    ks = jax.random.split(key, 24)
    f32 = jnp.float32

    def nrm(k, shape, scale):
        return jax.random.normal(k, shape, f32) * scale

    lam_u = jax.random.uniform(ks[10], (DEPTH, LRU_W), f32, 0.9, 0.999)
    return {
        "x": nrm(ks[0], (BATCH, SEQ, D_MODEL), 1.0),
        "w_in": nrm(ks[1], (DEPTH, D_MODEL, IN_COLS), D_MODEL ** -0.5),
        "lambda_qk": nrm(ks[2], (DEPTH, 4, HEAD_DIM), 0.1),
        "subln_g": 1.0 + nrm(ks[3], (DEPTH, 2 * HEAD_DIM), 0.02),
        "conv_w": nrm(ks[4], (DEPTH, CONV_W, LRU_W), CONV_W ** -0.5),
        "conv_b": nrm(ks[5], (DEPTH, LRU_W), 0.02),
        "gate_a_w": nrm(ks[6], (DEPTH, LRU_BLOCKS, LRU_BW, LRU_BW), LRU_BW ** -0.5),
        "gate_a_b": nrm(ks[7], (DEPTH, LRU_W), 0.02),
        "gate_x_w": nrm(ks[8], (DEPTH, LRU_BLOCKS, LRU_BW, LRU_BW), LRU_BW ** -0.5),
        "gate_x_b": nrm(ks[9], (DEPTH, LRU_W), 0.02),
        "lru_lambda": jnp.log(lam_u) - jnp.log1p(-lam_u),
        "w_br_attn": nrm(ks[11], (DEPTH, ATTN_W, D_MODEL), ATTN_W ** -0.5),
        "w_br_lru": nrm(ks[12], (DEPTH, LRU_W, D_MODEL), LRU_W ** -0.5),
        "w_out": nrm(ks[13], (DEPTH, D_MODEL, D_MODEL), D_MODEL ** -0.5 * DEEPNORM_BETA),
        "ln1_g": 1.0 + nrm(ks[14], (DEPTH, D_MODEL), 0.02),
        "ln1_b": nrm(ks[15], (DEPTH, D_MODEL), 0.02),
        "peer_wq": nrm(ks[16], (DEPTH, D_MODEL, PEER_HEADS * D_KEY), D_MODEL ** -0.5),
        "peer_subkeys": nrm(ks[17], (DEPTH, PEER_HEADS, 2, N_KEYS, D_KEY_HALF), D_KEY_HALF ** -0.5),
        "peer_u": nrm(ks[18], (DEPTH, N_EXPERTS, D_MODEL), D_MODEL ** -0.5),
        "peer_v": nrm(ks[19], (DEPTH, N_EXPERTS, D_MODEL), PEER_TOPK ** -0.5 * DEEPNORM_BETA),
        "ln2_g": 1.0 + nrm(ks[20], (DEPTH, D_MODEL), 0.02),
        "ln2_b": nrm(ks[21], (DEPTH, D_MODEL), 0.02),
    }


def reference(x, w_in, lambda_qk, subln_g, conv_w, conv_b, gate_a_w, gate_a_b, gate_x_w,
              gate_x_b, lru_lambda, w_br_attn, w_br_lru, w_out, ln1_g, ln1_b, peer_wq,
              peer_subkeys, peer_u, peer_v, ln2_g, ln2_b):
    for l in range(DEPTH):
        lam_init = 0.8 - 0.6 * math.exp(-0.3 * l)
        mix = token_mixer(x, w_in[l], lambda_qk[l], subln_g[l], conv_w[l], conv_b[l],
                          gate_a_w[l], gate_a_b[l], gate_x_w[l], gate_x_b[l], lru_lambda[l],
                          w_br_attn[l], w_br_lru[l], w_out[l], lam_init)
        x = layer_norm(DEEPNORM_ALPHA * x + mix, ln1_g[l], ln1_b[l])
        ffn = peer(x, peer_wq[l], peer_subkeys[l], peer_u[l], peer_v[l])
        x = layer_norm(DEEPNORM_ALPHA * x + ffn, ln2_g[l], ln2_b[l])
    return x
```

```python
import functools
import math

import jax
import jax.numpy as jnp
from jax import lax
from jax.experimental import pallas as pl
from jax.experimental.pallas import tpu as pltpu

F32 = jnp.float32
BF16 = jnp.bfloat16

ATTN_HEADS = 8
HEAD_DIM = 64
HEAD_W = 2 * HEAD_DIM
LRU_BLOCKS = 8
LRU_C = 8.0
CONV_W = 4
PEER_HEADS = 8
N_KEYS = 128
PEER_TOPK = 16
LN_EPS = 1e-5
RMS_EPS = 1e-6
SUBLANES = 8
VMEM_LIMIT = 56 * 1024 * 1024


def _params(*sem):
    return pltpu.CompilerParams(dimension_semantics=sem, vmem_limit_bytes=VMEM_LIMIT)


def _mm_kernel(x_ref, w_ref, o_ref):
    o_ref[...] = jnp.dot(x_ref[...].astype(BF16), w_ref[...],
                         preferred_element_type=F32).astype(o_ref.dtype)


def _matmul(x, w, out_dtype, tm, tn):
    t, k = x.shape
    n = w.shape[1]
    return pl.pallas_call(
        _mm_kernel,
        grid=(t // tm, n // tn),
        in_specs=[pl.BlockSpec((tm, k), lambda i, j: (i, 0)),
                  pl.BlockSpec((k, tn), lambda i, j: (0, j))],
        out_specs=pl.BlockSpec((tm, tn), lambda i, j: (i, j)),
        out_shape=jax.ShapeDtypeStruct((t, n), out_dtype),
        compiler_params=_params("parallel", "arbitrary"),
        name="in_proj",
    )(x, w)


def _attn_kernel(slopes_ref, lqk_ref, g_ref, q_ref, k_ref, v_ref, o_ref, *, tq, tk, lam_init):
    h = pl.program_id(1)
    qi = pl.program_id(2)
    slope = slopes_ref[h]
    lq = lqk_ref[...]
    lam = (jnp.exp(jnp.sum(lq[0:1] * lq[1:2], axis=1, keepdims=True))
           - jnp.exp(jnp.sum(lq[2:3] * lq[3:4], axis=1, keepdims=True)) + lam_init)

    qs = q_ref[0] * (HEAD_DIM ** -0.5)
    lane = lax.broadcasted_iota(jnp.int32, (tq, HEAD_W), 1)
    zero = jnp.zeros_like(qs)
    qq = jnp.concatenate([jnp.where(lane < HEAD_DIM, qs, zero),
                          jnp.where(lane >= HEAD_DIM, qs, zero)], axis=0)

    row = lax.broadcasted_iota(jnp.int32, (2 * tq, tk), 0)
    col = lax.broadcasted_iota(jnp.int32, (2 * tq, tk), 1)
    d0 = (jnp.where(row >= tq, row - tq, row) - col).astype(F32)

    def step(j, carry, masked):
        m, l, acc = carry
        start = pl.multiple_of(j * tk, tk)
        k = k_ref[0, pl.ds(start, tk), :]
        v = v_ref[0, pl.ds(start, tk), :]
        s = lax.dot_general(qq, k, (((1,), (1,)), ((), ())), preferred_element_type=F32)
        rel = d0 + (qi * tq - j * tk).astype(F32)
        s = s - slope * rel
        if masked:
            s = jnp.where(rel >= 0.0, s, -jnp.inf)
        m_new = jnp.maximum(m, jnp.max(s, axis=1, keepdims=True))
        a = jnp.exp(m - m_new)
        p = jnp.exp(s - m_new)
        l = a * l + jnp.sum(p, axis=1, keepdims=True)
        acc = a * acc + jnp.dot(p.astype(BF16), v, preferred_element_type=F32)
        return m_new, l, acc

    n_full = (qi * tq) // tk
    n_all = ((qi + 1) * tq + tk - 1) // tk
    carry = (jnp.full((2 * tq, 1), -jnp.inf, F32), jnp.zeros((2 * tq, 1), F32),
             jnp.zeros((2 * tq, HEAD_W), F32))
    carry = lax.fori_loop(0, n_full, functools.partial(step, masked=False), carry)
    _, l, acc = lax.fori_loop(n_full, n_all, functools.partial(step, masked=True), carry)

    o = acc / l
    o = o[:tq] - lam * o[tq:]
    o = o * lax.rsqrt(jnp.mean(o * o, axis=-1, keepdims=True) + RMS_EPS)
    o = o * g_ref[...] * (1.0 - lam_init)
    o_ref[0] = o.astype(o_ref.dtype)


def _diff_attention(qkv, slopes, lambda_qk, subln_g, lam_init, tq, tk):
    b, s, _ = qkv.shape
    kern = functools.partial(_attn_kernel, tq=tq, tk=tk, lam_init=lam_init)
    return pl.pallas_call(
        kern,
        grid=(b, ATTN_HEADS, s // tq),
        in_specs=[pl.BlockSpec(memory_space=pltpu.SMEM),
                  pl.BlockSpec((4, HEAD_DIM), lambda b_, h, i: (0, 0)),
                  pl.BlockSpec((1, HEAD_W), lambda b_, h, i: (0, 0)),
                  pl.BlockSpec((1, tq, HEAD_W), lambda b_, h, i: (b_, i, h)),
                  pl.BlockSpec((1, s, HEAD_W), lambda b_, h, i: (b_, 0, ATTN_HEADS + h)),
                  pl.BlockSpec((1, s, HEAD_W), lambda b_, h, i: (b_, 0, 2 * ATTN_HEADS + h))],
        out_specs=pl.BlockSpec((1, tq, HEAD_W), lambda b_, h, i: (b_, i, h)),
        out_shape=jax.ShapeDtypeStruct((b, s, ATTN_HEADS * HEAD_W), BF16),
        compiler_params=_params("parallel", "parallel", "arbitrary"),
        name="diff_attn",
    )(slopes, lambda_qk, subln_g, qkv, qkv, qkv)


def _gelu(x):
    return jax.nn.gelu(x)


def _lru_kernel(xr_ref, gr_ref, cw_ref, cb_ref, gaw_ref, gab_ref, gxw_ref, gxb_ref, lam_ref,
                o_ref, tail_scr, h_scr, a_scr, u_scr, *, ts):
    si = pl.program_id(1)

    @pl.when(si == 0)
    def _():
        tail_scr[...] = jnp.zeros_like(tail_scr)
        h_scr[...] = jnp.zeros_like(h_scr)

    x = xr_ref[0]
    w = x.shape[1]
    tail = tail_scr[...]
    row8 = lax.broadcasted_iota(jnp.int32, (SUBLANES, w), 0)
    cw = cw_ref[...]
    xb = cb_ref[...]
    for d in range(CONV_W - 1, 0, -1):
        xs = pltpu.roll(x, d, 0)
        top = jnp.where(row8 < d, pltpu.roll(tail, d, 0), xs[:SUBLANES])
        xs = jnp.concatenate([top, xs[SUBLANES:]], axis=0)
        xb = xb + xs * cw[CONV_W - 1 - d:CONV_W - d]
    xb = xb + x * cw[CONV_W - 1:CONV_W]
    tail_scr[...] = x[ts - SUBLANES:]

    xbb = xb.astype(BF16)
    bw = w // LRU_BLOCKS

    def gate(w_ref, b_ref):
        parts = [jnp.dot(xbb[:, g * bw:(g + 1) * bw], w_ref[g], preferred_element_type=F32)
                 for g in range(LRU_BLOCKS)]
        return jax.nn.sigmoid(jnp.concatenate(parts, axis=1) + b_ref[...])

    r = gate(gaw_ref, gab_ref)
    i = gate(gxw_ref, gxb_ref)
    lam = lam_ref[...]
    softplus_neg = jnp.maximum(-lam, 0.0) + jnp.log1p(jnp.exp(-jnp.abs(lam)))
    log_a = -LRU_C * r * softplus_neg
    a = jnp.exp(log_a)
    mult = jnp.sqrt(1.0 - jnp.exp(2.0 * log_a))
    rows = lax.broadcasted_iota(jnp.int32, (ts, w), 0)
    mult = jnp.where(jnp.logical_and(rows == 0, si == 0), 1.0, mult)
    a_scr[...] = a
    u_scr[...] = mult * (i * xb)

    def group(g, hprev):
        r0 = pl.multiple_of(g * SUBLANES, SUBLANES)
        ag = a_scr[pl.ds(r0, SUBLANES), :]
        ug = u_scr[pl.ds(r0, SUBLANES), :]
        for d in (1, 2, 4):
            keep = row8 >= d
            ug = jnp.where(keep, ag * pltpu.roll(ug, d, 0) + ug, ug)
            ag = jnp.where(keep, ag * pltpu.roll(ag, d, 0), ag)
        hg = ag * hprev + ug
        o_ref[0, pl.ds(r0, SUBLANES), :] = (hg * _gelu(gr_ref[0, pl.ds(r0, SUBLANES), :])).astype(o_ref.dtype)
        return jnp.broadcast_to(hg[SUBLANES - 1:SUBLANES, :], (SUBLANES, w))

    h_scr[...] = lax.fori_loop(0, ts // SUBLANES, group, h_scr[...])


def _rg_lru(rest, conv_w, conv_b, gaw, gab, gxw, gxb, lru_lambda, ts):
    b, s, _ = rest.shape
    w = conv_w.shape[1]
    kern = functools.partial(_lru_kernel, ts=ts)
    vec = pl.BlockSpec((1, w), lambda b_, i: (0, 0))
    blk = pl.BlockSpec(gaw.shape, lambda b_, i: (0, 0, 0))
    return pl.pallas_call(
        kern,
        grid=(b, s // ts),
        in_specs=[pl.BlockSpec((1, ts, w), lambda b_, i: (b_, i, 0)),
                  pl.BlockSpec((1, ts, w), lambda b_, i: (b_, i, 1)),
                  pl.BlockSpec((CONV_W, w), lambda b_, i: (0, 0)),
                  vec, blk, vec, blk, vec, vec],
        out_specs=pl.BlockSpec((1, ts, w), lambda b_, i: (b_, i, 0)),
        out_shape=jax.ShapeDtypeStruct((b, s, w), BF16),
        scratch_shapes=[pltpu.VMEM((SUBLANES, w), F32), pltpu.VMEM((SUBLANES, w), F32),
                        pltpu.VMEM((ts, w), F32), pltpu.VMEM((ts, w), F32)],
        compiler_params=_params("parallel", "arbitrary"),
        name="rg_lru",
    )(rest, rest, conv_w, conv_b, gaw, gab, gxw, gxb, lru_lambda)


def _layer_norm(z, g, b):
    mu = jnp.mean(z, axis=-1, keepdims=True)
    zc = z - mu
    var = jnp.mean(zc * zc, axis=-1, keepdims=True)
    return zc * lax.rsqrt(var + LN_EPS) * g + b


def _merge_kernel(ya_ref, yr_ref, ga_ref, gl_ref, x_ref, wa_ref, wl_ref, wo_ref, g_ref, b_ref,
                  o_ref, ob_ref, *, alpha):
    pa = jnp.dot(ya_ref[...], wa_ref[...], preferred_element_type=F32)
    pr = jnp.dot(yr_ref[...], wl_ref[...], preferred_element_type=F32)
    merged = jax.nn.sigmoid(ga_ref[...]) * pa + jax.nn.sigmoid(gl_ref[...]) * pr
    mix = jnp.dot(merged.astype(BF16), wo_ref[...], preferred_element_type=F32)
    y = _layer_norm(alpha * x_ref[...] + mix, g_ref[...], b_ref[...])
    o_ref[...] = y
    ob_ref[...] = y.astype(BF16)


def _merge_out(ya, yr, rest, x, wa, wl, wo, ln_g, ln_b, alpha, tm):
    t, d = x.shape
    kern = functools.partial(_merge_kernel, alpha=alpha)
    rows = lambda c: pl.BlockSpec((tm, d), lambda i: (i, c))
    full = pl.BlockSpec((d, d), lambda i: (0, 0))
    vec = pl.BlockSpec((1, d), lambda i: (0, 0))
    return pl.pallas_call(
        kern,
        grid=(t // tm,),
        in_specs=[rows(0), rows(0), rows(2), rows(3), rows(0), full, full, full, vec, vec],
        out_specs=[rows(0), rows(0)],
        out_shape=[jax.ShapeDtypeStruct((t, d), F32), jax.ShapeDtypeStruct((t, d), BF16)],
        compiler_params=_params("parallel"),
        name="merge_out_ln",
    )(ya, yr, rest, rest, x, wa, wl, wo, ln_g, ln_b)


def _cmpx(v, i, j):
    hi = jnp.maximum(v[i], v[j])
    v[j] = jnp.minimum(v[i], v[j])
    v[i] = hi


def _bitonic_merge_desc(v):
    n = len(v)
    j = n // 2
    while j >= 1:
        for i in range(n):
            if i & j == 0:
                _cmpx(v, i, i + j)
        j //= 2


def _bitonic_sort_desc(v):
    n = len(v)
    k = 2
    while k <= n:
        j = k // 2
        while j >= 1:
            for i in range(n):
                l = i ^ j
                if l > i:
                    if i & k == 0:
                        _cmpx(v, i, l)
                    else:
                        _cmpx(v, l, i)
            j //= 2
        k *= 2


def _top_merge(a, b):
    n = len(a)
    v = [jnp.maximum(a[i], b[n - 1 - i]) for i in range(n)]
    _bitonic_merge_desc(v)
    return v


_CAND_PAIRS = [(p, q) for p in range(PEER_TOPK) for q in range(PEER_TOPK)
               if (p + 1) * (q + 1) <= PEER_TOPK]


def _select_kernel(x_ref, wq_ref, sk_ref, s1_ref, s2_ref, bexp_ref, tau_ref, c1_ref,
                   q_scr, pk_scr, *, tt):
    qt = lax.dot_general(wq_ref[...], x_ref[...], (((1,), (1,)), ((), ())),
                         preferred_element_type=F32)
    q_scr[...] = qt.astype(BF16)
    dk = sk_ref.shape[3]
    n_grp = N_KEYS // SUBLANES
    s_refs = (s1_ref, s2_ref)

    def head(h, _):
        for c in range(2):
            r0 = pl.multiple_of((h * 2 + c) * dk, dk)
            sc = jnp.dot(sk_ref[h, c], q_scr[pl.ds(r0, dk), :], preferred_element_type=F32)
            s_refs[c][h] = sc
            v = [sc[g * SUBLANES:(g + 1) * SUBLANES, :] for g in range(n_grp)]
            _bitonic_sort_desc(v)
            for sh in (4, 2, 1):
                v = _top_merge(v, [pltpu.roll(e, sh, 0) for e in v])
            for p in range(PEER_TOPK):
                pk_scr[c, p, pl.ds(h, 1), :] = v[p][0:1, :]
        return 0

    lax.fori_loop(0, PEER_HEADS, head, 0)

    a = [pk_scr[0, p] for p in range(PEER_TOPK)]
    b = [pk_scr[1, p] for p in range(PEER_TOPK)]
    cands = [a[p] + b[q] for p, q in _CAND_PAIRS]
    neg = jnp.full_like(cands[0], -jnp.inf)
    pad = (-len(cands)) % PEER_TOPK
    blocks = cands + [neg] * pad
    top = None
    for i0 in range(0, len(blocks), PEER_TOPK):
        blk = blocks[i0:i0 + PEER_TOPK]
        _bitonic_sort_desc(blk)
        top = blk if top is None else _top_merge(top, blk)
    tau = top[PEER_TOPK - 1]
    mx = a[0] + b[0]
    z = jnp.zeros_like(tau)
    for cnd in cands:
        z = z + jnp.where(cnd >= tau, jnp.exp(cnd - mx), 0.0)
    tau_ref[...] = tau
    c1_ref[...] = a[0] + jnp.log(z)

    def head2(h, _):
        b0 = pk_scr[1, 0, pl.ds(h, 1), :]
        bexp_ref[h] = jnp.exp(s2_ref[h] - b0)
        return 0

    lax.fori_loop(0, PEER_HEADS, head2, 0)


def _peer_select(xb, wq_t, sk, tt):
    t, d = xb.shape
    nq = wq_t.shape[0]
    kern = functools.partial(_select_kernel, tt=tt)
    big = pl.BlockSpec((PEER_HEADS, N_KEYS, tt), lambda i: (0, 0, i))
    small = pl.BlockSpec((PEER_HEADS, tt), lambda i: (0, i))
    big_shape = jax.ShapeDtypeStruct((PEER_HEADS, N_KEYS, t), F32)
    small_shape = jax.ShapeDtypeStruct((PEER_HEADS, t), F32)
    return pl.pallas_call(
        kern,
        grid=(t // tt,),
        in_specs=[pl.BlockSpec((tt, d), lambda i: (i, 0)),
                  pl.BlockSpec((nq, d), lambda i: (0, 0)),
                  pl.BlockSpec(sk.shape, lambda i: (0, 0, 0, 0))],
        out_specs=[big, big, big, small, small],
        out_shape=[big_shape, big_shape, big_shape, small_shape, small_shape],
        scratch_shapes=[pltpu.VMEM((nq, tt), BF16),
                        pltpu.VMEM((2, PEER_TOPK, PEER_HEADS, tt), F32)],
        compiler_params=_params("parallel"),
        name="peer_select",
    )(xb, wq_t, sk)


def _peer_kernel(xb_ref, x_ref, u_ref, vt_ref, s1_ref, s2_ref, bexp_ref, tau_ref, c1_ref,
                 g_ref, b_ref, o_ref, acc_scr, w_scr, act_scr, *, te, tt, jb, alpha):
    e = pl.program_id(1)

    @pl.when(e == 0)
    def _():
        acc_scr[...] = jnp.zeros_like(acc_scr)

    act_scr[...] = lax.dot_general(u_ref[...], xb_ref[...], (((1,), (1,)), ((), ())),
                                   preferred_element_type=F32)
    rows_per_step = te // N_KEYS

    def row_block(ii, _):
        i = e * rows_per_step + ii
        s1rows = [s1_ref[h, pl.ds(i, 1), :] for h in range(PEER_HEADS)]
        arows = [jnp.exp(s1rows[h] - c1_ref[h:h + 1, :]) for h in range(PEER_HEADS)]
        taus = [tau_ref[h:h + 1, :] for h in range(PEER_HEADS)]

        def col_block(jj, _):
            j0 = pl.multiple_of(jj * jb, jb)
            gsum = jnp.zeros((jb, tt), F32)
            for h in range(PEER_HEADS):
                s = s1rows[h] + s2_ref[h, pl.ds(j0, jb), :]
                gsum = gsum + jnp.where(s >= taus[h], arows[h] * bexp_ref[h, pl.ds(j0, jb), :], 0.0)
            r0 = pl.multiple_of(ii * N_KEYS + j0, jb)
            w_scr[pl.ds(r0, jb), :] = (_gelu(act_scr[pl.ds(r0, jb), :]) * gsum).astype(BF16)
            return 0

        lax.fori_loop(0, N_KEYS // jb, col_block, 0)
        return 0

    lax.fori_loop(0, rows_per_step, row_block, 0)
    acc_scr[...] += jnp.dot(vt_ref[...], w_scr[...], preferred_element_type=F32)

    @pl.when(e == pl.num_programs(1) - 1)
    def _():
        ffn = acc_scr[...].T
        o_ref[...] = _layer_norm(alpha * x_ref[...] + ffn, g_ref[...], b_ref[...])


def _peer_mix(xb, x, u, vt, s1, s2, bexp, tau, c1, ln_g, ln_b, alpha, tt, te, jb):
    t, d = x.shape
    n_exp = u.shape[0]
    kern = functools.partial(_peer_kernel, te=te, tt=tt, jb=jb, alpha=alpha)
    big = pl.BlockSpec((PEER_HEADS, N_KEYS, tt), lambda i, e: (0, 0, i))
    small = pl.BlockSpec((PEER_HEADS, tt), lambda i, e: (0, i))
    vec = pl.BlockSpec((1, d), lambda i, e: (0, 0))
    return pl.pallas_call(
        kern,
        grid=(t // tt, n_exp // te),
        in_specs=[pl.BlockSpec((tt, d), lambda i, e: (i, 0)),
                  pl.BlockSpec((tt, d), lambda i, e: (i, 0)),
                  pl.BlockSpec((te, d), lambda i, e: (e, 0)),
                  pl.BlockSpec((d, te), lambda i, e: (0, e)),
                  big, big, big, small, small, vec, vec],
        out_specs=pl.BlockSpec((tt, d), lambda i, e: (i, 0)),
        out_shape=jax.ShapeDtypeStruct((t, d), F32),
        scratch_shapes=[pltpu.VMEM((d, tt), F32), pltpu.VMEM((te, tt), BF16),
                        pltpu.VMEM((te, tt), F32)],
        compiler_params=_params("parallel", "arbitrary"),
        name="peer_mix_ln",
    )(xb, x, u, vt, s1, s2, bexp, tau, c1, ln_g, ln_b)


def _tiles(b, s, t):
    return dict(mm_tm=min(1024, t), mm_tn=512, tq=min(256, s), tk=min(256, s),
                lru_ts=min(256, s), merge_tm=min(256, t), sel_tt=min(256, t),
                mix_tt=min(512, t), mix_te=1024, mix_jb=32)


def kernel(x, w_in, lambda_qk, subln_g, conv_w, conv_b, gate_a_w, gate_a_b, gate_x_w, gate_x_b,
           lru_lambda, w_br_attn, w_br_lru, w_out, ln1_g, ln1_b, peer_wq, peer_subkeys, peer_u,
           peer_v, ln2_g, ln2_b):
    b, s, d = x.shape
    t = b * s
    depth = w_in.shape[0]
    alpha = (2.0 * depth) ** 0.25
    attn_w = ATTN_HEADS * HEAD_W
    ts = _tiles(b, s, t)
    slopes = jnp.exp2(-8.0 * jnp.arange(1, ATTN_HEADS + 1, dtype=F32) / ATTN_HEADS)
    row = lambda a: a.reshape(1, -1)

    xf = x.reshape(t, d)
    for l in range(depth):
        lam_init = 0.8 - 0.6 * math.exp(-0.3 * l)
        w_l = w_in[l].astype(BF16)
        qkv = _matmul(xf, w_l[:, :3 * attn_w], BF16, ts["mm_tm"], ts["mm_tn"])
        rest = _matmul(xf, w_l[:, 3 * attn_w:], F32, ts["mm_tm"], ts["mm_tn"])
        ya = _diff_attention(qkv.reshape(b, s, 3 * attn_w), slopes, lambda_qk[l], row(subln_g[l]),
                             lam_init, ts["tq"], ts["tk"])
        yr = _rg_lru(rest.reshape(b, s, -1), conv_w[l], row(conv_b[l]), gate_a_w[l].astype(BF16),
                     row(gate_a_b[l]), gate_x_w[l].astype(BF16), row(gate_x_b[l]),
                     row(lru_lambda[l]), ts["lru_ts"])
        x1, x1b = _merge_out(ya.reshape(t, attn_w), yr.reshape(t, d), rest, xf,
                             w_br_attn[l].astype(BF16), w_br_lru[l].astype(BF16),
                             w_out[l].astype(BF16), row(ln1_g[l]), row(ln1_b[l]), alpha,
                             ts["merge_tm"])
        s1, s2, bexp, tau, c1 = _peer_select(x1b, peer_wq[l].T.astype(BF16),
                                             peer_subkeys[l].astype(BF16), ts["sel_tt"])
        xf = _peer_mix(x1b, x1, peer_u[l].astype(BF16), peer_v[l].T.astype(BF16), s1, s2, bexp,
                       tau, c1, row(ln2_g[l]), row(ln2_b[l]), alpha, ts["mix_tt"], ts["mix_te"],
                       ts["mix_jb"])
    return xf.reshape(b, s, d)
```

```python
import functools
import math

import jax
import jax.numpy as jnp
from jax import lax
from jax.experimental import pallas as pl
from jax.experimental.pallas import tpu as pltpu

F32 = jnp.float32
BF16 = jnp.bfloat16

ATTN_HEADS = 8
HEAD_DIM = 64
HEAD_W = 2 * HEAD_DIM
LRU_BLOCKS = 8
LRU_C = 8.0
CONV_W = 4
PEER_HEADS = 8
N_KEYS = 128
PEER_TOPK = 16
LN_EPS = 1e-5
RMS_EPS = 1e-6
SUBLANES = 8
PACKED_ROWS = 16
VMEM_LIMIT = 56 * 1024 * 1024

_NT = (((1,), (1,)), ((), ()))


def _params(*sem):
    return pltpu.CompilerParams(dimension_semantics=sem, vmem_limit_bytes=VMEM_LIMIT)


def _mm_kernel(x_ref, w_ref, o_ref):
    o_ref[...] = jnp.dot(x_ref[...].astype(BF16), w_ref[...],
                         preferred_element_type=F32).astype(o_ref.dtype)


def _matmul(x, w, out_dtype, tm, tn):
    t, k = x.shape
    n = w.shape[1]
    return pl.pallas_call(
        _mm_kernel,
        grid=(t // tm, n // tn),
        in_specs=[pl.BlockSpec((tm, k), lambda i, j: (i, 0)),
                  pl.BlockSpec((k, tn), lambda i, j: (0, j))],
        out_specs=pl.BlockSpec((tm, tn), lambda i, j: (i, j)),
        out_shape=jax.ShapeDtypeStruct((t, n), out_dtype),
        compiler_params=_params("parallel", "arbitrary"),
        name="in_proj",
    )(x, w)


def _vt_kernel(x_ref, w_ref, o_ref):
    vt = lax.dot_general(w_ref[...], x_ref[...].astype(BF16), _NT,
                         preferred_element_type=F32)
    o_ref[...] = vt.astype(o_ref.dtype).reshape(o_ref.shape)


def _value_proj_t(x, w_t, tk):
    t, d = x.shape
    return pl.pallas_call(
        _vt_kernel,
        grid=(t // tk,),
        in_specs=[pl.BlockSpec((tk, d), lambda i: (i, 0)),
                  pl.BlockSpec(w_t.shape, lambda i: (0, 0))],
        out_specs=pl.BlockSpec((ATTN_HEADS, 1, HEAD_W, tk), lambda i: (0, i, 0, 0)),
        out_shape=jax.ShapeDtypeStruct((ATTN_HEADS, t // tk, HEAD_W, tk), BF16),
        compiler_params=_params("parallel"),
        name="value_proj_t",
    )(x, w_t)


def _attn_kernel(slopes_ref, lqk_ref, g_ref, q_ref, k_ref, vt_ref, o_ref, acc_scr, *, tq, tk, hp, lam_init):
    hg = pl.program_id(1)
    qi = pl.program_id(2)
    lq = lqk_ref[...]
    lam = (jnp.exp(jnp.sum(lq[0:1] * lq[1:2], axis=1, keepdims=True))
           - jnp.exp(jnp.sum(lq[2:3] * lq[3:4], axis=1, keepdims=True)) + lam_init)

    lane = lax.broadcasted_iota(jnp.int32, (tq, HEAD_W), 1)
    rel0 = (lax.broadcasted_iota(jnp.int32, (tk, tq), 1)
            - lax.broadcasted_iota(jnp.int32, (tk, tq), 0))
    rel0f = rel0.astype(F32)
    slopes, qqs, biases = [], [], []
    for hh in range(hp):
        slope = slopes_ref[hg * hp + hh]
        qs = q_ref[0, :, hh * HEAD_W:(hh + 1) * HEAD_W] * (HEAD_DIM ** -0.5)
        zero = jnp.zeros_like(qs)
        qqs.append(jnp.concatenate([jnp.where(lane < HEAD_DIM, qs, zero),
                                    jnp.where(lane >= HEAD_DIM, qs, zero)], axis=0))
        slopes.append(slope)
        biases.append(slope * rel0f)
    acc_scr[...] = jnp.zeros_like(acc_scr)

    def step(j, carry, masked):
        start = pl.multiple_of(j * tk, tk)
        shift = qi * tq - j * tk
        out = []
        for hh in range(hp):
            m, l = carry[hh]
            k = k_ref[0, pl.ds(start, tk), hh * HEAD_W:(hh + 1) * HEAD_W]
            off = slopes[hh] * shift.astype(F32)
            bt = jnp.where(rel0 + shift >= 0, biases[hh], jnp.inf) if masked else biases[hh]
            st = lax.dot_general(k, qqs[hh], _NT, preferred_element_type=F32)
            st = st - jnp.concatenate([bt, bt], axis=1)
            m_new = jnp.maximum(m, jnp.max(st, axis=0, keepdims=True) - off)
            pt = jnp.exp(st - (m_new + off))
            a = jnp.exp(m - m_new)
            l = a * l + jnp.sum(pt, axis=0, keepdims=True)
            acc_scr[hh] = a * acc_scr[hh] + jnp.dot(vt_ref[hh, j], pt.astype(BF16),
                                                    preferred_element_type=F32)
            out.append((m_new, l))
        return tuple(out)

    n_full = (qi * tq) // tk
    n_all = ((qi + 1) * tq + tk - 1) // tk
    carry = tuple((jnp.full((1, 2 * tq), -jnp.inf, F32), jnp.zeros((1, 2 * tq), F32))
                  for _ in range(hp))
    carry = lax.fori_loop(0, n_full, functools.partial(step, masked=False), carry)
    carry = lax.fori_loop(n_full, n_all, functools.partial(step, masked=True), carry)

    for hh in range(hp):
        ot = acc_scr[hh] / carry[hh][1]
        o = (ot[:, :tq] - lam * ot[:, tq:]).T
        o = o * lax.rsqrt(jnp.mean(o * o, axis=-1, keepdims=True) + RMS_EPS)
        o = o * g_ref[...] * (1.0 - lam_init)
        o_ref[0, :, hh * HEAD_W:(hh + 1) * HEAD_W] = o.astype(o_ref.dtype)


def _diff_attention(qk, vt, slopes, lambda_qk, subln_g, lam_init, tq, tk, hp):
    b, s, _ = qk.shape
    kern = functools.partial(_attn_kernel, tq=tq, tk=tk, hp=hp, lam_init=lam_init)
    gw = hp * HEAD_W
    return pl.pallas_call(
        kern,
        grid=(b, ATTN_HEADS // hp, s // tq),
        in_specs=[pl.BlockSpec(memory_space=pltpu.SMEM),
                  pl.BlockSpec((4, HEAD_DIM), lambda b_, h, i: (0, 0)),
                  pl.BlockSpec((1, HEAD_W), lambda b_, h, i: (0, 0)),
                  pl.BlockSpec((1, tq, gw), lambda b_, h, i: (b_, i, h)),
                  pl.BlockSpec((1, s, gw), lambda b_, h, i: (b_, 0, ATTN_HEADS // hp + h)),
                  pl.BlockSpec((hp, s // tk, HEAD_W, tk), lambda b_, h, i: (h, b_, 0, 0))],
        out_specs=pl.BlockSpec((1, tq, gw), lambda b_, h, i: (b_, i, h)),
        out_shape=jax.ShapeDtypeStruct((b, s, ATTN_HEADS * HEAD_W), BF16),
        scratch_shapes=[pltpu.VMEM((hp, HEAD_W, 2 * tq), F32)],
        compiler_params=_params("parallel", "parallel", "arbitrary"),
        name="diff_attn",
    )(slopes, lambda_qk, subln_g, qk, qk, vt)


def _gelu(x):
    return jax.nn.gelu(x)


def _gelu_tanh(x):
    c = math.sqrt(2.0 / math.pi)
    hx = 0.5 * x
    return hx + hx * jnp.tanh(x * (c + (c * 0.044715) * (x * x)))


def _lru_kernel(xr_ref, gr_ref, cw_ref, cb_ref, gaw_ref, gab_ref, gxw_ref, gxb_ref, lam_ref,
                o_ref, tail_scr, h_scr, a_scr, u_scr, *, ts):
    si = pl.program_id(1)

    @pl.when(si == 0)
    def _():
        tail_scr[...] = jnp.zeros_like(tail_scr)
        h_scr[...] = jnp.zeros_like(h_scr)

    x = xr_ref[0]
    w = x.shape[1]
    tail = tail_scr[...]
    row8 = lax.broadcasted_iota(jnp.int32, (SUBLANES, w), 0)
    cw = cw_ref[...]
    xb = cb_ref[...]
    for d in range(CONV_W - 1, 0, -1):
        xs = pltpu.roll(x, d, 0)
        top = jnp.where(row8 < d, pltpu.roll(tail, d, 0), xs[:SUBLANES])
        xs = jnp.concatenate([top, xs[SUBLANES:]], axis=0)
        xb = xb + xs * cw[CONV_W - 1 - d:CONV_W - d]
    xb = xb + x * cw[CONV_W - 1:CONV_W]
    tail_scr[...] = x[ts - SUBLANES:]

    xbb = xb.astype(BF16)
    bw = w // LRU_BLOCKS

    def gate(w_ref, b_ref):
        parts = [jnp.dot(xbb[:, g * bw:(g + 1) * bw], w_ref[g], preferred_element_type=F32)
                 for g in range(LRU_BLOCKS)]
        return jax.nn.sigmoid(jnp.concatenate(parts, axis=1) + b_ref[...])

    r = gate(gaw_ref, gab_ref)
    i = gate(gxw_ref, gxb_ref)
    lam = lam_ref[...]
    softplus_neg = jnp.maximum(-lam, 0.0) + jnp.log1p(jnp.exp(-jnp.abs(lam)))
    log_a = -LRU_C * r * softplus_neg
    a = jnp.exp(log_a)
    mult = jnp.sqrt(1.0 - jnp.exp(2.0 * log_a))
    rows = lax.broadcasted_iota(jnp.int32, (ts, w), 0)
    mult = jnp.where(jnp.logical_and(rows == 0, si == 0), 1.0, mult)
    a_scr[...] = a
    u_scr[...] = mult * (i * xb)

    def group(g, hprev):
        r0 = pl.multiple_of(g * SUBLANES, SUBLANES)
        ag = a_scr[pl.ds(r0, SUBLANES), :]
        ug = u_scr[pl.ds(r0, SUBLANES), :]
        for d in (1, 2, 4):
            keep = row8 >= d
            ug = jnp.where(keep, ag * pltpu.roll(ug, d, 0) + ug, ug)
            ag = jnp.where(keep, ag * pltpu.roll(ag, d, 0), ag)
        hg = ag * hprev + ug
        o_ref[0, pl.ds(r0, SUBLANES), :] = (hg * _gelu(gr_ref[0, pl.ds(r0, SUBLANES), :])).astype(o_ref.dtype)
        return jnp.broadcast_to(hg[SUBLANES - 1:SUBLANES, :], (SUBLANES, w))

    h_scr[...] = lax.fori_loop(0, ts // SUBLANES, group, h_scr[...])


def _rg_lru(rest, conv_w, conv_b, gaw, gab, gxw, gxb, lru_lambda, ts):
    b, s, _ = rest.shape
    w = conv_w.shape[1]
    kern = functools.partial(_lru_kernel, ts=ts)
    vec = pl.BlockSpec((1, w), lambda b_, i: (0, 0))
    blk = pl.BlockSpec(gaw.shape, lambda b_, i: (0, 0, 0))
    return pl.pallas_call(
        kern,
        grid=(b, s // ts),
        in_specs=[pl.BlockSpec((1, ts, w), lambda b_, i: (b_, i, 0)),
                  pl.BlockSpec((1, ts, w), lambda b_, i: (b_, i, 1)),
                  pl.BlockSpec((CONV_W, w), lambda b_, i: (0, 0)),
                  vec, blk, vec, blk, vec, vec],
        out_specs=pl.BlockSpec((1, ts, w), lambda b_, i: (b_, i, 0)),
        out_shape=jax.ShapeDtypeStruct((b, s, w), BF16),
        scratch_shapes=[pltpu.VMEM((SUBLANES, w), F32), pltpu.VMEM((SUBLANES, w), F32),
                        pltpu.VMEM((ts, w), F32), pltpu.VMEM((ts, w), F32)],
        compiler_params=_params("parallel", "arbitrary"),
        name="rg_lru",
    )(rest, rest, conv_w, conv_b, gaw, gab, gxw, gxb, lru_lambda)


def _layer_norm(z, g, b):
    mu = jnp.mean(z, axis=-1, keepdims=True)
    zc = z - mu
    var = jnp.mean(zc * zc, axis=-1, keepdims=True)
    return zc * lax.rsqrt(var + LN_EPS) * g + b


def _merge_kernel(ya_ref, yr_ref, ga_ref, gl_ref, x_ref, wa_ref, wl_ref, wo_ref, g_ref, b_ref,
                  o_ref, ob_ref, *, alpha):
    pa = jnp.dot(ya_ref[...], wa_ref[...], preferred_element_type=F32)
    pr = jnp.dot(yr_ref[...], wl_ref[...], preferred_element_type=F32)
    merged = jax.nn.sigmoid(ga_ref[...]) * pa + jax.nn.sigmoid(gl_ref[...]) * pr
    mix = jnp.dot(merged.astype(BF16), wo_ref[...], preferred_element_type=F32)
    y = _layer_norm(alpha * x_ref[...] + mix, g_ref[...], b_ref[...])
    o_ref[...] = y
    ob_ref[...] = y.astype(BF16)


def _merge_out(ya, yr, rest, x, wa, wl, wo, ln_g, ln_b, alpha, tm):
    t, d = x.shape
    kern = functools.partial(_merge_kernel, alpha=alpha)
    rows = lambda c: pl.BlockSpec((tm, d), lambda i: (i, c))
    full = pl.BlockSpec((d, d), lambda i: (0, 0))
    vec = pl.BlockSpec((1, d), lambda i: (0, 0))
    return pl.pallas_call(
        kern,
        grid=(t // tm,),
        in_specs=[rows(0), rows(0), rows(2), rows(3), rows(0), full, full, full, vec, vec],
        out_specs=[rows(0), rows(0)],
        out_shape=[jax.ShapeDtypeStruct((t, d), F32), jax.ShapeDtypeStruct((t, d), BF16)],
        compiler_params=_params("parallel"),
        name="merge_out_ln",
    )(ya, yr, rest, rest, x, wa, wl, wo, ln_g, ln_b)


def _cmpx(v, i, j):
    hi = jnp.maximum(v[i], v[j])
    v[j] = jnp.minimum(v[i], v[j])
    v[i] = hi


def _bitonic_merge_desc(v):
    n = len(v)
    j = n // 2
    while j >= 1:
        for i in range(n):
            if i & j == 0:
                _cmpx(v, i, i + j)
        j //= 2


def _bitonic_sort_desc(v):
    n = len(v)
    k = 2
    while k <= n:
        j = k // 2
        while j >= 1:
            for i in range(n):
                l = i ^ j
                if l > i:
                    if i & k == 0:
                        _cmpx(v, i, l)
                    else:
                        _cmpx(v, l, i)
            j //= 2
        k *= 2


def _top_merge(a, b):
    n = len(a)
    v = [jnp.maximum(a[i], b[n - 1 - i]) for i in range(n)]
    _bitonic_merge_desc(v)
    return v


_CAND_PAIRS = [(p, q) for p in range(PEER_TOPK) for q in range(PEER_TOPK)
               if (p + 1) * (q + 1) <= PEER_TOPK]


def _select_kernel(x_ref, wq_ref, sk_ref, rank_ref, bexp_ref, cnt_ref, arow_ref,
                   q_scr, s_scr, pk_scr, t_scr, *, tt):
    qt = lax.dot_general(wq_ref[...], x_ref[...], _NT, preferred_element_type=F32)
    q_scr[...] = qt.astype(BF16)
    dk = sk_ref.shape[3]
    n_grp = N_KEYS // SUBLANES

    def head(h, _):
        for c in range(2):
            r0 = pl.multiple_of((h * 2 + c) * dk, dk)
            sc = jnp.dot(sk_ref[h, c], q_scr[pl.ds(r0, dk), :], preferred_element_type=F32)
            s_scr[c, h] = sc
            v = [sc[g * SUBLANES:(g + 1) * SUBLANES, :] for g in range(n_grp)]
            _bitonic_sort_desc(v)
            for sh in (4, 2, 1):
                v = _top_merge(v, [pltpu.roll(e, sh, 0) for e in v])
            for p in range(PEER_TOPK):
                pk_scr[c, p, pl.ds(h, 1), :] = v[p][0:1, :]
        return 0

    lax.fori_loop(0, PEER_HEADS, head, 0)

    a = [pk_scr[0, p] for p in range(PEER_TOPK)]
    b = [pk_scr[1, p] for p in range(PEER_TOPK)]
    cands = [a[p] + b[q] for p, q in _CAND_PAIRS]
    neg = jnp.full_like(cands[0], -jnp.inf)
    blocks = cands + [neg] * ((-len(cands)) % PEER_TOPK)
    top = None
    for i0 in range(0, len(blocks), PEER_TOPK):
        blk = blocks[i0:i0 + PEER_TOPK]
        _bitonic_sort_desc(blk)
        top = blk if top is None else _top_merge(top, blk)
    tau = top[PEER_TOPK - 1]
    mx = a[0] + b[0]
    z = jnp.zeros_like(tau)
    for cnd in cands:
        z = z + jnp.where(cnd >= tau, jnp.exp(cnd - mx), 0.0)
    t_scr[0] = tau
    t_scr[1] = a[0] + jnp.log(z)

    def head2(h, _):
        s1 = s_scr[0, h]
        s2 = s_scr[1, h]
        tau_h = t_scr[0, pl.ds(h, 1), :]
        rank = jnp.zeros_like(s2)
        cnt = jnp.zeros_like(s1)
        for q in range(PEER_TOPK):
            bq = pk_scr[1, q, pl.ds(h, 1), :]
            rank = rank + jnp.where(bq > s2, 1.0, 0.0)
            cnt = cnt + jnp.where(s1 + bq >= tau_h, 1.0, 0.0)
        rank_ref[h] = rank.astype(BF16)
        cnt_ref[h] = cnt
        arow_ref[h] = jnp.exp(s1 - t_scr[1, pl.ds(h, 1), :])
        bexp_ref[h] = jnp.exp(s2 - pk_scr[1, 0, pl.ds(h, 1), :]).astype(BF16)
        return 0

    lax.fori_loop(0, PEER_HEADS, head2, 0)


def _peer_select(xb, wq_t, sk, tt):
    t, d = xb.shape
    nq = wq_t.shape[0]
    kern = functools.partial(_select_kernel, tt=tt)
    big = pl.BlockSpec((PEER_HEADS, N_KEYS, tt), lambda i: (0, 0, i))
    shape = lambda dt: jax.ShapeDtypeStruct((PEER_HEADS, N_KEYS, t), dt)
    return pl.pallas_call(
        kern,
        grid=(t // tt,),
        in_specs=[pl.BlockSpec((tt, d), lambda i: (i, 0)),
                  pl.BlockSpec((nq, d), lambda i: (0, 0)),
                  pl.BlockSpec(sk.shape, lambda i: (0, 0, 0, 0))],
        out_specs=[big, big, big, big],
        out_shape=[shape(BF16), shape(BF16), shape(F32), shape(F32)],
        scratch_shapes=[pltpu.VMEM((nq, tt), BF16),
                        pltpu.VMEM((2, PEER_HEADS, N_KEYS, tt), F32),
                        pltpu.VMEM((2, PEER_TOPK, PEER_HEADS, tt), F32),
                        pltpu.VMEM((2, PEER_HEADS, tt), F32)],
        compiler_params=_params("parallel"),
        name="peer_select",
    )(xb, wq_t, sk)


def _peer_kernel(xb_ref, x_ref, u_ref, vt_ref, rank_ref, bexp_ref, cnt_ref, arow_ref,
                 g_ref, b_ref, o_ref, acc_scr, act0, act1, w0, w1, *, tt, nblk, eb, alpha):
    e = pl.program_id(1)

    @pl.when(e == 0)
    def _():
        acc_scr[...] = jnp.zeros_like(acc_scr)

    acts = (act0, act1)
    ws = (w0, w1)
    rows_per_blk = eb // N_KEYS

    def mm1(k):
        acts[k % 2][...] = lax.dot_general(u_ref[k * eb:(k + 1) * eb, :], xb_ref[...], _NT,
                                           preferred_element_type=F32)

    def mm2(p):
        acc_scr[...] += jnp.dot(vt_ref[p], ws[p % 2][...], preferred_element_type=F32)

    def gates(k):
        act, w = acts[k % 2], ws[(k // 2) % 2]
        for ii in range(rows_per_blk):
            i = k * rows_per_blk + ii
            bcast = lambda ref, h: jnp.broadcast_to(ref[h, i:i + 1, :], (PACKED_ROWS, tt)).astype(BF16)
            cnts = [bcast(cnt_ref, h) for h in range(PEER_HEADS)]
            arows = [bcast(arow_ref, h) for h in range(PEER_HEADS)]
            for j0 in range(0, N_KEYS, PACKED_ROWS):
                gsum = jnp.zeros((PACKED_ROWS, tt), BF16)
                for h in range(PEER_HEADS):
                    sel = rank_ref[h, j0:j0 + PACKED_ROWS, :] < cnts[h]
                    gsum = gsum + jnp.where(sel, arows[h] * bexp_ref[h, j0:j0 + PACKED_ROWS, :], 0.0)
                r = ii * N_KEYS + j0
                rw = (k % 2) * eb + r
                w[rw:rw + PACKED_ROWS, :] = _gelu_tanh(act[r:r + PACKED_ROWS, :]).astype(BF16) * gsum

    mm1(0)
    for k in range(nblk):
        if k + 1 < nblk:
            mm1(k + 1)
        gates(k)
        if k % 2 == 1:
            mm2(k // 2)

    @pl.when(e == pl.num_programs(1) - 1)
    def _():
        ffn = acc_scr[...].T
        o_ref[...] = _layer_norm(alpha * x_ref[...] + ffn, g_ref[...], b_ref[...])


def _peer_mix(xb, x, u, vt, rank, bexp, cnt, arow, ln_g, ln_b, alpha, tt, te, eb):
    t, d = x.shape
    n_exp = u.shape[0]
    nblk = te // eb
    kern = functools.partial(_peer_kernel, tt=tt, nblk=nblk, eb=eb, alpha=alpha)
    big = pl.BlockSpec((PEER_HEADS, N_KEYS, tt), lambda i, e: (0, 0, i))
    rows = pl.BlockSpec((PEER_HEADS, te // N_KEYS, tt), lambda i, e: (0, e, i))
    vec = pl.BlockSpec((1, d), lambda i, e: (0, 0))
    return pl.pallas_call(
        kern,
        grid=(t // tt, n_exp // te),
        in_specs=[pl.BlockSpec((tt, d), lambda i, e: (i, 0)),
                  pl.BlockSpec((tt, d), lambda i, e: (i, 0)),
                  pl.BlockSpec((te, d), lambda i, e: (e, 0)),
                  pl.BlockSpec((nblk // 2, d, 2 * eb), lambda i, e: (e, 0, 0)),
                  big, big, rows, rows, vec, vec],
        out_specs=pl.BlockSpec((tt, d), lambda i, e: (i, 0)),
        out_shape=jax.ShapeDtypeStruct((t, d), F32),
        scratch_shapes=[pltpu.VMEM((d, tt), F32),
                        pltpu.VMEM((eb, tt), F32), pltpu.VMEM((eb, tt), F32),
                        pltpu.VMEM((2 * eb, tt), BF16), pltpu.VMEM((2 * eb, tt), BF16)],
        compiler_params=_params("parallel", "arbitrary"),
        name="peer_mix_ln",
    )(xb, x, u, vt, rank, bexp, cnt, arow, ln_g, ln_b)


def _tiles(b, s, t):
    return dict(mm_tm=min(1024, t), mm_tn=512, tq=min(256, s), tk=min(256, s), attn_hp=4,
                lru_ts=min(256, s), merge_tm=min(256, t), sel_tt=min(256, t),
                mix_tt=min(512, t), mix_te=2048, mix_eb=256)


def kernel(x, w_in, lambda_qk, subln_g, conv_w, conv_b, gate_a_w, gate_a_b, gate_x_w, gate_x_b,
           lru_lambda, w_br_attn, w_br_lru, w_out, ln1_g, ln1_b, peer_wq, peer_subkeys, peer_u,
           peer_v, ln2_g, ln2_b):
    b, s, d = x.shape
    t = b * s
    depth = w_in.shape[0]
    alpha = (2.0 * depth) ** 0.25
    attn_w = ATTN_HEADS * HEAD_W
    ts = _tiles(b, s, t)
    slopes = jnp.exp2(-8.0 * jnp.arange(1, ATTN_HEADS + 1, dtype=F32) / ATTN_HEADS)
    row = lambda a: a.reshape(1, -1)
    eb = ts["mix_eb"]

    xf = x.reshape(t, d)
    for l in range(depth):
        lam_init = 0.8 - 0.6 * math.exp(-0.3 * l)
        w_l = w_in[l].astype(BF16)
        qk = _matmul(xf, w_l[:, :2 * attn_w], BF16, ts["mm_tm"], ts["mm_tn"])
        vt = _value_proj_t(xf, w_l[:, 2 * attn_w:3 * attn_w].T, ts["tk"])
        rest = _matmul(xf, w_l[:, 3 * attn_w:], F32, ts["mm_tm"], ts["mm_tn"])
        ya = _diff_attention(qk.reshape(b, s, 2 * attn_w), vt, slopes, lambda_qk[l], row(subln_g[l]),
                             lam_init, ts["tq"], ts["tk"], ts["attn_hp"])
        yr = _rg_lru(rest.reshape(b, s, -1), conv_w[l], row(conv_b[l]), gate_a_w[l].astype(BF16),
                     row(gate_a_b[l]), gate_x_w[l].astype(BF16), row(gate_x_b[l]),
                     row(lru_lambda[l]), ts["lru_ts"])
        x1, x1b = _merge_out(ya.reshape(t, attn_w), yr.reshape(t, d), rest, xf,
                             w_br_attn[l].astype(BF16), w_br_lru[l].astype(BF16),
                             w_out[l].astype(BF16), row(ln1_g[l]), row(ln1_b[l]), alpha,
                             ts["merge_tm"])
        rank, bexp, cnt, arow = _peer_select(x1b, peer_wq[l].T.astype(BF16),
                                             peer_subkeys[l].astype(BF16), ts["sel_tt"])
        v_blocks = peer_v[l].astype(BF16).reshape(-1, 2 * eb, d).transpose(0, 2, 1)
        xf = _peer_mix(x1b, x1, peer_u[l].astype(BF16), v_blocks, rank, bexp, cnt, arow,
                       row(ln2_g[l]), row(ln2_b[l]), alpha, ts["mix_tt"], ts["mix_te"], eb)
    return xf.reshape(b, s, d)
```

```python
import functools
import math

import jax
import jax.numpy as jnp
from jax import lax
from jax.experimental import pallas as pl
from jax.experimental.pallas import tpu as pltpu

F32 = jnp.float32
BF16 = jnp.bfloat16

ATTN_HEADS = 8
HEAD_DIM = 64
HEAD_W = 2 * HEAD_DIM
LRU_BLOCKS = 8
LRU_C = 8.0
CONV_W = 4
PEER_HEADS = 8
N_KEYS = 128
PEER_TOPK = 16
LN_EPS = 1e-5
RMS_EPS = 1e-6
SUBLANES = 8
PACKED_ROWS = 16
VMEM_LIMIT = 56 * 1024 * 1024

_NT = (((1,), (1,)), ((), ()))


def _params(*sem):
    return pltpu.CompilerParams(dimension_semantics=sem, vmem_limit_bytes=VMEM_LIMIT)


def _mm_kernel(x_ref, w_ref, o_ref):
    o_ref[...] = jnp.dot(x_ref[...].astype(BF16), w_ref[...],
                         preferred_element_type=F32).astype(o_ref.dtype)


def _matmul(x, w, out_dtype, tm, tn):
    t, k = x.shape
    n = w.shape[1]
    return pl.pallas_call(
        _mm_kernel,
        grid=(t // tm, n // tn),
        in_specs=[pl.BlockSpec((tm, k), lambda i, j: (i, 0)),
                  pl.BlockSpec((k, tn), lambda i, j: (0, j))],
        out_specs=pl.BlockSpec((tm, tn), lambda i, j: (i, j)),
        out_shape=jax.ShapeDtypeStruct((t, n), out_dtype),
        compiler_params=_params("parallel", "arbitrary"),
        name="in_proj",
    )(x, w)


def _vt_kernel(x_ref, w_ref, o_ref):
    vt = lax.dot_general(w_ref[...], x_ref[...].astype(BF16), _NT,
                         preferred_element_type=F32)
    o_ref[...] = vt.astype(o_ref.dtype).reshape(o_ref.shape)


def _value_proj_t(x, w_t, tk):
    t, d = x.shape
    return pl.pallas_call(
        _vt_kernel,
        grid=(t // tk,),
        in_specs=[pl.BlockSpec((tk, d), lambda i: (i, 0)),
                  pl.BlockSpec(w_t.shape, lambda i: (0, 0))],
        out_specs=pl.BlockSpec((ATTN_HEADS, 1, HEAD_W, tk), lambda i: (0, i, 0, 0)),
        out_shape=jax.ShapeDtypeStruct((ATTN_HEADS, t // tk, HEAD_W, tk), BF16),
        compiler_params=_params("parallel"),
        name="value_proj_t",
    )(x, w_t)


def _attn_kernel(slopes_ref, lqk_ref, g_ref, q_ref, k_ref, vt_ref, o_ref, *scr, tb, hp, lam_init):
    acc_scr, st_scr = scr[:hp], scr[hp:]
    hg = pl.program_id(1)
    qi = pl.program_id(2)
    lq = lqk_ref[...]
    lam = (jnp.exp(jnp.sum(lq[0:1] * lq[1:2], axis=1, keepdims=True))
           - jnp.exp(jnp.sum(lq[2:3] * lq[3:4], axis=1, keepdims=True)) + lam_init)

    lane = lax.broadcasted_iota(jnp.int32, (tb, HEAD_W), 1)
    rel0 = (lax.broadcasted_iota(jnp.int32, (tb, tb), 1)
            - lax.broadcasted_iota(jnp.int32, (tb, tb), 0))
    rel0f = rel0.astype(F32)
    slopes, qqs, biases = [], [], []
    for hh in range(hp):
        slope = slopes_ref[hg * hp + hh]
        qs = q_ref[0, :, hh * HEAD_W:(hh + 1) * HEAD_W] * (HEAD_DIM ** -0.5)
        zero = jnp.zeros_like(qs)
        qqs.append(jnp.concatenate([jnp.where(lane < HEAD_DIM, qs, zero),
                                    jnp.where(lane >= HEAD_DIM, qs, zero)], axis=0))
        slopes.append(slope)
        biases.append(slope * rel0f)
        acc_scr[hh][...] = jnp.zeros_like(acc_scr[hh])

    def scores(j, hh):
        start = pl.multiple_of(j * tb, tb)
        k = k_ref[0, pl.ds(start, tb), hh * HEAD_W:(hh + 1) * HEAD_W]
        st_scr[hh][...] = lax.dot_general(k, qqs[hh], _NT, preferred_element_type=F32)

    def step(j, carry, last):
        shift = (qi - j) * tb
        out = []
        for hh in range(hp):
            m, l = carry[hh]
            off = slopes[hh] * shift.astype(F32)
            bt = jnp.where(rel0 >= 0, biases[hh], jnp.inf) if last else biases[hh]
            st = st_scr[hh][...] - jnp.concatenate([bt, bt], axis=1)
            m_new = jnp.maximum(m, jnp.max(st, axis=0, keepdims=True) - off)
            pt = jnp.exp(st - (m_new + off))
            a = jnp.exp(m - m_new)
            l = a * l + jnp.sum(pt, axis=0, keepdims=True)
            pt = pt.astype(BF16)
            if not last:
                scores(j + 1, hh)
            acc_scr[hh][...] = a * acc_scr[hh][...] + jnp.dot(vt_ref[hh, j], pt,
                                                              preferred_element_type=F32)
            out.append((m_new, l))
        return tuple(out)

    for hh in range(hp):
        scores(0, hh)
    carry = tuple((jnp.full((1, 2 * tb), -jnp.inf, F32), jnp.zeros((1, 2 * tb), F32))
                  for _ in range(hp))
    carry = lax.fori_loop(0, qi, functools.partial(step, last=False), carry)
    carry = step(qi, carry, last=True)

    for hh in range(hp):
        ot = acc_scr[hh][...] / carry[hh][1]
        o = (ot[:, :tb] - lam * ot[:, tb:]).T
        o = o * lax.rsqrt(jnp.mean(o * o, axis=-1, keepdims=True) + RMS_EPS)
        o = o * g_ref[...] * (1.0 - lam_init)
        o_ref[0, :, hh * HEAD_W:(hh + 1) * HEAD_W] = o.astype(o_ref.dtype)


def _diff_attention(qk, vt, slopes, lambda_qk, subln_g, lam_init, tb, hp):
    b, s, _ = qk.shape
    kern = functools.partial(_attn_kernel, tb=tb, hp=hp, lam_init=lam_init)
    gw = hp * HEAD_W
    return pl.pallas_call(
        kern,
        grid=(b, ATTN_HEADS // hp, s // tb),
        in_specs=[pl.BlockSpec(memory_space=pltpu.SMEM),
                  pl.BlockSpec((4, HEAD_DIM), lambda b_, h, i: (0, 0)),
                  pl.BlockSpec((1, HEAD_W), lambda b_, h, i: (0, 0)),
                  pl.BlockSpec((1, tb, gw), lambda b_, h, i: (b_, i, h)),
                  pl.BlockSpec((1, s, gw), lambda b_, h, i: (b_, 0, ATTN_HEADS // hp + h)),
                  pl.BlockSpec((hp, s // tb, HEAD_W, tb), lambda b_, h, i: (h, b_, 0, 0))],
        out_specs=pl.BlockSpec((1, tb, gw), lambda b_, h, i: (b_, i, h)),
        out_shape=jax.ShapeDtypeStruct((b, s, ATTN_HEADS * HEAD_W), BF16),
        scratch_shapes=([pltpu.VMEM((HEAD_W, 2 * tb), F32)] * hp
                        + [pltpu.VMEM((tb, 2 * tb), F32)] * hp),
        compiler_params=_params("parallel", "parallel", "arbitrary"),
        name="diff_attn",
    )(slopes, lambda_qk, subln_g, qk, qk, vt)


def _gelu(x):
    return jax.nn.gelu(x)


def _gelu_tanh(x):
    c = math.sqrt(2.0 / math.pi)
    hx = 0.5 * x
    return hx + hx * jnp.tanh(x * (c + (c * 0.044715) * (x * x)))


def _lru_kernel(xr_ref, gr_ref, cw_ref, cb_ref, gaw_ref, gab_ref, gxw_ref, gxb_ref, lam_ref,
                o_ref, tail_scr, h_scr, a_scr, u_scr, *, ts):
    si = pl.program_id(1)

    @pl.when(si == 0)
    def _():
        tail_scr[...] = jnp.zeros_like(tail_scr)
        h_scr[...] = jnp.zeros_like(h_scr)

    x = xr_ref[0]
    w = x.shape[1]
    tail = tail_scr[...]
    row8 = lax.broadcasted_iota(jnp.int32, (SUBLANES, w), 0)
    cw = cw_ref[...]
    xb = cb_ref[...]
    for d in range(CONV_W - 1, 0, -1):
        xs = pltpu.roll(x, d, 0)
        top = jnp.where(row8 < d, pltpu.roll(tail, d, 0), xs[:SUBLANES])
        xs = jnp.concatenate([top, xs[SUBLANES:]], axis=0)
        xb = xb + xs * cw[CONV_W - 1 - d:CONV_W - d]
    xb = xb + x * cw[CONV_W - 1:CONV_W]
    tail_scr[...] = x[ts - SUBLANES:]

    xbb = xb.astype(BF16)
    bw = w // LRU_BLOCKS

    def gate(w_ref, b_ref):
        parts = [jnp.dot(xbb[:, g * bw:(g + 1) * bw], w_ref[g], preferred_element_type=F32)
                 for g in range(LRU_BLOCKS)]
        return jax.nn.sigmoid(jnp.concatenate(parts, axis=1) + b_ref[...])

    r = gate(gaw_ref, gab_ref)
    i = gate(gxw_ref, gxb_ref)
    lam = lam_ref[...]
    softplus_neg = jnp.maximum(-lam, 0.0) + jnp.log1p(jnp.exp(-jnp.abs(lam)))
    log_a = -LRU_C * r * softplus_neg
    a = jnp.exp(log_a)
    mult = jnp.sqrt(1.0 - jnp.exp(2.0 * log_a))
    rows = lax.broadcasted_iota(jnp.int32, (ts, w), 0)
    mult = jnp.where(jnp.logical_and(rows == 0, si == 0), 1.0, mult)
    a_scr[...] = a
    u_scr[...] = mult * (i * xb)

    def group(g, hprev):
        r0 = pl.multiple_of(g * SUBLANES, SUBLANES)
        ag = a_scr[pl.ds(r0, SUBLANES), :]
        ug = u_scr[pl.ds(r0, SUBLANES), :]
        for d in (1, 2, 4):
            keep = row8 >= d
            ug = jnp.where(keep, ag * pltpu.roll(ug, d, 0) + ug, ug)
            ag = jnp.where(keep, ag * pltpu.roll(ag, d, 0), ag)
        hg = ag * hprev + ug
        o_ref[0, pl.ds(r0, SUBLANES), :] = (hg * _gelu(gr_ref[0, pl.ds(r0, SUBLANES), :])).astype(o_ref.dtype)
        return jnp.broadcast_to(hg[SUBLANES - 1:SUBLANES, :], (SUBLANES, w))

    h_scr[...] = lax.fori_loop(0, ts // SUBLANES, group, h_scr[...])


def _rg_lru(rest, conv_w, conv_b, gaw, gab, gxw, gxb, lru_lambda, ts):
    b, s, _ = rest.shape
    w = conv_w.shape[1]
    kern = functools.partial(_lru_kernel, ts=ts)
    vec = pl.BlockSpec((1, w), lambda b_, i: (0, 0))
    blk = pl.BlockSpec(gaw.shape, lambda b_, i: (0, 0, 0))
    return pl.pallas_call(
        kern,
        grid=(b, s // ts),
        in_specs=[pl.BlockSpec((1, ts, w), lambda b_, i: (b_, i, 0)),
                  pl.BlockSpec((1, ts, w), lambda b_, i: (b_, i, 1)),
                  pl.BlockSpec((CONV_W, w), lambda b_, i: (0, 0)),
                  vec, blk, vec, blk, vec, vec],
        out_specs=pl.BlockSpec((1, ts, w), lambda b_, i: (b_, i, 0)),
        out_shape=jax.ShapeDtypeStruct((b, s, w), BF16),
        scratch_shapes=[pltpu.VMEM((SUBLANES, w), F32), pltpu.VMEM((SUBLANES, w), F32),
                        pltpu.VMEM((ts, w), F32), pltpu.VMEM((ts, w), F32)],
        compiler_params=_params("parallel", "arbitrary"),
        name="rg_lru",
    )(rest, rest, conv_w, conv_b, gaw, gab, gxw, gxb, lru_lambda)


def _layer_norm(z, g, b):
    mu = jnp.mean(z, axis=-1, keepdims=True)
    zc = z - mu
    var = jnp.mean(zc * zc, axis=-1, keepdims=True)
    return zc * lax.rsqrt(var + LN_EPS) * g + b


def _merge_kernel(ya_ref, yr_ref, ga_ref, gl_ref, x_ref, wa_ref, wl_ref, wo_ref, g_ref, b_ref,
                  o_ref, ob_ref, *, alpha):
    pa = jnp.dot(ya_ref[...], wa_ref[...], preferred_element_type=F32)
    pr = jnp.dot(yr_ref[...], wl_ref[...], preferred_element_type=F32)
    merged = jax.nn.sigmoid(ga_ref[...]) * pa + jax.nn.sigmoid(gl_ref[...]) * pr
    mix = jnp.dot(merged.astype(BF16), wo_ref[...], preferred_element_type=F32)
    y = _layer_norm(alpha * x_ref[...] + mix, g_ref[...], b_ref[...])
    o_ref[...] = y
    ob_ref[...] = y.T.astype(BF16)


def _merge_out(ya, yr, rest, x, wa, wl, wo, ln_g, ln_b, alpha, tm):
    t, d = x.shape
    kern = functools.partial(_merge_kernel, alpha=alpha)
    rows = lambda c: pl.BlockSpec((tm, d), lambda i: (i, c))
    full = pl.BlockSpec((d, d), lambda i: (0, 0))
    vec = pl.BlockSpec((1, d), lambda i: (0, 0))
    return pl.pallas_call(
        kern,
        grid=(t // tm,),
        in_specs=[rows(0), rows(0), rows(2), rows(3), rows(0), full, full, full, vec, vec],
        out_specs=[rows(0), pl.BlockSpec((d, tm), lambda i: (0, i))],
        out_shape=[jax.ShapeDtypeStruct((t, d), F32), jax.ShapeDtypeStruct((d, t), BF16)],
        compiler_params=_params("parallel"),
        name="merge_out_ln",
    )(ya, yr, rest, rest, x, wa, wl, wo, ln_g, ln_b)


def _cmpx(v, i, j):
    hi = jnp.maximum(v[i], v[j])
    v[j] = jnp.minimum(v[i], v[j])
    v[i] = hi


def _bitonic_merge_desc(v):
    n = len(v)
    j = n // 2
    while j >= 1:
        for i in range(n):
            if i & j == 0:
                _cmpx(v, i, i + j)
        j //= 2


def _bitonic_sort_desc(v):
    n = len(v)
    k = 2
    while k <= n:
        j = k // 2
        while j >= 1:
            for i in range(n):
                l = i ^ j
                if l > i:
                    if i & k == 0:
                        _cmpx(v, i, l)
                    else:
                        _cmpx(v, l, i)
            j //= 2
        k *= 2


def _top_merge(a, b):
    n = len(a)
    v = [jnp.maximum(a[i], b[n - 1 - i]) for i in range(n)]
    _bitonic_merge_desc(v)
    return v


_CAND_PAIRS = [(p, q) for p in range(PEER_TOPK) for q in range(PEER_TOPK)
               if (p + 1) * (q + 1) <= PEER_TOPK]


def _select_kernel(x_ref, wq_ref, sk_ref, rank_ref, bexp_ref, cnt_ref, arow_ref,
                   q_scr, s_scr, pk_scr, t_scr, *, tt):
    qt = jnp.dot(wq_ref[...], x_ref[...], preferred_element_type=F32)
    q_scr[...] = qt.astype(BF16)
    dk = sk_ref.shape[3]
    n_grp = N_KEYS // SUBLANES

    def head(h, _):
        for c in range(2):
            r0 = pl.multiple_of((h * 2 + c) * dk, dk)
            sc = jnp.dot(sk_ref[h, c], q_scr[pl.ds(r0, dk), :], preferred_element_type=F32)
            s_scr[c, h] = sc
            v = [sc[g * SUBLANES:(g + 1) * SUBLANES, :] for g in range(n_grp)]
            _bitonic_sort_desc(v)
            for sh in (4, 2, 1):
                v = _top_merge(v, [pltpu.roll(e, sh, 0) for e in v])
            for p in range(PEER_TOPK):
                pk_scr[c, p, pl.ds(h, 1), :] = v[p][0:1, :]
        return 0

    lax.fori_loop(0, PEER_HEADS, head, 0)

    a = [pk_scr[0, p] for p in range(PEER_TOPK)]
    b = [pk_scr[1, p] for p in range(PEER_TOPK)]
    cands = [a[p] + b[q] for p, q in _CAND_PAIRS]
    neg = jnp.full_like(cands[0], -jnp.inf)
    blocks = cands + [neg] * ((-len(cands)) % PEER_TOPK)
    top = None
    for i0 in range(0, len(blocks), PEER_TOPK):
        blk = blocks[i0:i0 + PEER_TOPK]
        _bitonic_sort_desc(blk)
        top = blk if top is None else _top_merge(top, blk)
    tau = top[PEER_TOPK - 1]
    mx = a[0] + b[0]
    z = jnp.zeros_like(tau)
    for cnd in cands:
        z = z + jnp.where(cnd >= tau, jnp.exp(cnd - mx), 0.0)
    t_scr[0] = tau
    t_scr[1] = a[0] + jnp.log(z)

    def head2(h, _):
        s1 = s_scr[0, h]
        s2 = s_scr[1, h]
        tau_h = t_scr[0, pl.ds(h, 1), :]
        rank = jnp.zeros_like(s2)
        cnt = jnp.zeros_like(s1)
        for q in range(PEER_TOPK):
            bq = pk_scr[1, q, pl.ds(h, 1), :]
            rank = rank + jnp.where(bq > s2, 1.0, 0.0)
            cnt = cnt + jnp.where(s1 + bq >= tau_h, 1.0, 0.0)
        rank_ref[h] = rank.astype(BF16)
        cnt_ref[h] = cnt
        arow_ref[h] = jnp.exp(s1 - t_scr[1, pl.ds(h, 1), :])
        bexp_ref[h] = jnp.exp(s2 - pk_scr[1, 0, pl.ds(h, 1), :]).astype(BF16)
        return 0

    lax.fori_loop(0, PEER_HEADS, head2, 0)


def _peer_select(xt, wq_t, sk, tt):
    d, t = xt.shape
    nq = wq_t.shape[0]
    kern = functools.partial(_select_kernel, tt=tt)
    big = pl.BlockSpec((PEER_HEADS, N_KEYS, tt), lambda i: (0, 0, i))
    shape = lambda dt: jax.ShapeDtypeStruct((PEER_HEADS, N_KEYS, t), dt)
    return pl.pallas_call(
        kern,
        grid=(t // tt,),
        in_specs=[pl.BlockSpec((d, tt), lambda i: (0, i)),
                  pl.BlockSpec((nq, d), lambda i: (0, 0)),
                  pl.BlockSpec(sk.shape, lambda i: (0, 0, 0, 0))],
        out_specs=[big, big, big, big],
        out_shape=[shape(BF16), shape(BF16), shape(F32), shape(F32)],
        scratch_shapes=[pltpu.VMEM((nq, tt), BF16),
                        pltpu.VMEM((2, PEER_HEADS, N_KEYS, tt), F32),
                        pltpu.VMEM((2, PEER_TOPK, PEER_HEADS, tt), F32),
                        pltpu.VMEM((2, PEER_HEADS, tt), F32)],
        compiler_params=_params("parallel"),
        name="peer_select",
    )(xt, wq_t, sk)


def _peer_kernel(xt_ref, x_ref, u_ref, vt_ref, rank_ref, bexp_ref, cnt_ref, arow_ref,
                 g_ref, b_ref, o_ref, acc_scr, act0, act1, w0, w1, row_scr, *, tt, nblk, eb, alpha):
    e = pl.program_id(1)

    @pl.when(e == 0)
    def _():
        acc_scr[...] = jnp.zeros_like(acc_scr)

    acts = (act0, act1)
    ws = (w0, w1)
    rows_per_blk = eb // N_KEYS
    half = acc_scr.shape[0] // 2

    def mm1(k):
        acts[k % 2][...] = jnp.dot(u_ref[k * eb:(k + 1) * eb, :], xt_ref[...],
                                   preferred_element_type=F32)

    def mm2(p, part):
        rows = slice(part * half, (part + 1) * half)
        acc_scr[rows, :] += jnp.dot(vt_ref[p, rows, :], ws[p % 2][...], preferred_element_type=F32)

    def gates(k):
        act, w = acts[k % 2], ws[(k // 2) % 2]
        for ii in range(rows_per_blk):
            i = k * rows_per_blk + ii
            for h in range(PEER_HEADS):
                for a, ref in enumerate((cnt_ref, arow_ref)):
                    row_scr[a, h] = jnp.broadcast_to(ref[h, i:i + 1, :], (PACKED_ROWS, tt)).astype(BF16)
            for j0 in range(0, N_KEYS, PACKED_ROWS):
                r = ii * N_KEYS + j0
                rw = (k % 2) * eb + r
                w[rw:rw + PACKED_ROWS, :] = _gelu_tanh(act[r:r + PACKED_ROWS, :]).astype(BF16)
                gsum = None
                for h in range(PEER_HEADS):
                    sel = rank_ref[h, j0:j0 + PACKED_ROWS, :] < row_scr[0, h]
                    term = jnp.where(sel, row_scr[1, h] * bexp_ref[h, j0:j0 + PACKED_ROWS, :],
                                     jnp.zeros((), BF16))
                    gsum = term if gsum is None else gsum + term
                w[rw:rw + PACKED_ROWS, :] = w[rw:rw + PACKED_ROWS, :] * gsum

    mm1(0)
    for k in range(nblk):
        if k + 1 < nblk:
            mm1(k + 1)
        if k >= 2:
            mm2(k // 2 - 1, k % 2)
        gates(k)
    mm2(nblk // 2 - 1, 0)
    mm2(nblk // 2 - 1, 1)

    @pl.when(e == pl.num_programs(1) - 1)
    def _():
        ffn = acc_scr[...].T
        o_ref[...] = _layer_norm(alpha * x_ref[...] + ffn, g_ref[...], b_ref[...])


def _peer_mix(xt, x, u, vt, rank, bexp, cnt, arow, ln_g, ln_b, alpha, tt, te, eb):
    t, d = x.shape
    n_exp = u.shape[0]
    nblk = te // eb
    kern = functools.partial(_peer_kernel, tt=tt, nblk=nblk, eb=eb, alpha=alpha)
    big = pl.BlockSpec((PEER_HEADS, N_KEYS, tt), lambda i, e: (0, 0, i))
    rows = pl.BlockSpec((PEER_HEADS, te // N_KEYS, tt), lambda i, e: (0, e, i))
    vec = pl.BlockSpec((1, d), lambda i, e: (0, 0))
    return pl.pallas_call(
        kern,
        grid=(t // tt, n_exp // te),
        in_specs=[pl.BlockSpec((d, tt), lambda i, e: (0, i)),
                  pl.BlockSpec((tt, d), lambda i, e: (i, 0)),
                  pl.BlockSpec((te, d), lambda i, e: (e, 0)),
                  pl.BlockSpec((nblk // 2, d, 2 * eb), lambda i, e: (e, 0, 0)),
                  big, big, rows, rows, vec, vec],
        out_specs=pl.BlockSpec((tt, d), lambda i, e: (i, 0)),
        out_shape=jax.ShapeDtypeStruct((t, d), F32),
        scratch_shapes=[pltpu.VMEM((d, tt), F32),
                        pltpu.VMEM((eb, tt), F32), pltpu.VMEM((eb, tt), F32),
                        pltpu.VMEM((2 * eb, tt), BF16), pltpu.VMEM((2 * eb, tt), BF16),
                        pltpu.VMEM((2, PEER_HEADS, PACKED_ROWS, tt), BF16)],
        compiler_params=_params("parallel", "arbitrary"),
        name="peer_mix_ln",
    )(xt, x, u, vt, rank, bexp, cnt, arow, ln_g, ln_b)


def _tiles(b, s, t):
    return dict(mm_tm=min(1024, t), mm_tn=512, attn_tb=min(256, s), attn_hp=4,
                lru_ts=min(256, s), merge_tm=min(256, t), sel_tt=min(256, t),
                mix_tt=min(512, t), mix_te=2048, mix_eb=256)


def kernel(x, w_in, lambda_qk, subln_g, conv_w, conv_b, gate_a_w, gate_a_b, gate_x_w, gate_x_b,
           lru_lambda, w_br_attn, w_br_lru, w_out, ln1_g, ln1_b, peer_wq, peer_subkeys, peer_u,
           peer_v, ln2_g, ln2_b):
    b, s, d = x.shape
    t = b * s
    depth = w_in.shape[0]
    alpha = (2.0 * depth) ** 0.25
    attn_w = ATTN_HEADS * HEAD_W
    ts = _tiles(b, s, t)
    slopes = jnp.exp2(-8.0 * jnp.arange(1, ATTN_HEADS + 1, dtype=F32) / ATTN_HEADS)
    row = lambda a: a.reshape(1, -1)
    eb = ts["mix_eb"]

    xf = x.reshape(t, d)
    for l in range(depth):
        lam_init = 0.8 - 0.6 * math.exp(-0.3 * l)
        w_l = w_in[l].astype(BF16)
        qk = _matmul(xf, w_l[:, :2 * attn_w], BF16, ts["mm_tm"], ts["mm_tn"])
        vt = _value_proj_t(xf, w_l[:, 2 * attn_w:3 * attn_w].T, ts["attn_tb"])
        rest = _matmul(xf, w_l[:, 3 * attn_w:], F32, ts["mm_tm"], ts["mm_tn"])
        ya = _diff_attention(qk.reshape(b, s, 2 * attn_w), vt, slopes, lambda_qk[l], row(subln_g[l]),
                             lam_init, ts["attn_tb"], ts["attn_hp"])
        yr = _rg_lru(rest.reshape(b, s, -1), conv_w[l], row(conv_b[l]), gate_a_w[l].astype(BF16),
                     row(gate_a_b[l]), gate_x_w[l].astype(BF16), row(gate_x_b[l]),
                     row(lru_lambda[l]), ts["lru_ts"])
        x1, x1t = _merge_out(ya.reshape(t, attn_w), yr.reshape(t, d), rest, xf,
                             w_br_attn[l].astype(BF16), w_br_lru[l].astype(BF16),
                             w_out[l].astype(BF16), row(ln1_g[l]), row(ln1_b[l]), alpha,
                             ts["merge_tm"])
        rank, bexp, cnt, arow = _peer_select(x1t, peer_wq[l].T.astype(BF16),
                                             peer_subkeys[l].astype(BF16), ts["sel_tt"])
        v_blocks = peer_v[l].astype(BF16).reshape(-1, 2 * eb, d).transpose(0, 2, 1)
        xf = _peer_mix(x1t, x1, peer_u[l].astype(BF16), v_blocks, rank, bexp, cnt, arow,
                       row(ln2_g[l]), row(ln2_b[l]), alpha, ts["mix_tt"], ts["mix_te"], eb)
    return xf.reshape(b, s, d)
```

```python
import functools
import math

import jax
import jax.numpy as jnp
from jax import lax
from jax.experimental import pallas as pl
from jax.experimental.pallas import tpu as pltpu

F32 = jnp.float32
BF16 = jnp.bfloat16

ATTN_HEADS = 8
HEAD_DIM = 64
HEAD_W = 2 * HEAD_DIM
LRU_BLOCKS = 8
LRU_C = 8.0
CONV_W = 4
PEER_HEADS = 8
N_KEYS = 128
PEER_TOPK = 16
LN_EPS = 1e-5
RMS_EPS = 1e-6
PROJ_LRU_COL = 2
PROJ_GATE_COL = 4
SUBLANES = 8
PACKED_ROWS = 16
VMEM_LIMIT = 56 * 1024 * 1024

_NT = (((1,), (1,)), ((), ()))


def _params(*sem):
    return pltpu.CompilerParams(dimension_semantics=sem, vmem_limit_bytes=VMEM_LIMIT)


MM_ROWS = 256


def _mm_kernel(x_ref, w_ref, o_ref):
    for r in range(0, x_ref.shape[0], MM_ROWS):
        o_ref[r:r + MM_ROWS, :] = jnp.dot(x_ref[r:r + MM_ROWS, :].astype(BF16), w_ref[...],
                                          preferred_element_type=F32).astype(o_ref.dtype)


def _matmul(x, w, out_dtype, tm, tn):
    t, k = x.shape
    n = w.shape[1]
    return pl.pallas_call(
        _mm_kernel,
        grid=(t // tm, n // tn),
        in_specs=[pl.BlockSpec((tm, k), lambda i, j: (i, 0)),
                  pl.BlockSpec((k, tn), lambda i, j: (0, j))],
        out_specs=pl.BlockSpec((tm, tn), lambda i, j: (i, j)),
        out_shape=jax.ShapeDtypeStruct((t, n), out_dtype),
        compiler_params=_params("parallel", "arbitrary"),
        name="in_proj",
    )(x, w)


def _vt_kernel(x_ref, w_ref, o_ref):
    vt = lax.dot_general(w_ref[...], x_ref[...].astype(BF16), _NT,
                         preferred_element_type=F32)
    o_ref[...] = vt.astype(o_ref.dtype).reshape(o_ref.shape)


def _value_proj_t(x, w_t, tk):
    t, d = x.shape
    return pl.pallas_call(
        _vt_kernel,
        grid=(t // tk,),
        in_specs=[pl.BlockSpec((tk, d), lambda i: (i, 0)),
                  pl.BlockSpec(w_t.shape, lambda i: (0, 0))],
        out_specs=pl.BlockSpec((ATTN_HEADS, 1, HEAD_W, tk), lambda i: (0, i, 0, 0)),
        out_shape=jax.ShapeDtypeStruct((ATTN_HEADS, t // tk, HEAD_W, tk), BF16),
        compiler_params=_params("parallel"),
        name="value_proj_t",
    )(x, w_t)


def _attn_kernel(slopes_ref, lqk_ref, g_ref, q_ref, k_ref, vt_ref, o_ref, *scr, tb, hp, lam_init):
    acc_scr, st_scr = scr[:hp], scr[hp:]
    hg = pl.program_id(1)
    qi = pl.program_id(2)
    lq = lqk_ref[...]
    lam = (jnp.exp(jnp.sum(lq[0:1] * lq[1:2], axis=1, keepdims=True))
           - jnp.exp(jnp.sum(lq[2:3] * lq[3:4], axis=1, keepdims=True)) + lam_init)

    lane = lax.broadcasted_iota(jnp.int32, (tb, HEAD_W), 1)
    rel0 = (lax.broadcasted_iota(jnp.int32, (tb, tb), 1)
            - lax.broadcasted_iota(jnp.int32, (tb, tb), 0))
    rel0f = rel0.astype(F32)
    slopes, qqs, biases = [], [], []
    for hh in range(hp):
        slope = slopes_ref[hg * hp + hh]
        qs = q_ref[0, :, hh * HEAD_W:(hh + 1) * HEAD_W] * (HEAD_DIM ** -0.5)
        zero = jnp.zeros_like(qs)
        qqs.append(jnp.concatenate([jnp.where(lane < HEAD_DIM, qs, zero),
                                    jnp.where(lane >= HEAD_DIM, qs, zero)], axis=0))
        slopes.append(slope)
        biases.append(slope * rel0f)
        acc_scr[hh][...] = jnp.zeros_like(acc_scr[hh])

    def scores(j, hh):
        start = pl.multiple_of(j * tb, tb)
        k = k_ref[0, pl.ds(start, tb), hh * HEAD_W:(hh + 1) * HEAD_W]
        st_scr[hh][...] = lax.dot_general(k, qqs[hh], _NT, preferred_element_type=F32)

    def step(j, carry, last):
        shift = (qi - j) * tb
        out = []
        for hh in range(hp):
            m, l = carry[hh]
            off = slopes[hh] * shift.astype(F32)
            bt = jnp.where(rel0 >= 0, biases[hh], jnp.inf) if last else biases[hh]
            st = st_scr[hh][...] - jnp.concatenate([bt, bt], axis=1)
            m_new = jnp.maximum(m, jnp.max(st, axis=0, keepdims=True) - off)
            pt = jnp.exp(st - (m_new + off))
            a = jnp.exp(m - m_new)
            l = a * l + jnp.sum(pt, axis=0, keepdims=True)
            pt = pt.astype(BF16)
            if not last:
                scores(j + 1, hh)
            acc_scr[hh][...] = a * acc_scr[hh][...] + jnp.dot(vt_ref[hh, j], pt,
                                                              preferred_element_type=F32)
            out.append((m_new, l))
        return tuple(out)

    for hh in range(hp):
        scores(0, hh)
    carry = tuple((jnp.full((1, 2 * tb), -jnp.inf, F32), jnp.zeros((1, 2 * tb), F32))
                  for _ in range(hp))
    carry = lax.fori_loop(0, qi, functools.partial(step, last=False), carry)
    carry = step(qi, carry, last=True)

    for hh in range(hp):
        ot = acc_scr[hh][...] / carry[hh][1]
        o = (ot[:, :tb] - lam * ot[:, tb:]).T
        o = o * lax.rsqrt(jnp.mean(o * o, axis=-1, keepdims=True) + RMS_EPS)
        o = o * g_ref[...] * (1.0 - lam_init)
        o_ref[0, :, hh * HEAD_W:(hh + 1) * HEAD_W] = o.astype(o_ref.dtype)


def _diff_attention(qk, vt, slopes, lambda_qk, subln_g, lam_init, tb, hp):
    b, s, _ = qk.shape
    kern = functools.partial(_attn_kernel, tb=tb, hp=hp, lam_init=lam_init)
    gw = hp * HEAD_W
    return pl.pallas_call(
        kern,
        grid=(b, ATTN_HEADS // hp, s // tb),
        in_specs=[pl.BlockSpec(memory_space=pltpu.SMEM),
                  pl.BlockSpec((4, HEAD_DIM), lambda b_, h, i: (0, 0)),
                  pl.BlockSpec((1, HEAD_W), lambda b_, h, i: (0, 0)),
                  pl.BlockSpec((1, tb, gw), lambda b_, h, i: (b_, i, h)),
                  pl.BlockSpec((1, s, gw), lambda b_, h, i: (b_, 0, ATTN_HEADS // hp + h)),
                  pl.BlockSpec((hp, s // tb, HEAD_W, tb), lambda b_, h, i: (h, b_, 0, 0))],
        out_specs=pl.BlockSpec((1, tb, gw), lambda b_, h, i: (b_, i, h)),
        out_shape=jax.ShapeDtypeStruct((b, s, ATTN_HEADS * HEAD_W), BF16),
        scratch_shapes=([pltpu.VMEM((HEAD_W, 2 * tb), F32)] * hp
                        + [pltpu.VMEM((tb, 2 * tb), F32)] * hp),
        compiler_params=_params("parallel", "parallel", "arbitrary"),
        name="diff_attn",
    )(slopes, lambda_qk, subln_g, qk, qk, vt)


def _gelu(x):
    return jax.nn.gelu(x)


def _gelu_tanh(x):
    c = math.sqrt(2.0 / math.pi)
    hx = 0.5 * x
    return hx + hx * jnp.tanh(x * (c + (c * 0.044715) * (x * x)))


def _lru_kernel(xr_ref, gr_ref, cw_ref, cb_ref, gaw_ref, gab_ref, gxw_ref, gxb_ref, lam_ref,
                o_ref, tail_scr, h_scr, a_scr, u_scr, gg_scr, *, ts):
    si = pl.program_id(1)

    @pl.when(si == 0)
    def _():
        tail_scr[...] = jnp.zeros_like(tail_scr)
        h_scr[...] = jnp.zeros_like(h_scr)

    x = xr_ref[0].astype(F32)
    w = x.shape[1]
    tail = tail_scr[...]
    row8 = lax.broadcasted_iota(jnp.int32, (SUBLANES, w), 0)
    cw = cw_ref[...]
    xb = cb_ref[...]
    for d in range(CONV_W - 1, 0, -1):
        xs = pltpu.roll(x, d, 0)
        top = jnp.where(row8 < d, pltpu.roll(tail, d, 0), xs[:SUBLANES])
        xs = jnp.concatenate([top, xs[SUBLANES:]], axis=0)
        xb = xb + xs * cw[CONV_W - 1 - d:CONV_W - d]
    xb = xb + x * cw[CONV_W - 1:CONV_W]
    tail_scr[...] = x[ts - SUBLANES:]

    xbb = xb.astype(BF16)
    bw = w // LRU_BLOCKS

    def gate(w_ref, b_ref):
        parts = [jnp.dot(xbb[:, g * bw:(g + 1) * bw], w_ref[g], preferred_element_type=F32)
                 for g in range(LRU_BLOCKS)]
        return jax.nn.sigmoid(jnp.concatenate(parts, axis=1) + b_ref[...])

    r = gate(gaw_ref, gab_ref)
    i = gate(gxw_ref, gxb_ref)
    lam = lam_ref[...]
    softplus_neg = jnp.maximum(-lam, 0.0) + jnp.log1p(jnp.exp(-jnp.abs(lam)))
    log_a = -LRU_C * r * softplus_neg
    a = jnp.exp(log_a)
    mult = jnp.sqrt(1.0 - jnp.exp(2.0 * log_a))
    rows = lax.broadcasted_iota(jnp.int32, (ts, w), 0)
    mult = jnp.where(jnp.logical_and(rows == 0, si == 0), 1.0, mult)
    a_scr[...] = a
    u_scr[...] = mult * (i * xb)
    gg_scr[...] = _gelu(gr_ref[0].astype(F32))

    def group(g, hprev):
        r0 = pl.multiple_of(g * SUBLANES, SUBLANES)
        ag = a_scr[pl.ds(r0, SUBLANES), :]
        ug = u_scr[pl.ds(r0, SUBLANES), :]
        for d in (1, 2, 4):
            keep = row8 >= d
            ug = jnp.where(keep, ag * pltpu.roll(ug, d, 0) + ug, ug)
            ag = jnp.where(keep, ag * pltpu.roll(ag, d, 0), ag)
        hg = ag * hprev + ug
        o_ref[0, pl.ds(r0, SUBLANES), :] = (hg * gg_scr[pl.ds(r0, SUBLANES), :]).astype(o_ref.dtype)
        return jnp.broadcast_to(hg[SUBLANES - 1:SUBLANES, :], (SUBLANES, w))

    h_scr[...] = lax.fori_loop(0, ts // SUBLANES, group, h_scr[...])


def _rg_lru(rest, conv_w, conv_b, gaw, gab, gxw, gxb, lru_lambda, ts):
    b, s, _ = rest.shape
    w = conv_w.shape[1]
    kern = functools.partial(_lru_kernel, ts=ts)
    vec = pl.BlockSpec((1, w), lambda b_, i: (0, 0))
    blk = pl.BlockSpec(gaw.shape, lambda b_, i: (0, 0, 0))
    return pl.pallas_call(
        kern,
        grid=(b, s // ts),
        in_specs=[pl.BlockSpec((1, ts, w), lambda b_, i: (b_, i, PROJ_LRU_COL)),
                  pl.BlockSpec((1, ts, w), lambda b_, i: (b_, i, PROJ_LRU_COL + 1)),
                  pl.BlockSpec((CONV_W, w), lambda b_, i: (0, 0)),
                  vec, blk, vec, blk, vec, vec],
        out_specs=pl.BlockSpec((1, ts, w), lambda b_, i: (b_, i, 0)),
        out_shape=jax.ShapeDtypeStruct((b, s, w), BF16),
        scratch_shapes=[pltpu.VMEM((SUBLANES, w), F32), pltpu.VMEM((SUBLANES, w), F32),
                        pltpu.VMEM((ts, w), F32), pltpu.VMEM((ts, w), F32),
                        pltpu.VMEM((ts, w), F32)],
        compiler_params=_params("parallel", "arbitrary"),
        name="rg_lru",
    )(rest, rest, conv_w, conv_b, gaw, gab, gxw, gxb, lru_lambda)


def _layer_norm(z, g, b):
    mu = jnp.mean(z, axis=-1, keepdims=True)
    zc = z - mu
    var = jnp.mean(zc * zc, axis=-1, keepdims=True)
    return zc * lax.rsqrt(var + LN_EPS) * g + b


def _merge_kernel(ya_ref, yr_ref, ga_ref, gl_ref, x_ref, wa_ref, wl_ref, wo_ref, g_ref, b_ref,
                  o_ref, ob_ref, *, alpha):
    pa = jnp.dot(ya_ref[...], wa_ref[...], preferred_element_type=F32)
    pr = jnp.dot(yr_ref[...], wl_ref[...], preferred_element_type=F32)
    merged = (jax.nn.sigmoid(ga_ref[...].astype(F32)) * pa
              + jax.nn.sigmoid(gl_ref[...].astype(F32)) * pr)
    mix = jnp.dot(merged.astype(BF16), wo_ref[...], preferred_element_type=F32)
    y = _layer_norm(alpha * x_ref[...] + mix, g_ref[...], b_ref[...])
    o_ref[...] = y
    ob_ref[...] = y.T.astype(BF16)


def _merge_out(ya, yr, rest, x, wa, wl, wo, ln_g, ln_b, alpha, tm):
    t, d = x.shape
    kern = functools.partial(_merge_kernel, alpha=alpha)
    rows = lambda c: pl.BlockSpec((tm, d), lambda i: (i, c))
    full = pl.BlockSpec((d, d), lambda i: (0, 0))
    vec = pl.BlockSpec((1, d), lambda i: (0, 0))
    return pl.pallas_call(
        kern,
        grid=(t // tm,),
        in_specs=[rows(0), rows(0), rows(PROJ_GATE_COL), rows(PROJ_GATE_COL + 1), rows(0),
                  full, full, full, vec, vec],
        out_specs=[rows(0), pl.BlockSpec((d, tm), lambda i: (0, i))],
        out_shape=[jax.ShapeDtypeStruct((t, d), F32), jax.ShapeDtypeStruct((d, t), BF16)],
        compiler_params=_params("parallel"),
        name="merge_out_ln",
    )(ya, yr, rest, rest, x, wa, wl, wo, ln_g, ln_b)


def _cmpx(v, i, j):
    hi = jnp.maximum(v[i], v[j])
    v[j] = jnp.minimum(v[i], v[j])
    v[i] = hi


def _bitonic_merge_desc(v):
    n = len(v)
    j = n // 2
    while j >= 1:
        for i in range(n):
            if i & j == 0:
                _cmpx(v, i, i + j)
        j //= 2


def _bitonic_sort_desc(v):
    n = len(v)
    k = 2
    while k <= n:
        j = k // 2
        while j >= 1:
            for i in range(n):
                l = i ^ j
                if l > i:
                    if i & k == 0:
                        _cmpx(v, i, l)
                    else:
                        _cmpx(v, l, i)
            j //= 2
        k *= 2


def _top_merge(a, b):
    n = len(a)
    v = [jnp.maximum(a[i], b[n - 1 - i]) for i in range(n)]
    _bitonic_merge_desc(v)
    return v


_CAND_PAIRS = [(p, q) for p in range(PEER_TOPK) for q in range(PEER_TOPK)
               if (p + 1) * (q + 1) <= PEER_TOPK]


def _select_kernel(x_ref, wq_ref, sk_ref, rank_ref, bexp_ref, cnt_ref, arow_ref,
                   q_scr, s_scr, pk_scr, t_scr, *, tt):
    qt = jnp.dot(wq_ref[...], x_ref[...], preferred_element_type=F32)
    q_scr[...] = qt.astype(BF16)
    dk = sk_ref.shape[3]
    n_grp = N_KEYS // SUBLANES

    def head(h, _):
        for c in range(2):
            r0 = pl.multiple_of((h * 2 + c) * dk, dk)
            sc = jnp.dot(sk_ref[h, c], q_scr[pl.ds(r0, dk), :], preferred_element_type=F32)
            s_scr[c, h] = sc
            v = [sc[g * SUBLANES:(g + 1) * SUBLANES, :] for g in range(n_grp)]
            _bitonic_sort_desc(v)
            for sh in (4, 2, 1):
                v = _top_merge(v, [pltpu.roll(e, sh, 0) for e in v])
            for p in range(PEER_TOPK):
                pk_scr[c, p, pl.ds(h, 1), :] = v[p][0:1, :]
        return 0

    lax.fori_loop(0, PEER_HEADS, head, 0)

    a = [pk_scr[0, p] for p in range(PEER_TOPK)]
    b = [pk_scr[1, p] for p in range(PEER_TOPK)]
    cands = [a[p] + b[q] for p, q in _CAND_PAIRS]
    neg = jnp.full_like(cands[0], -jnp.inf)
    blocks = cands + [neg] * ((-len(cands)) % PEER_TOPK)
    top = None
    for i0 in range(0, len(blocks), PEER_TOPK):
        blk = blocks[i0:i0 + PEER_TOPK]
        _bitonic_sort_desc(blk)
        top = blk if top is None else _top_merge(top, blk)
    tau = top[PEER_TOPK - 1]
    mx = a[0] + b[0]
    z = jnp.zeros_like(tau)
    for cnd in cands:
        z = z + jnp.where(cnd >= tau, jnp.exp(cnd - mx), 0.0)
    t_scr[0] = tau
    t_scr[1] = a[0] + jnp.log(z)

    def head2(h, _):
        s1 = s_scr[0, h]
        s2 = s_scr[1, h]
        tau_h = t_scr[0, pl.ds(h, 1), :]
        rank = jnp.zeros_like(s2)
        cnt = jnp.zeros_like(s1)
        for q in range(PEER_TOPK):
            bq = pk_scr[1, q, pl.ds(h, 1), :]
            rank = rank + jnp.where(bq > s2, 1.0, 0.0)
            cnt = cnt + jnp.where(s1 + bq >= tau_h, 1.0, 0.0)
        rank_ref[h] = rank.astype(BF16)
        cnt_ref[h] = cnt
        arow_ref[h] = jnp.exp(s1 - t_scr[1, pl.ds(h, 1), :])
        bexp_ref[h] = jnp.exp(s2 - pk_scr[1, 0, pl.ds(h, 1), :]).astype(BF16)
        return 0

    lax.fori_loop(0, PEER_HEADS, head2, 0)


def _peer_select(xt, wq_t, sk, tt):
    d, t = xt.shape
    nq = wq_t.shape[0]
    kern = functools.partial(_select_kernel, tt=tt)
    big = pl.BlockSpec((PEER_HEADS, N_KEYS, tt), lambda i: (0, 0, i))
    shape = lambda dt: jax.ShapeDtypeStruct((PEER_HEADS, N_KEYS, t), dt)
    return pl.pallas_call(
        kern,
        grid=(t // tt,),
        in_specs=[pl.BlockSpec((d, tt), lambda i: (0, i)),
                  pl.BlockSpec((nq, d), lambda i: (0, 0)),
                  pl.BlockSpec(sk.shape, lambda i: (0, 0, 0, 0))],
        out_specs=[big, big, big, big],
        out_shape=[shape(BF16), shape(BF16), shape(F32), shape(F32)],
        scratch_shapes=[pltpu.VMEM((nq, tt), BF16),
                        pltpu.VMEM((2, PEER_HEADS, N_KEYS, tt), F32),
                        pltpu.VMEM((2, PEER_TOPK, PEER_HEADS, tt), F32),
                        pltpu.VMEM((2, PEER_HEADS, tt), F32)],
        compiler_params=_params("parallel"),
        name="peer_select",
    )(xt, wq_t, sk)


def _peer_kernel(xt_ref, x_ref, u_ref, vt_ref, vtp_ref, rank_ref, bexp_ref, cnt_ref, arow_ref,
                 g_ref, b_ref, o_ref, acc_scr, act0, act1, w0, w1, row_scr, *, tt, nblk, eb, alpha):
    e = pl.program_id(1)
    acts = (act0, act1)
    ws = (w0, w1)
    rows_per_blk = eb // N_KEYS
    half = acc_scr.shape[0] // 2
    last_pair = nblk // 2 - 1

    @pl.when(e == 0)
    def _():
        acc_scr[...] = jnp.zeros_like(acc_scr)
        ws[last_pair % 2][...] = jnp.zeros_like(ws[last_pair % 2])

    def mm1(k):
        acts[k % 2][...] = jnp.dot(u_ref[k * eb:(k + 1) * eb, :], xt_ref[...],
                                   preferred_element_type=F32)

    def mm2(p, part, v_ref=None):
        rows = slice(part * half, (part + 1) * half)
        v = vt_ref[p, rows, :] if v_ref is None else v_ref[0, rows, :]
        acc_scr[rows, :] += jnp.dot(v, ws[p % 2][...], preferred_element_type=F32)

    def gates(k):
        act, w = acts[k % 2], ws[(k // 2) % 2]
        for ii in range(rows_per_blk):
            i = k * rows_per_blk + ii
            for h in range(PEER_HEADS):
                for a, ref in enumerate((cnt_ref, arow_ref)):
                    row_scr[a, h] = jnp.broadcast_to(ref[h, i:i + 1, :], (SUBLANES, tt)).astype(BF16)
            for j0 in range(0, N_KEYS, SUBLANES):
                gsum = None
                for h in range(PEER_HEADS):
                    sel = rank_ref[h, j0:j0 + SUBLANES, :] < row_scr[0, h]
                    term = jnp.where(sel, row_scr[1, h] * bexp_ref[h, j0:j0 + SUBLANES, :],
                                     jnp.zeros((), BF16))
                    gsum = term if gsum is None else gsum + term
                r = ii * N_KEYS + j0
                rw = (k % 2) * eb + r
                w[rw:rw + SUBLANES, :] = _gelu_tanh(act[r:r + SUBLANES, :]).astype(BF16) * gsum

    mm1(0)
    for k in range(nblk):
        gates(k)
        if k + 1 < nblk:
            mm1(k + 1)
        if k >= 2:
            mm2(k // 2 - 1, k % 2)
        else:
            mm2(last_pair, k, vtp_ref)

    @pl.when(e == pl.num_programs(1) - 1)
    def _():
        mm2(last_pair, 0)
        mm2(last_pair, 1)
        ffn = acc_scr[...].T
        o_ref[...] = _layer_norm(alpha * x_ref[...] + ffn, g_ref[...], b_ref[...])


def _peer_mix(xt, x, u, vt, rank, bexp, cnt, arow, ln_g, ln_b, alpha, tt, te, eb):
    t, d = x.shape
    n_exp = u.shape[0]
    nblk = te // eb
    kern = functools.partial(_peer_kernel, tt=tt, nblk=nblk, eb=eb, alpha=alpha)
    big = pl.BlockSpec((PEER_HEADS, N_KEYS, tt), lambda i, e: (0, 0, i))
    rows = pl.BlockSpec((PEER_HEADS, te // N_KEYS, tt), lambda i, e: (0, e, i))
    vec = pl.BlockSpec((1, d), lambda i, e: (0, 0))
    return pl.pallas_call(
        kern,
        grid=(t // tt, n_exp // te),
        in_specs=[pl.BlockSpec((d, tt), lambda i, e: (0, i)),
                  pl.BlockSpec((tt, d), lambda i, e: (i, 0)),
                  pl.BlockSpec((te, d), lambda i, e: (e, 0)),
                  pl.BlockSpec((nblk // 2, d, 2 * eb), lambda i, e: (e, 0, 0)),
                  pl.BlockSpec((1, d, 2 * eb),
                               lambda i, e: (jnp.maximum(e * (nblk // 2) - 1, 0), 0, 0)),
                  big, big, rows, rows, vec, vec],
        out_specs=pl.BlockSpec((tt, d), lambda i, e: (i, 0)),
        out_shape=jax.ShapeDtypeStruct((t, d), F32),
        scratch_shapes=[pltpu.VMEM((d, tt), F32),
                        pltpu.VMEM((eb, tt), F32), pltpu.VMEM((eb, tt), F32),
                        pltpu.VMEM((2 * eb, tt), BF16), pltpu.VMEM((2 * eb, tt), BF16),
                        pltpu.VMEM((2, PEER_HEADS, SUBLANES, tt), BF16)],
        compiler_params=_params("parallel", "arbitrary"),
        name="peer_mix_ln",
    )(xt, x, u, vt, vt, rank, bexp, cnt, arow, ln_g, ln_b)


def _tiles(b, s, t):
    return dict(mm_tm=min(2048, t), mm_tn=512, attn_tb=min(256, s), attn_hp=4,
                lru_ts=min(256, s), merge_tm=min(256, t), sel_tt=min(256, t),
                mix_tt=min(512, t), mix_te=2048, mix_eb=256)


def kernel(x, w_in, lambda_qk, subln_g, conv_w, conv_b, gate_a_w, gate_a_b, gate_x_w, gate_x_b,
           lru_lambda, w_br_attn, w_br_lru, w_out, ln1_g, ln1_b, peer_wq, peer_subkeys, peer_u,
           peer_v, ln2_g, ln2_b):
    b, s, d = x.shape
    t = b * s
    depth = w_in.shape[0]
    alpha = (2.0 * depth) ** 0.25
    attn_w = ATTN_HEADS * HEAD_W
    ts = _tiles(b, s, t)
    slopes = jnp.exp2(-8.0 * jnp.arange(1, ATTN_HEADS + 1, dtype=F32) / ATTN_HEADS)
    row = lambda a: a.reshape(1, -1)
    eb = ts["mix_eb"]

    xf = x.reshape(t, d)
    for l in range(depth):
        lam_init = 0.8 - 0.6 * math.exp(-0.3 * l)
        w_l = w_in[l].astype(BF16)
        w_cat = jnp.concatenate([w_l[:, :2 * attn_w], w_l[:, 3 * attn_w:]], axis=1)
        proj = _matmul(xf, w_cat, BF16, ts["mm_tm"], ts["mm_tn"])
        vt = _value_proj_t(xf, w_l[:, 2 * attn_w:3 * attn_w].T, ts["attn_tb"])
        proj3 = proj.reshape(b, s, -1)
        ya = _diff_attention(proj3, vt, slopes, lambda_qk[l], row(subln_g[l]),
                             lam_init, ts["attn_tb"], ts["attn_hp"])
        yr = _rg_lru(proj3, conv_w[l], row(conv_b[l]), gate_a_w[l].astype(BF16),
                     row(gate_a_b[l]), gate_x_w[l].astype(BF16), row(gate_x_b[l]),
                     row(lru_lambda[l]), ts["lru_ts"])
        x1, x1t = _merge_out(ya.reshape(t, attn_w), yr.reshape(t, d), proj, xf,
                             w_br_attn[l].astype(BF16), w_br_lru[l].astype(BF16),
                             w_out[l].astype(BF16), row(ln1_g[l]), row(ln1_b[l]), alpha,
                             ts["merge_tm"])
        rank, bexp, cnt, arow = _peer_select(x1t, peer_wq[l].T.astype(BF16),
                                             peer_subkeys[l].astype(BF16), ts["sel_tt"])
        v_blocks = peer_v[l].astype(BF16).reshape(-1, 2 * eb, d).transpose(0, 2, 1)
        xf = _peer_mix(x1t, x1, peer_u[l].astype(BF16), v_blocks, rank, bexp, cnt, arow,
                       row(ln2_g[l]), row(ln2_b[l]), alpha, ts["mix_tt"], ts["mix_te"], eb)
    return xf.reshape(b, s, d)
```

```python
import functools
import math

import jax
import jax.numpy as jnp
from jax import lax
from jax.experimental import pallas as pl
from jax.experimental.pallas import tpu as pltpu

F32 = jnp.float32
BF16 = jnp.bfloat16

ATTN_HEADS = 8
HEAD_DIM = 64
HEAD_W = 2 * HEAD_DIM
LRU_BLOCKS = 8
LRU_C = 8.0
CONV_W = 4
PEER_HEADS = 8
N_KEYS = 128
PEER_TOPK = 16
LN_EPS = 1e-5
RMS_EPS = 1e-6
PROJ_LRU_COL = 2
PROJ_GATE_COL = 4
SUBLANES = 8
PACKED_ROWS = 16
VMEM_LIMIT = 56 * 1024 * 1024

_NT = (((1,), (1,)), ((), ()))


def _params(*sem, flags=None):
    return pltpu.CompilerParams(dimension_semantics=sem, vmem_limit_bytes=VMEM_LIMIT, flags=flags)


MM_ROWS = 256


def _mm_kernel(x_ref, w_ref, o_ref):
    for r in range(0, x_ref.shape[0], MM_ROWS):
        o_ref[r:r + MM_ROWS, :] = jnp.dot(x_ref[r:r + MM_ROWS, :].astype(BF16), w_ref[...],
                                          preferred_element_type=F32).astype(o_ref.dtype)


def _matmul(x, w, out_dtype, tm, tn):
    t, k = x.shape
    n = w.shape[1]
    return pl.pallas_call(
        _mm_kernel,
        grid=(t // tm, n // tn),
        in_specs=[pl.BlockSpec((tm, k), lambda i, j: (i, 0)),
                  pl.BlockSpec((k, tn), lambda i, j: (0, j))],
        out_specs=pl.BlockSpec((tm, tn), lambda i, j: (i, j)),
        out_shape=jax.ShapeDtypeStruct((t, n), out_dtype),
        compiler_params=_params("parallel", "arbitrary"),
        name="in_proj",
    )(x, w)


def _vt_kernel(x_ref, w_ref, o_ref):
    vt = lax.dot_general(w_ref[...], x_ref[...].astype(BF16), _NT,
                         preferred_element_type=F32)
    o_ref[...] = vt.astype(o_ref.dtype).reshape(o_ref.shape)


def _value_proj_t(x, w_t, tk):
    t, d = x.shape
    return pl.pallas_call(
        _vt_kernel,
        grid=(t // tk,),
        in_specs=[pl.BlockSpec((tk, d), lambda i: (i, 0)),
                  pl.BlockSpec(w_t.shape, lambda i: (0, 0))],
        out_specs=pl.BlockSpec((ATTN_HEADS, 1, HEAD_W, tk), lambda i: (0, i, 0, 0)),
        out_shape=jax.ShapeDtypeStruct((ATTN_HEADS, t // tk, HEAD_W, tk), BF16),
        compiler_params=_params("parallel"),
        name="value_proj_t",
    )(x, w_t)


def _attn_kernel(slopes_ref, lqk_ref, g_ref, q_ref, k_ref, vt_ref, o_ref, *scr, tb, hp, lam_init):
    acc_scr, st_scr = scr[:hp], scr[hp:]
    hg = pl.program_id(1)
    qi = pl.program_id(2)
    lq = lqk_ref[...]
    lam = (jnp.exp(jnp.sum(lq[0:1] * lq[1:2], axis=1, keepdims=True))
           - jnp.exp(jnp.sum(lq[2:3] * lq[3:4], axis=1, keepdims=True)) + lam_init)

    lane = lax.broadcasted_iota(jnp.int32, (tb, HEAD_W), 1)
    rel0 = (lax.broadcasted_iota(jnp.int32, (tb, tb), 1)
            - lax.broadcasted_iota(jnp.int32, (tb, tb), 0))
    rel0f = rel0.astype(F32)
    slopes, qqs, biases = [], [], []
    for hh in range(hp):
        slope = slopes_ref[hg * hp + hh]
        qs = q_ref[0, :, hh * HEAD_W:(hh + 1) * HEAD_W] * (HEAD_DIM ** -0.5)
        zero = jnp.zeros_like(qs)
        qqs.append(jnp.concatenate([jnp.where(lane < HEAD_DIM, qs, zero),
                                    jnp.where(lane >= HEAD_DIM, qs, zero)], axis=0))
        slopes.append(slope)
        biases.append(slope * rel0f)
        acc_scr[hh][...] = jnp.zeros_like(acc_scr[hh])

    def scores(j, hh):
        start = pl.multiple_of(j * tb, tb)
        k = k_ref[0, pl.ds(start, tb), hh * HEAD_W:(hh + 1) * HEAD_W]
        st_scr[hh][...] = lax.dot_general(k, qqs[hh], _NT, preferred_element_type=F32)

    def step(j, carry, last):
        shift = (qi - j) * tb
        out = []
        for hh in range(hp):
            m, l = carry[hh]
            off = slopes[hh] * shift.astype(F32)
            bt = jnp.where(rel0 >= 0, biases[hh], jnp.inf) if last else biases[hh]
            st = st_scr[hh][...] - jnp.concatenate([bt, bt], axis=1)
            m_new = jnp.maximum(m, jnp.max(st, axis=0, keepdims=True) - off)
            pt = jnp.exp(st - (m_new + off))
            a = jnp.exp(m - m_new)
            l = a * l + jnp.sum(pt, axis=0, keepdims=True)
            pt = pt.astype(BF16)
            if not last:
                scores(j + 1, hh)
            acc_scr[hh][...] = a * acc_scr[hh][...] + jnp.dot(vt_ref[hh, j], pt,
                                                              preferred_element_type=F32)
            out.append((m_new, l))
        return tuple(out)

    for hh in range(hp):
        scores(0, hh)
    carry = tuple((jnp.full((1, 2 * tb), -jnp.inf, F32), jnp.zeros((1, 2 * tb), F32))
                  for _ in range(hp))
    carry = lax.fori_loop(0, qi, functools.partial(step, last=False), carry)
    carry = step(qi, carry, last=True)

    for hh in range(hp):
        ot = acc_scr[hh][...] / carry[hh][1]
        o = (ot[:, :tb] - lam * ot[:, tb:]).T
        o = o * lax.rsqrt(jnp.mean(o * o, axis=-1, keepdims=True) + RMS_EPS)
        o = o * g_ref[...] * (1.0 - lam_init)
        o_ref[0, :, hh * HEAD_W:(hh + 1) * HEAD_W] = o.astype(o_ref.dtype)


def _diff_attention(qk, vt, slopes, lambda_qk, subln_g, lam_init, tb, hp):
    b, s, _ = qk.shape
    kern = functools.partial(_attn_kernel, tb=tb, hp=hp, lam_init=lam_init)
    gw = hp * HEAD_W
    return pl.pallas_call(
        kern,
        grid=(b, ATTN_HEADS // hp, s // tb),
        in_specs=[pl.BlockSpec(memory_space=pltpu.SMEM),
                  pl.BlockSpec((4, HEAD_DIM), lambda b_, h, i: (0, 0)),
                  pl.BlockSpec((1, HEAD_W), lambda b_, h, i: (0, 0)),
                  pl.BlockSpec((1, tb, gw), lambda b_, h, i: (b_, i, h)),
                  pl.BlockSpec((1, s, gw), lambda b_, h, i: (b_, 0, ATTN_HEADS // hp + h)),
                  pl.BlockSpec((hp, s // tb, HEAD_W, tb), lambda b_, h, i: (h, b_, 0, 0))],
        out_specs=pl.BlockSpec((1, tb, gw), lambda b_, h, i: (b_, i, h)),
        out_shape=jax.ShapeDtypeStruct((b, s, ATTN_HEADS * HEAD_W), BF16),
        scratch_shapes=([pltpu.VMEM((HEAD_W, 2 * tb), F32)] * hp
                        + [pltpu.VMEM((tb, 2 * tb), F32)] * hp),
        compiler_params=_params("parallel", "parallel", "arbitrary"),
        name="diff_attn",
    )(slopes, lambda_qk, subln_g, qk, qk, vt)


def _gelu(x):
    return jax.nn.gelu(x)


def _gelu_tanh(x):
    c = math.sqrt(2.0 / math.pi)
    hx = 0.5 * x
    return hx + hx * jnp.tanh(x * (c + (c * 0.044715) * (x * x)))


def _lru_kernel(xr_ref, gr_ref, cw_ref, cb_ref, gaw_ref, gab_ref, gxw_ref, gxb_ref, lam_ref,
                o_ref, tail_scr, h_scr, a_scr, u_scr, gg_scr, *, ts):
    si = pl.program_id(1)

    @pl.when(si == 0)
    def _():
        tail_scr[...] = jnp.zeros_like(tail_scr)
        h_scr[...] = jnp.zeros_like(h_scr)

    x = xr_ref[0].astype(F32)
    w = x.shape[1]
    tail = tail_scr[...]
    row8 = lax.broadcasted_iota(jnp.int32, (SUBLANES, w), 0)
    cw = cw_ref[...]
    xb = cb_ref[...]
    for d in range(CONV_W - 1, 0, -1):
        xs = pltpu.roll(x, d, 0)
        top = jnp.where(row8 < d, pltpu.roll(tail, d, 0), xs[:SUBLANES])
        xs = jnp.concatenate([top, xs[SUBLANES:]], axis=0)
        xb = xb + xs * cw[CONV_W - 1 - d:CONV_W - d]
    xb = xb + x * cw[CONV_W - 1:CONV_W]
    tail_scr[...] = x[ts - SUBLANES:]

    xbb = xb.astype(BF16)
    bw = w // LRU_BLOCKS

    def gate(w_ref, b_ref):
        parts = [jnp.dot(xbb[:, g * bw:(g + 1) * bw], w_ref[g], preferred_element_type=F32)
                 for g in range(LRU_BLOCKS)]
        return jax.nn.sigmoid(jnp.concatenate(parts, axis=1) + b_ref[...])

    r = gate(gaw_ref, gab_ref)
    i = gate(gxw_ref, gxb_ref)
    lam = lam_ref[...]
    softplus_neg = jnp.maximum(-lam, 0.0) + jnp.log1p(jnp.exp(-jnp.abs(lam)))
    log_a = -LRU_C * r * softplus_neg
    a = jnp.exp(log_a)
    mult = jnp.sqrt(1.0 - jnp.exp(2.0 * log_a))
    rows = lax.broadcasted_iota(jnp.int32, (ts, w), 0)
    mult = jnp.where(jnp.logical_and(rows == 0, si == 0), 1.0, mult)
    a_scr[...] = a
    u_scr[...] = mult * (i * xb)
    gg_scr[...] = _gelu(gr_ref[0].astype(F32))

    def group(g, hprev):
        r0 = pl.multiple_of(g * SUBLANES, SUBLANES)
        ag = a_scr[pl.ds(r0, SUBLANES), :]
        ug = u_scr[pl.ds(r0, SUBLANES), :]
        for d in (1, 2, 4):
            keep = row8 >= d
            ug = jnp.where(keep, ag * pltpu.roll(ug, d, 0) + ug, ug)
            ag = jnp.where(keep, ag * pltpu.roll(ag, d, 0), ag)
        hg = ag * hprev + ug
        o_ref[0, pl.ds(r0, SUBLANES), :] = (hg * gg_scr[pl.ds(r0, SUBLANES), :]).astype(o_ref.dtype)
        return jnp.broadcast_to(hg[SUBLANES - 1:SUBLANES, :], (SUBLANES, w))

    h_scr[...] = lax.fori_loop(0, ts // SUBLANES, group, h_scr[...])


def _rg_lru(rest, conv_w, conv_b, gaw, gab, gxw, gxb, lru_lambda, ts):
    b, s, _ = rest.shape
    w = conv_w.shape[1]
    kern = functools.partial(_lru_kernel, ts=ts)
    vec = pl.BlockSpec((1, w), lambda b_, i: (0, 0))
    blk = pl.BlockSpec(gaw.shape, lambda b_, i: (0, 0, 0))
    return pl.pallas_call(
        kern,
        grid=(b, s // ts),
        in_specs=[pl.BlockSpec((1, ts, w), lambda b_, i: (b_, i, PROJ_LRU_COL)),
                  pl.BlockSpec((1, ts, w), lambda b_, i: (b_, i, PROJ_LRU_COL + 1)),
                  pl.BlockSpec((CONV_W, w), lambda b_, i: (0, 0)),
                  vec, blk, vec, blk, vec, vec],
        out_specs=pl.BlockSpec((1, ts, w), lambda b_, i: (b_, i, 0)),
        out_shape=jax.ShapeDtypeStruct((b, s, w), BF16),
        scratch_shapes=[pltpu.VMEM((SUBLANES, w), F32), pltpu.VMEM((SUBLANES, w), F32),
                        pltpu.VMEM((ts, w), F32), pltpu.VMEM((ts, w), F32),
                        pltpu.VMEM((ts, w), F32)],
        compiler_params=_params("parallel", "arbitrary"),
        name="rg_lru",
    )(rest, rest, conv_w, conv_b, gaw, gab, gxw, gxb, lru_lambda)


def _layer_norm(z, g, b):
    mu = jnp.mean(z, axis=-1, keepdims=True)
    zc = z - mu
    var = jnp.mean(zc * zc, axis=-1, keepdims=True)
    return zc * lax.rsqrt(var + LN_EPS) * g + b


def _merge_kernel(ya_ref, yr_ref, ga_ref, gl_ref, x_ref, wa_ref, wl_ref, wo_ref, g_ref, b_ref,
                  o_ref, ob_ref, *, alpha):
    pa = jnp.dot(ya_ref[...], wa_ref[...], preferred_element_type=F32)
    pr = jnp.dot(yr_ref[...], wl_ref[...], preferred_element_type=F32)
    merged = (jax.nn.sigmoid(ga_ref[...].astype(F32)) * pa
              + jax.nn.sigmoid(gl_ref[...].astype(F32)) * pr)
    mix = jnp.dot(merged.astype(BF16), wo_ref[...], preferred_element_type=F32)
    y = _layer_norm(alpha * x_ref[...] + mix, g_ref[...], b_ref[...])
    o_ref[...] = y
    ob_ref[...] = y.T.astype(BF16)


def _merge_out(ya, yr, rest, x, wa, wl, wo, ln_g, ln_b, alpha, tm):
    t, d = x.shape
    kern = functools.partial(_merge_kernel, alpha=alpha)
    rows = lambda c: pl.BlockSpec((tm, d), lambda i: (i, c))
    full = pl.BlockSpec((d, d), lambda i: (0, 0))
    vec = pl.BlockSpec((1, d), lambda i: (0, 0))
    return pl.pallas_call(
        kern,
        grid=(t // tm,),
        in_specs=[rows(0), rows(0), rows(PROJ_GATE_COL), rows(PROJ_GATE_COL + 1), rows(0),
                  full, full, full, vec, vec],
        out_specs=[rows(0), pl.BlockSpec((d, tm), lambda i: (0, i))],
        out_shape=[jax.ShapeDtypeStruct((t, d), F32), jax.ShapeDtypeStruct((d, t), BF16)],
        compiler_params=_params("parallel"),
        name="merge_out_ln",
    )(ya, yr, rest, rest, x, wa, wl, wo, ln_g, ln_b)


def _cmpx(v, i, j):
    hi = jnp.maximum(v[i], v[j])
    v[j] = jnp.minimum(v[i], v[j])
    v[i] = hi


def _bitonic_merge_desc(v):
    n = len(v)
    j = n // 2
    while j >= 1:
        for i in range(n):
            if i & j == 0:
                _cmpx(v, i, i + j)
        j //= 2


def _bitonic_sort_desc(v):
    n = len(v)
    k = 2
    while k <= n:
        j = k // 2
        while j >= 1:
            for i in range(n):
                l = i ^ j
                if l > i:
                    if i & k == 0:
                        _cmpx(v, i, l)
                    else:
                        _cmpx(v, l, i)
            j //= 2
        k *= 2


def _top_merge(a, b):
    n = len(a)
    v = [jnp.maximum(a[i], b[n - 1 - i]) for i in range(n)]
    _bitonic_merge_desc(v)
    return v


def _prefix_count(b, test):
    assert len(b) == 16
    c8 = test(b[7])
    n = jnp.where(c8, 8.0, 0.0)
    c4 = test(jnp.where(c8, b[11], b[3]))
    n = n + jnp.where(c4, 4.0, 0.0)
    c2 = test(jnp.where(c8, jnp.where(c4, b[13], b[9]), jnp.where(c4, b[5], b[1])))
    n = n + jnp.where(c2, 2.0, 0.0)
    lo = jnp.where(c4, jnp.where(c2, b[6], b[4]), jnp.where(c2, b[2], b[0]))
    hi = jnp.where(c4, jnp.where(c2, b[14], b[12]), jnp.where(c2, b[10], b[8]))
    n = n + jnp.where(test(jnp.where(c8, hi, lo)), 1.0, 0.0)
    return jnp.where(test(b[15]), 16.0, n)


_CAND_PAIRS = [(p, q) for p in range(PEER_TOPK) for q in range(PEER_TOPK)
               if (p + 1) * (q + 1) <= PEER_TOPK]


def _select_kernel(x_ref, wq_ref, sk_ref, rank_ref, bexp_ref, cnt_ref, arow_ref,
                   q_scr, s_scr, pk_scr, t_scr, *, tt):
    qt = jnp.dot(wq_ref[...], x_ref[...], preferred_element_type=F32)
    q_scr[...] = qt.astype(BF16)
    dk = sk_ref.shape[3]
    n_grp = N_KEYS // SUBLANES

    def head(h, _):
        for c in range(2):
            r0 = pl.multiple_of((h * 2 + c) * dk, dk)
            sc = jnp.dot(sk_ref[h, c], q_scr[pl.ds(r0, dk), :], preferred_element_type=F32)
            s_scr[c, h] = sc
            v = [sc[g * SUBLANES:(g + 1) * SUBLANES, :] for g in range(n_grp)]
            _bitonic_sort_desc(v)
            for sh in (4, 2, 1):
                v = _top_merge(v, [pltpu.roll(e, sh, 0) for e in v])
            for p in range(PEER_TOPK):
                pk_scr[c, p, pl.ds(h, 1), :] = v[p][0:1, :]
        return 0

    lax.fori_loop(0, PEER_HEADS, head, 0)

    a = [pk_scr[0, p] for p in range(PEER_TOPK)]
    b = [pk_scr[1, p] for p in range(PEER_TOPK)]
    cands = [a[p] + b[q] for p, q in _CAND_PAIRS]
    neg = jnp.full_like(cands[0], -jnp.inf)
    blocks = cands + [neg] * ((-len(cands)) % PEER_TOPK)
    top = None
    for i0 in range(0, len(blocks), PEER_TOPK):
        blk = blocks[i0:i0 + PEER_TOPK]
        _bitonic_sort_desc(blk)
        top = blk if top is None else _top_merge(top, blk)
    tau = top[PEER_TOPK - 1]
    mx = a[0] + b[0]
    z = jnp.zeros_like(tau)
    for cnd in cands:
        z = z + jnp.where(cnd >= tau, jnp.exp(cnd - mx), 0.0)
    t_scr[0] = tau
    t_scr[1] = a[0] + jnp.log(z)

    def head2(h, _):
        s1 = s_scr[0, h]
        s2 = s_scr[1, h]
        tau_h = t_scr[0, pl.ds(h, 1), :]
        bs = [pk_scr[1, q, pl.ds(h, 1), :] for q in range(PEER_TOPK)]
        rank_ref[h] = _prefix_count(bs, lambda thr: thr > s2).astype(BF16)
        cnt_ref[h] = _prefix_count(bs, lambda thr: s1 + thr >= tau_h)
        arow_ref[h] = jnp.exp(s1 - t_scr[1, pl.ds(h, 1), :])
        bexp_ref[h] = jnp.exp(s2 - pk_scr[1, 0, pl.ds(h, 1), :]).astype(BF16)
        return 0

    lax.fori_loop(0, PEER_HEADS, head2, 0)


def _peer_select(xt, wq_t, sk, tt):
    d, t = xt.shape
    nq = wq_t.shape[0]
    kern = functools.partial(_select_kernel, tt=tt)
    big = pl.BlockSpec((PEER_HEADS, N_KEYS, tt), lambda i: (0, 0, i))
    shape = lambda dt: jax.ShapeDtypeStruct((PEER_HEADS, N_KEYS, t), dt)
    return pl.pallas_call(
        kern,
        grid=(t // tt,),
        in_specs=[pl.BlockSpec((d, tt), lambda i: (0, i)),
                  pl.BlockSpec((nq, d), lambda i: (0, 0)),
                  pl.BlockSpec(sk.shape, lambda i: (0, 0, 0, 0))],
        out_specs=[big, big, big, big],
        out_shape=[shape(BF16), shape(BF16), shape(F32), shape(F32)],
        scratch_shapes=[pltpu.VMEM((nq, tt), BF16),
                        pltpu.VMEM((2, PEER_HEADS, N_KEYS, tt), F32),
                        pltpu.VMEM((2, PEER_TOPK, PEER_HEADS, tt), F32),
                        pltpu.VMEM((2, PEER_HEADS, tt), F32)],
        compiler_params=_params("parallel"),
        name="peer_select",
    )(xt, wq_t, sk)


def _peer_kernel(xt_ref, x_ref, u_ref, vt_ref, rank_ref, bexp_ref, cnt_ref, arow_ref,
                 g_ref, b_ref, o_ref, acc_scr, act0, act1, w0, w1, row_scr, *, tt, nblk, eb, alpha):
    e = pl.program_id(1)

    @pl.when(e == 0)
    def _():
        acc_scr[...] = jnp.zeros_like(acc_scr)

    acts = (act0, act1)
    ws = (w0, w1)
    rows_per_blk = eb // N_KEYS
    half = acc_scr.shape[0] // 2

    def mm1(k):
        acts[k % 2][...] = jnp.dot(u_ref[k * eb:(k + 1) * eb, :], xt_ref[...],
                                   preferred_element_type=F32)

    def mm2(p, part):
        rows = slice(part * half, (part + 1) * half)
        acc_scr[rows, :] += jnp.dot(vt_ref[p, rows, :], ws[p % 2][...], preferred_element_type=F32)

    def gates(k):
        act, w = acts[k % 2], ws[(k // 2) % 2]
        for ii in range(rows_per_blk):
            i = k * rows_per_blk + ii
            for h in range(PEER_HEADS):
                for a, ref in enumerate((cnt_ref, arow_ref)):
                    row_scr[a, h] = jnp.broadcast_to(ref[h, i:i + 1, :], (PACKED_ROWS, tt)).astype(BF16)
            for j0 in range(0, N_KEYS, PACKED_ROWS):
                gsum = None
                for h in range(PEER_HEADS):
                    sel = rank_ref[h, j0:j0 + PACKED_ROWS, :] < row_scr[0, h]
                    term = jnp.where(sel, row_scr[1, h] * bexp_ref[h, j0:j0 + PACKED_ROWS, :],
                                     jnp.zeros((), BF16))
                    gsum = term if gsum is None else gsum + term
                r = ii * N_KEYS + j0
                rw = (k % 2) * eb + r
                w[rw:rw + PACKED_ROWS, :] = _gelu_tanh(act[r:r + PACKED_ROWS, :].astype(BF16)) * gsum

    mm1(0)
    for k in range(nblk):
        if k + 1 < nblk:
            mm1(k + 1)
        if k >= 2:
            mm2(k // 2 - 1, k % 2)
        gates(k)
    mm2(nblk // 2 - 1, 0)
    mm2(nblk // 2 - 1, 1)

    @pl.when(e == pl.num_programs(1) - 1)
    def _():
        ffn = acc_scr[...].T
        o_ref[...] = _layer_norm(alpha * x_ref[...] + ffn, g_ref[...], b_ref[...])


def _peer_mix(xt, x, u, vt, rank, bexp, cnt, arow, ln_g, ln_b, alpha, tt, te, eb):
    t, d = x.shape
    n_exp = u.shape[0]
    nblk = te // eb
    kern = functools.partial(_peer_kernel, tt=tt, nblk=nblk, eb=eb, alpha=alpha)
    big = pl.BlockSpec((PEER_HEADS, N_KEYS, tt), lambda i, e: (0, 0, i))
    rows = pl.BlockSpec((PEER_HEADS, te // N_KEYS, tt), lambda i, e: (0, e, i))
    vec = pl.BlockSpec((1, d), lambda i, e: (0, 0))
    return pl.pallas_call(
        kern,
        grid=(t // tt, n_exp // te),
        in_specs=[pl.BlockSpec((d, tt), lambda i, e: (0, i)),
                  pl.BlockSpec((tt, d), lambda i, e: (i, 0)),
                  pl.BlockSpec((te, d), lambda i, e: (e, 0)),
                  pl.BlockSpec((nblk // 2, d, 2 * eb), lambda i, e: (e, 0, 0)),
                  big, big, rows, rows, vec, vec],
        out_specs=pl.BlockSpec((tt, d), lambda i, e: (i, 0)),
        out_shape=jax.ShapeDtypeStruct((t, d), F32),
        scratch_shapes=[pltpu.VMEM((d, tt), F32),
                        pltpu.VMEM((eb, tt), F32), pltpu.VMEM((eb, tt), F32),
                        pltpu.VMEM((2 * eb, tt), BF16), pltpu.VMEM((2 * eb, tt), BF16),
                        pltpu.VMEM((2, PEER_HEADS, PACKED_ROWS, tt), BF16)],
        compiler_params=_params("parallel", "arbitrary"),
        name="peer_mix_ln",
    )(xt, x, u, vt, rank, bexp, cnt, arow, ln_g, ln_b)


def _tiles(b, s, t):
    return dict(mm_tm=min(2048, t), mm_tn=512, attn_tb=min(256, s), attn_hp=4,
                lru_ts=min(256, s), merge_tm=min(256, t), sel_tt=min(256, t),
                mix_tt=min(512, t), mix_te=2048, mix_eb=256)


def kernel(x, w_in, lambda_qk, subln_g, conv_w, conv_b, gate_a_w, gate_a_b, gate_x_w, gate_x_b,
           lru_lambda, w_br_attn, w_br_lru, w_out, ln1_g, ln1_b, peer_wq, peer_subkeys, peer_u,
           peer_v, ln2_g, ln2_b):
    b, s, d = x.shape
    t = b * s
    depth = w_in.shape[0]
    alpha = (2.0 * depth) ** 0.25
    attn_w = ATTN_HEADS * HEAD_W
    ts = _tiles(b, s, t)
    slopes = jnp.exp2(-8.0 * jnp.arange(1, ATTN_HEADS + 1, dtype=F32) / ATTN_HEADS)
    row = lambda a: a.reshape(1, -1)
    eb = ts["mix_eb"]

    xf = x.reshape(t, d)
    for l in range(depth):
        lam_init = 0.8 - 0.6 * math.exp(-0.3 * l)
        w_l = w_in[l].astype(BF16)
        w_cat = jnp.concatenate([w_l[:, :2 * attn_w], w_l[:, 3 * attn_w:]], axis=1)
        proj = _matmul(xf, w_cat, BF16, ts["mm_tm"], ts["mm_tn"])
        vt = _value_proj_t(xf, w_l[:, 2 * attn_w:3 * attn_w].T, ts["attn_tb"])
        proj3 = proj.reshape(b, s, -1)
        ya = _diff_attention(proj3, vt, slopes, lambda_qk[l], row(subln_g[l]),
                             lam_init, ts["attn_tb"], ts["attn_hp"])
        yr = _rg_lru(proj3, conv_w[l], row(conv_b[l]), gate_a_w[l].astype(BF16),
                     row(gate_a_b[l]), gate_x_w[l].astype(BF16), row(gate_x_b[l]),
                     row(lru_lambda[l]), ts["lru_ts"])
        x1, x1t = _merge_out(ya.reshape(t, attn_w), yr.reshape(t, d), proj, xf,
                             w_br_attn[l].astype(BF16), w_br_lru[l].astype(BF16),
                             w_out[l].astype(BF16), row(ln1_g[l]), row(ln1_b[l]), alpha,
                             ts["merge_tm"])
        rank, bexp, cnt, arow = _peer_select(x1t, peer_wq[l].T.astype(BF16),
                                             peer_subkeys[l].astype(BF16), ts["sel_tt"])
        v_blocks = peer_v[l].astype(BF16).reshape(-1, 2 * eb, d).transpose(0, 2, 1)
        xf = _peer_mix(x1t, x1, peer_u[l].astype(BF16), v_blocks, rank, bexp, cnt, arow,
                       row(ln2_g[l]), row(ln2_b[l]), alpha, ts["mix_tt"], ts["mix_te"], eb)
    return xf.reshape(b, s, d)
```

```python
import functools
import math

import jax
import jax.numpy as jnp
from jax import lax
from jax.experimental import pallas as pl
from jax.experimental.pallas import tpu as pltpu

F32 = jnp.float32
BF16 = jnp.bfloat16

ATTN_HEADS = 8
HEAD_DIM = 64
HEAD_W = 2 * HEAD_DIM
LRU_BLOCKS = 8
LRU_C = 8.0
CONV_W = 4
PEER_HEADS = 8
N_KEYS = 128
PEER_TOPK = 16
LN_EPS = 1e-5
RMS_EPS = 1e-6
PROJ_LRU_COL = 2
PROJ_GATE_COL = 4
SUBLANES = 8
PACKED_ROWS = 16
V_AUG_ROWS = PACKED_ROWS
VMEM_LIMIT = 56 * 1024 * 1024

_NT = (((1,), (1,)), ((), ()))


def _params(*sem):
    return pltpu.CompilerParams(dimension_semantics=sem, vmem_limit_bytes=VMEM_LIMIT)


MM_ROWS = 256


def _mm_kernel(x_ref, w_ref, o_ref):
    for r in range(0, x_ref.shape[0], MM_ROWS):
        o_ref[r:r + MM_ROWS, :] = jnp.dot(x_ref[r:r + MM_ROWS, :].astype(BF16), w_ref[...],
                                          preferred_element_type=F32).astype(o_ref.dtype)


def _matmul(x, w, out_dtype, tm, tn):
    t, k = x.shape
    n = w.shape[1]
    return pl.pallas_call(
        _mm_kernel,
        grid=(t // tm, n // tn),
        in_specs=[pl.BlockSpec((tm, k), lambda i, j: (i, 0)),
                  pl.BlockSpec((k, tn), lambda i, j: (0, j))],
        out_specs=pl.BlockSpec((tm, tn), lambda i, j: (i, j)),
        out_shape=jax.ShapeDtypeStruct((t, n), out_dtype),
        compiler_params=_params("parallel", "arbitrary"),
        name="in_proj",
    )(x, w)


def _vt_kernel(x_ref, w_ref, o_ref):
    vt = lax.dot_general(w_ref[...], x_ref[...].astype(BF16), _NT,
                         preferred_element_type=F32)
    tk = vt.shape[1]
    vt = vt.astype(o_ref.dtype).reshape(ATTN_HEADS, HEAD_W, tk)
    ones_row = lax.broadcasted_iota(jnp.int32, (ATTN_HEADS, V_AUG_ROWS, tk), 1) == 0
    extra = jnp.where(ones_row, 1.0, 0.0).astype(o_ref.dtype)
    o_ref[...] = jnp.concatenate([vt, extra], axis=1).reshape(o_ref.shape)


def _value_proj_t(x, w_t, tk):
    t, d = x.shape
    rows = HEAD_W + V_AUG_ROWS
    return pl.pallas_call(
        _vt_kernel,
        grid=(t // tk,),
        in_specs=[pl.BlockSpec((tk, d), lambda i: (i, 0)),
                  pl.BlockSpec(w_t.shape, lambda i: (0, 0))],
        out_specs=pl.BlockSpec((ATTN_HEADS, 1, rows, tk), lambda i: (0, i, 0, 0)),
        out_shape=jax.ShapeDtypeStruct((ATTN_HEADS, t // tk, rows, tk), BF16),
        compiler_params=_params("parallel"),
        name="value_proj_t",
    )(x, w_t)


def _attn_kernel(slopes_ref, lqk_ref, g_ref, q_ref, k_ref, vt_ref, o_ref, *scr, tb, hp, lam_init):
    acc_scr, st_scr, ka_scr = scr[:hp], scr[hp:2 * hp], scr[2 * hp:]
    hg = pl.program_id(1)
    qi = pl.program_id(2)
    s = k_ref.shape[1]
    lq = lqk_ref[...]
    lam = (jnp.exp(jnp.sum(lq[0:1] * lq[1:2], axis=1, keepdims=True))
           - jnp.exp(jnp.sum(lq[2:3] * lq[3:4], axis=1, keepdims=True)) + lam_init)
    slopes = [slopes_ref[hg * hp + hh] for hh in range(hp)]

    @pl.when(qi == 0)
    def _():
        lane = lax.broadcasted_iota(jnp.int32, (s, HEAD_W), 1)
        c = (lax.broadcasted_iota(jnp.int32, (s, HEAD_W), 0) & (tb - 1)).astype(F32)
        for hh in range(hp):
            extra = jnp.where(lane == 0, 1.0, jnp.where(lane == 1, slopes[hh] * c, 0.0))
            ka_scr[hh][:, :HEAD_W] = k_ref[0, :, hh * HEAD_W:(hh + 1) * HEAD_W]
            ka_scr[hh][:, HEAD_W:] = extra.astype(BF16)

    lane = lax.broadcasted_iota(jnp.int32, (tb, HEAD_W), 1)
    lane2 = lax.broadcasted_iota(jnp.int32, (2 * tb, HEAD_W), 1)
    r = (lax.broadcasted_iota(jnp.int32, (2 * tb, HEAD_W), 0) & (tb - 1)).astype(F32)
    masked = ((lax.broadcasted_iota(jnp.int32, (tb, 2 * tb), 1) & (tb - 1))
              < lax.broadcasted_iota(jnp.int32, (tb, 2 * tb), 0))
    qqs = []
    for hh in range(hp):
        qs = q_ref[0, :, hh * HEAD_W:(hh + 1) * HEAD_W] * (HEAD_DIM ** -0.5)
        zero = jnp.zeros_like(qs)
        qq = jnp.concatenate([jnp.where(lane < HEAD_DIM, qs, zero),
                              jnp.where(lane >= HEAD_DIM, qs, zero)], axis=0)
        extra = jnp.where(lane2 == 0, -slopes[hh] * r, jnp.where(lane2 == 1, 1.0, 0.0))
        qqs.append(jnp.concatenate([qq, extra.astype(BF16)], axis=1))
        acc_scr[hh][...] = jnp.zeros_like(acc_scr[hh])

    def scores(j, hh):
        start = pl.multiple_of(j * tb, tb)
        st_scr[hh][...] = lax.dot_general(ka_scr[hh][pl.ds(start, tb), :], qqs[hh], _NT,
                                          preferred_element_type=F32)

    def step(j, ms, last):
        shift = (qi - j) * tb
        out = []
        for hh in range(hp):
            off = slopes[hh] * shift.astype(F32)
            st = st_scr[hh][...]
            if last:
                st = jnp.where(masked, -jnp.inf, st)
            m_new = jnp.maximum(ms[hh], jnp.max(st, axis=0, keepdims=True) - off)
            pt = jnp.exp(st - (m_new + off)).astype(BF16)
            a = jnp.exp(ms[hh] - m_new)
            if not last:
                scores(j + 1, hh)
            acc_scr[hh][...] = a * acc_scr[hh][...] + jnp.dot(vt_ref[hh, j], pt,
                                                              preferred_element_type=F32)
            out.append(m_new)
        return tuple(out)

    for hh in range(hp):
        scores(0, hh)
    ms = tuple(jnp.full((1, 2 * tb), -jnp.inf, F32) for _ in range(hp))
    ms = lax.fori_loop(0, qi, functools.partial(step, last=False), ms)
    step(qi, ms, last=True)

    for hh in range(hp):
        acc = acc_scr[hh][...]
        ot = acc[:HEAD_W] / acc[HEAD_W:HEAD_W + 1]
        o = (ot[:, :tb] - lam * ot[:, tb:]).T
        o = o * lax.rsqrt(jnp.mean(o * o, axis=-1, keepdims=True) + RMS_EPS)
        o = o * g_ref[...] * (1.0 - lam_init)
        o_ref[0, :, hh * HEAD_W:(hh + 1) * HEAD_W] = o.astype(o_ref.dtype)


def _diff_attention(qk, vt, slopes, lambda_qk, subln_g, lam_init, tb, hp):
    b, s, _ = qk.shape
    assert tb & (tb - 1) == 0 and tb <= 256
    kern = functools.partial(_attn_kernel, tb=tb, hp=hp, lam_init=lam_init)
    gw = hp * HEAD_W
    vrows = vt.shape[2]
    return pl.pallas_call(
        kern,
        grid=(b, ATTN_HEADS // hp, s // tb),
        in_specs=[pl.BlockSpec(memory_space=pltpu.SMEM),
                  pl.BlockSpec((4, HEAD_DIM), lambda b_, h, i: (0, 0)),
                  pl.BlockSpec((1, HEAD_W), lambda b_, h, i: (0, 0)),
                  pl.BlockSpec((1, tb, gw), lambda b_, h, i: (b_, i, h)),
                  pl.BlockSpec((1, s, gw), lambda b_, h, i: (b_, 0, ATTN_HEADS // hp + h)),
                  pl.BlockSpec((hp, s // tb, vrows, tb), lambda b_, h, i: (h, b_, 0, 0))],
        out_specs=pl.BlockSpec((1, tb, gw), lambda b_, h, i: (b_, i, h)),
        out_shape=jax.ShapeDtypeStruct((b, s, ATTN_HEADS * HEAD_W), BF16),
        scratch_shapes=([pltpu.VMEM((vrows, 2 * tb), F32)] * hp
                        + [pltpu.VMEM((tb, 2 * tb), F32)] * hp
                        + [pltpu.VMEM((s, 2 * HEAD_W), BF16)] * hp),
        compiler_params=_params("parallel", "parallel", "arbitrary"),
        name="diff_attn",
    )(slopes, lambda_qk, subln_g, qk, qk, vt)


def _gelu(x):
    return jax.nn.gelu(x)


def _gelu_tanh(x):
    c = math.sqrt(2.0 / math.pi)
    hx = 0.5 * x
    return hx + hx * jnp.tanh(x * (c + (c * 0.044715) * (x * x)))


def _lru_kernel(xr_ref, gr_ref, cw_ref, cb_ref, gaw_ref, gab_ref, gxw_ref, gxb_ref, lam_ref,
                o_ref, tail_scr, h_scr, a_scr, u_scr, gg_scr, *, ts):
    si = pl.program_id(1)

    @pl.when(si == 0)
    def _():
        tail_scr[...] = jnp.zeros_like(tail_scr)
        h_scr[...] = jnp.zeros_like(h_scr)

    x = xr_ref[0].astype(F32)
    w = x.shape[1]
    tail = tail_scr[...]
    row8 = lax.broadcasted_iota(jnp.int32, (SUBLANES, w), 0)
    cw = cw_ref[...]
    xb = cb_ref[...]
    for d in range(CONV_W - 1, 0, -1):
        xs = pltpu.roll(x, d, 0)
        top = jnp.where(row8 < d, pltpu.roll(tail, d, 0), xs[:SUBLANES])
        xs = jnp.concatenate([top, xs[SUBLANES:]], axis=0)
        xb = xb + xs * cw[CONV_W - 1 - d:CONV_W - d]
    xb = xb + x * cw[CONV_W - 1:CONV_W]
    tail_scr[...] = x[ts - SUBLANES:]

    xbb = xb.astype(BF16)
    bw = w // LRU_BLOCKS

    def gate(w_ref, b_ref):
        parts = [jnp.dot(xbb[:, g * bw:(g + 1) * bw], w_ref[g], preferred_element_type=F32)
                 for g in range(LRU_BLOCKS)]
        return jax.nn.sigmoid(jnp.concatenate(parts, axis=1) + b_ref[...])

    r = gate(gaw_ref, gab_ref)
    i = gate(gxw_ref, gxb_ref)
    lam = lam_ref[...]
    softplus_neg = jnp.maximum(-lam, 0.0) + jnp.log1p(jnp.exp(-jnp.abs(lam)))
    log_a = -LRU_C * r * softplus_neg
    a = jnp.exp(log_a)
    mult = jnp.sqrt(1.0 - jnp.exp(2.0 * log_a))
    rows = lax.broadcasted_iota(jnp.int32, (ts, w), 0)
    mult = jnp.where(jnp.logical_and(rows == 0, si == 0), 1.0, mult)
    a_scr[...] = a
    u_scr[...] = mult * (i * xb)
    gg_scr[...] = _gelu(gr_ref[0].astype(F32))

    def group(g, hprev):
        r0 = pl.multiple_of(g * SUBLANES, SUBLANES)
        ag = a_scr[pl.ds(r0, SUBLANES), :]
        ug = u_scr[pl.ds(r0, SUBLANES), :]
        for d in (1, 2, 4):
            keep = row8 >= d
            ug = jnp.where(keep, ag * pltpu.roll(ug, d, 0) + ug, ug)
            ag = jnp.where(keep, ag * pltpu.roll(ag, d, 0), ag)
        hg = ag * hprev + ug
        o_ref[0, pl.ds(r0, SUBLANES), :] = (hg * gg_scr[pl.ds(r0, SUBLANES), :]).astype(o_ref.dtype)
        return jnp.broadcast_to(hg[SUBLANES - 1:SUBLANES, :], (SUBLANES, w))

    h_scr[...] = lax.fori_loop(0, ts // SUBLANES, group, h_scr[...])


def _rg_lru(rest, conv_w, conv_b, gaw, gab, gxw, gxb, lru_lambda, ts):
    b, s, _ = rest.shape
    w = conv_w.shape[1]
    kern = functools.partial(_lru_kernel, ts=ts)
    vec = pl.BlockSpec((1, w), lambda b_, i: (0, 0))
    blk = pl.BlockSpec(gaw.shape, lambda b_, i: (0, 0, 0))
    return pl.pallas_call(
        kern,
        grid=(b, s // ts),
        in_specs=[pl.BlockSpec((1, ts, w), lambda b_, i: (b_, i, PROJ_LRU_COL)),
                  pl.BlockSpec((1, ts, w), lambda b_, i: (b_, i, PROJ_LRU_COL + 1)),
                  pl.BlockSpec((CONV_W, w), lambda b_, i: (0, 0)),
                  vec, blk, vec, blk, vec, vec],
        out_specs=pl.BlockSpec((1, ts, w), lambda b_, i: (b_, i, 0)),
        out_shape=jax.ShapeDtypeStruct((b, s, w), BF16),
        scratch_shapes=[pltpu.VMEM((SUBLANES, w), F32), pltpu.VMEM((SUBLANES, w), F32),
                        pltpu.VMEM((ts, w), F32), pltpu.VMEM((ts, w), F32),
                        pltpu.VMEM((ts, w), F32)],
        compiler_params=_params("parallel", "arbitrary"),
        name="rg_lru",
    )(rest, rest, conv_w, conv_b, gaw, gab, gxw, gxb, lru_lambda)


def _layer_norm(z, g, b):
    mu = jnp.mean(z, axis=-1, keepdims=True)
    zc = z - mu
    var = jnp.mean(zc * zc, axis=-1, keepdims=True)
    return zc * lax.rsqrt(var + LN_EPS) * g + b


def _merge_kernel(ya_ref, yr_ref, ga_ref, gl_ref, x_ref, wa_ref, wl_ref, wo_ref, g_ref, b_ref,
                  o_ref, ob_ref, *, alpha):
    pa = jnp.dot(ya_ref[...], wa_ref[...], preferred_element_type=F32)
    pr = jnp.dot(yr_ref[...], wl_ref[...], preferred_element_type=F32)
    merged = (jax.nn.sigmoid(ga_ref[...].astype(F32)) * pa
              + jax.nn.sigmoid(gl_ref[...].astype(F32)) * pr)
    mix = jnp.dot(merged.astype(BF16), wo_ref[...], preferred_element_type=F32)
    y = _layer_norm(alpha * x_ref[...] + mix, g_ref[...], b_ref[...])
    o_ref[...] = y
    ob_ref[...] = y.T.astype(BF16)


def _merge_out(ya, yr, rest, x, wa, wl, wo, ln_g, ln_b, alpha, tm):
    t, d = x.shape
    kern = functools.partial(_merge_kernel, alpha=alpha)
    rows = lambda c: pl.BlockSpec((tm, d), lambda i: (i, c))
    full = pl.BlockSpec((d, d), lambda i: (0, 0))
    vec = pl.BlockSpec((1, d), lambda i: (0, 0))
    return pl.pallas_call(
        kern,
        grid=(t // tm,),
        in_specs=[rows(0), rows(0), rows(PROJ_GATE_COL), rows(PROJ_GATE_COL + 1), rows(0),
                  full, full, full, vec, vec],
        out_specs=[rows(0), pl.BlockSpec((d, tm), lambda i: (0, i))],
        out_shape=[jax.ShapeDtypeStruct((t, d), F32), jax.ShapeDtypeStruct((d, t), BF16)],
        compiler_params=_params("parallel"),
        name="merge_out_ln",
    )(ya, yr, rest, rest, x, wa, wl, wo, ln_g, ln_b)


def _cmpx(v, i, j):
    hi = jnp.maximum(v[i], v[j])
    v[j] = jnp.minimum(v[i], v[j])
    v[i] = hi


def _bitonic_merge_desc(v):
    n = len(v)
    j = n // 2
    while j >= 1:
        for i in range(n):
            if i & j == 0:
                _cmpx(v, i, i + j)
        j //= 2


def _bitonic_sort_desc(v):
    n = len(v)
    k = 2
    while k <= n:
        j = k // 2
        while j >= 1:
            for i in range(n):
                l = i ^ j
                if l > i:
                    if i & k == 0:
                        _cmpx(v, i, l)
                    else:
                        _cmpx(v, l, i)
            j //= 2
        k *= 2


def _top_merge(a, b):
    n = len(a)
    v = [jnp.maximum(a[i], b[n - 1 - i]) for i in range(n)]
    _bitonic_merge_desc(v)
    return v


def _prefix_count(b, test):
    assert len(b) == 16
    c8 = test(b[7])
    n = jnp.where(c8, 8.0, 0.0)
    c4 = test(jnp.where(c8, b[11], b[3]))
    n = n + jnp.where(c4, 4.0, 0.0)
    c2 = test(jnp.where(c8, jnp.where(c4, b[13], b[9]), jnp.where(c4, b[5], b[1])))
    n = n + jnp.where(c2, 2.0, 0.0)
    lo = jnp.where(c4, jnp.where(c2, b[6], b[4]), jnp.where(c2, b[2], b[0]))
    hi = jnp.where(c4, jnp.where(c2, b[14], b[12]), jnp.where(c2, b[10], b[8]))
    n = n + jnp.where(test(jnp.where(c8, hi, lo)), 1.0, 0.0)
    return jnp.where(test(b[15]), 16.0, n)


_CAND_PAIRS = [(p, q) for p in range(PEER_TOPK) for q in range(PEER_TOPK)
               if (p + 1) * (q + 1) <= PEER_TOPK]


def _select_kernel(x_ref, wq_ref, sk_ref, rank_ref, bexp_ref, cnt_ref, arow_ref,
                   q_scr, s_scr, pk_scr, t_scr, *, tt):
    qt = jnp.dot(wq_ref[...], x_ref[...], preferred_element_type=F32)
    q_scr[...] = qt.astype(BF16)
    dk = sk_ref.shape[3]
    n_grp = N_KEYS // SUBLANES

    def head(h, _):
        for c in range(2):
            r0 = pl.multiple_of((h * 2 + c) * dk, dk)
            sc = jnp.dot(sk_ref[h, c], q_scr[pl.ds(r0, dk), :], preferred_element_type=F32)
            s_scr[c, h] = sc
            v = [sc[g * SUBLANES:(g + 1) * SUBLANES, :] for g in range(n_grp)]
            _bitonic_sort_desc(v)
            for sh in (4, 2, 1):
                v = _top_merge(v, [pltpu.roll(e, sh, 0) for e in v])
            for p in range(PEER_TOPK):
                pk_scr[c, p, pl.ds(h, 1), :] = v[p][0:1, :]
        return 0

    lax.fori_loop(0, PEER_HEADS, head, 0)

    a = [pk_scr[0, p] for p in range(PEER_TOPK)]
    b = [pk_scr[1, p] for p in range(PEER_TOPK)]
    cands = [a[p] + b[q] for p, q in _CAND_PAIRS]
    neg = jnp.full_like(cands[0], -jnp.inf)
    blocks = cands + [neg] * ((-len(cands)) % PEER_TOPK)
    top = None
    for i0 in range(0, len(blocks), PEER_TOPK):
        blk = blocks[i0:i0 + PEER_TOPK]
        _bitonic_sort_desc(blk)
        top = blk if top is None else _top_merge(top, blk)
    tau = top[PEER_TOPK - 1]
    mx = a[0] + b[0]
    z = jnp.zeros_like(tau)
    for cnd in cands:
        z = z + jnp.where(cnd >= tau, jnp.exp(cnd - mx), 0.0)
    t_scr[0] = tau
    t_scr[1] = a[0] + jnp.log(z)

    def head2(h, _):
        s1 = s_scr[0, h]
        s2 = s_scr[1, h]
        tau_h = t_scr[0, pl.ds(h, 1), :]
        bs = [pk_scr[1, q, pl.ds(h, 1), :] for q in range(PEER_TOPK)]
        rank_ref[h] = _prefix_count(bs, lambda thr: thr > s2).astype(BF16)
        cnt_ref[h] = _prefix_count(bs, lambda thr: s1 + thr >= tau_h)
        arow_ref[h] = jnp.exp(s1 - t_scr[1, pl.ds(h, 1), :])
        bexp_ref[h] = jnp.exp(s2 - pk_scr[1, 0, pl.ds(h, 1), :]).astype(BF16)
        return 0

    lax.fori_loop(0, PEER_HEADS, head2, 0)


def _peer_select(xt, wq_t, sk, tt):
    d, t = xt.shape
    nq = wq_t.shape[0]
    kern = functools.partial(_select_kernel, tt=tt)
    big = pl.BlockSpec((PEER_HEADS, N_KEYS, tt), lambda i: (0, 0, i))
    shape = lambda dt: jax.ShapeDtypeStruct((PEER_HEADS, N_KEYS, t), dt)
    return pl.pallas_call(
        kern,
        grid=(t // tt,),
        in_specs=[pl.BlockSpec((d, tt), lambda i: (0, i)),
                  pl.BlockSpec((nq, d), lambda i: (0, 0)),
                  pl.BlockSpec(sk.shape, lambda i: (0, 0, 0, 0))],
        out_specs=[big, big, big, big],
        out_shape=[shape(BF16), shape(BF16), shape(F32), shape(F32)],
        scratch_shapes=[pltpu.VMEM((nq, tt), BF16),
                        pltpu.VMEM((2, PEER_HEADS, N_KEYS, tt), F32),
                        pltpu.VMEM((2, PEER_TOPK, PEER_HEADS, tt), F32),
                        pltpu.VMEM((2, PEER_HEADS, tt), F32)],
        compiler_params=_params("parallel"),
        name="peer_select",
    )(xt, wq_t, sk)


def _peer_kernel(xt_ref, x_ref, u_ref, vt_ref, rank_ref, bexp_ref, cnt_ref, arow_ref,
                 g_ref, b_ref, o_ref, acc_scr, act0, act1, w0, w1, row_scr, *, tt, nblk, eb, alpha):
    e = pl.program_id(1)

    @pl.when(e == 0)
    def _():
        acc_scr[...] = jnp.zeros_like(acc_scr)

    acts = (act0, act1)
    ws = (w0, w1)
    rows_per_blk = eb // N_KEYS
    half = acc_scr.shape[0] // 2

    def mm1(k):
        acts[k % 2][...] = jnp.dot(u_ref[k * eb:(k + 1) * eb, :], xt_ref[...],
                                   preferred_element_type=F32)

    def mm2(p, part):
        rows = slice(part * half, (part + 1) * half)
        acc_scr[rows, :] += jnp.dot(vt_ref[p, rows, :], ws[p % 2][...], preferred_element_type=F32)

    def gates(k):
        act, w = acts[k % 2], ws[(k // 2) % 2]
        for ii in range(rows_per_blk):
            i = k * rows_per_blk + ii
            for h in range(PEER_HEADS):
                for a, ref in enumerate((cnt_ref, arow_ref)):
                    row_scr[a, h] = jnp.broadcast_to(ref[h, i:i + 1, :], (PACKED_ROWS, tt)).astype(BF16)
            for j0 in range(0, N_KEYS, PACKED_ROWS):
                r = ii * N_KEYS + j0
                rw = (k % 2) * eb + r
                gsum = None
                for h in range(PEER_HEADS):
                    sel = rank_ref[h, j0:j0 + PACKED_ROWS, :] < row_scr[0, h]
                    term = jnp.where(sel, row_scr[1, h] * bexp_ref[h, j0:j0 + PACKED_ROWS, :],
                                     jnp.zeros((), BF16))
                    gsum = term if gsum is None else gsum + term
                w[rw:rw + PACKED_ROWS, :] = _gelu_tanh(act[r:r + PACKED_ROWS, :].astype(BF16)) * gsum

    mm1(0)
    for k in range(nblk):
        if k + 1 < nblk:
            mm1(k + 1)
        if k >= 2:
            mm2(k // 2 - 1, k % 2)
        gates(k)
    mm2(nblk // 2 - 1, 0)
    mm2(nblk // 2 - 1, 1)

    @pl.when(e == pl.num_programs(1) - 1)
    def _():
        ffn = acc_scr[...].T
        o_ref[...] = _layer_norm(alpha * x_ref[...] + ffn, g_ref[...], b_ref[...])


def _peer_mix(xt, x, u, vt, rank, bexp, cnt, arow, ln_g, ln_b, alpha, tt, te, eb):
    t, d = x.shape
    n_exp = u.shape[0]
    nblk = te // eb
    kern = functools.partial(_peer_kernel, tt=tt, nblk=nblk, eb=eb, alpha=alpha)
    big = pl.BlockSpec((PEER_HEADS, N_KEYS, tt), lambda i, e: (0, 0, i))
    rows = pl.BlockSpec((PEER_HEADS, te // N_KEYS, tt), lambda i, e: (0, e, i))
    vec = pl.BlockSpec((1, d), lambda i, e: (0, 0))
    return pl.pallas_call(
        kern,
        grid=(t // tt, n_exp // te),
        in_specs=[pl.BlockSpec((d, tt), lambda i, e: (0, i)),
                  pl.BlockSpec((tt, d), lambda i, e: (i, 0)),
                  pl.BlockSpec((te, d), lambda i, e: (e, 0)),
                  pl.BlockSpec((nblk // 2, d, 2 * eb), lambda i, e: (e, 0, 0)),
                  big, big, rows, rows, vec, vec],
        out_specs=pl.BlockSpec((tt, d), lambda i, e: (i, 0)),
        out_shape=jax.ShapeDtypeStruct((t, d), F32),
        scratch_shapes=[pltpu.VMEM((d, tt), F32),
                        pltpu.VMEM((eb, tt), F32), pltpu.VMEM((eb, tt), F32),
                        pltpu.VMEM((2 * eb, tt), BF16), pltpu.VMEM((2 * eb, tt), BF16),
                        pltpu.VMEM((2, PEER_HEADS, PACKED_ROWS, tt), BF16)],
        compiler_params=_params("parallel", "arbitrary"),
        name="peer_mix_ln",
    )(xt, x, u, vt, rank, bexp, cnt, arow, ln_g, ln_b)


def _tiles(b, s, t):
    return dict(mm_tm=min(2048, t), mm_tn=512, attn_tb=min(256, s), attn_hp=4,
                lru_ts=min(256, s), merge_tm=min(256, t), sel_tt=min(512, t),
                mix_tt=min(512, t), mix_te=2048, mix_eb=256)


def kernel(x, w_in, lambda_qk, subln_g, conv_w, conv_b, gate_a_w, gate_a_b, gate_x_w, gate_x_b,
           lru_lambda, w_br_attn, w_br_lru, w_out, ln1_g, ln1_b, peer_wq, peer_subkeys, peer_u,
           peer_v, ln2_g, ln2_b):
    b, s, d = x.shape
    t = b * s
    depth = w_in.shape[0]
    alpha = (2.0 * depth) ** 0.25
    attn_w = ATTN_HEADS * HEAD_W
    ts = _tiles(b, s, t)
    slopes = jnp.exp2(-8.0 * jnp.arange(1, ATTN_HEADS + 1, dtype=F32) / ATTN_HEADS)
    row = lambda a: a.reshape(1, -1)
    eb = ts["mix_eb"]

    xf = x.reshape(t, d)
    for l in range(depth):
        lam_init = 0.8 - 0.6 * math.exp(-0.3 * l)
        w_l = w_in[l].astype(BF16)
        w_cat = jnp.concatenate([w_l[:, :2 * attn_w], w_l[:, 3 * attn_w:]], axis=1)
        proj = _matmul(xf, w_cat, BF16, ts["mm_tm"], ts["mm_tn"])
        vt = _value_proj_t(xf, w_l[:, 2 * attn_w:3 * attn_w].T, ts["attn_tb"])
        proj3 = proj.reshape(b, s, -1)
        ya = _diff_attention(proj3, vt, slopes, lambda_qk[l], row(subln_g[l]),
                             lam_init, ts["attn_tb"], ts["attn_hp"])
        yr = _rg_lru(proj3, conv_w[l], row(conv_b[l]), gate_a_w[l].astype(BF16),
                     row(gate_a_b[l]), gate_x_w[l].astype(BF16), row(gate_x_b[l]),
                     row(lru_lambda[l]), ts["lru_ts"])
        x1, x1t = _merge_out(ya.reshape(t, attn_w), yr.reshape(t, d), proj, xf,
                             w_br_attn[l].astype(BF16), w_br_lru[l].astype(BF16),
                             w_out[l].astype(BF16), row(ln1_g[l]), row(ln1_b[l]), alpha,
                             ts["merge_tm"])
        rank, bexp, cnt, arow = _peer_select(x1t, peer_wq[l].T.astype(BF16),
                                             peer_subkeys[l].astype(BF16), ts["sel_tt"])
        v_blocks = peer_v[l].astype(BF16).reshape(-1, 2 * eb, d).transpose(0, 2, 1)
        xf = _peer_mix(x1t, x1, peer_u[l].astype(BF16), v_blocks, rank, bexp, cnt, arow,
                       row(ln2_g[l]), row(ln2_b[l]), alpha, ts["mix_tt"], ts["mix_te"], eb)
    return xf.reshape(b, s, d)
```

```python
import functools
import math

import jax
import jax.numpy as jnp
from jax import lax
from jax.experimental import pallas as pl
from jax.experimental.pallas import tpu as pltpu

F32 = jnp.float32
BF16 = jnp.bfloat16

ATTN_HEADS = 8
HEAD_DIM = 64
HEAD_W = 2 * HEAD_DIM
LRU_BLOCKS = 8
LRU_C = 8.0
CONV_W = 4
PEER_HEADS = 8
N_KEYS = 128
PEER_TOPK = 16
LN_EPS = 1e-5
RMS_EPS = 1e-6
PROJ_LRU_COL = 2
PROJ_GATE_COL = 4
SUBLANES = 8
PACKED_ROWS = 16
V_AUG_ROWS = PACKED_ROWS
VMEM_LIMIT = 56 * 1024 * 1024

_NT = (((1,), (1,)), ((), ()))


def _params(*sem):
    return pltpu.CompilerParams(dimension_semantics=sem, vmem_limit_bytes=VMEM_LIMIT)


MM_ROWS = 256


def _mm_kernel(x_ref, w_ref, o_ref):
    for r in range(0, x_ref.shape[0], MM_ROWS):
        o_ref[r:r + MM_ROWS, :] = jnp.dot(x_ref[r:r + MM_ROWS, :].astype(BF16), w_ref[...],
                                          preferred_element_type=F32).astype(o_ref.dtype)


def _matmul(x, w, out_dtype, tm, tn):
    t, k = x.shape
    n = w.shape[1]
    return pl.pallas_call(
        _mm_kernel,
        grid=(t // tm, n // tn),
        in_specs=[pl.BlockSpec((tm, k), lambda i, j: (i, 0)),
                  pl.BlockSpec((k, tn), lambda i, j: (0, j))],
        out_specs=pl.BlockSpec((tm, tn), lambda i, j: (i, j)),
        out_shape=jax.ShapeDtypeStruct((t, n), out_dtype),
        compiler_params=_params("parallel", "arbitrary"),
        name="in_proj",
    )(x, w)


def _vt_kernel(x_ref, w_ref, o_ref):
    vt = lax.dot_general(w_ref[...], x_ref[...].astype(BF16), _NT,
                         preferred_element_type=F32)
    tk = vt.shape[1]
    vt = vt.astype(o_ref.dtype).reshape(ATTN_HEADS, HEAD_W, tk)
    ones_row = lax.broadcasted_iota(jnp.int32, (ATTN_HEADS, V_AUG_ROWS, tk), 1) == 0
    extra = jnp.where(ones_row, 1.0, 0.0).astype(o_ref.dtype)
    o_ref[...] = jnp.concatenate([vt, extra], axis=1).reshape(o_ref.shape)


def _value_proj_t(x, w_t, tk):
    t, d = x.shape
    rows = HEAD_W + V_AUG_ROWS
    return pl.pallas_call(
        _vt_kernel,
        grid=(t // tk,),
        in_specs=[pl.BlockSpec((tk, d), lambda i: (i, 0)),
                  pl.BlockSpec(w_t.shape, lambda i: (0, 0))],
        out_specs=pl.BlockSpec((ATTN_HEADS, 1, rows, tk), lambda i: (0, i, 0, 0)),
        out_shape=jax.ShapeDtypeStruct((ATTN_HEADS, t // tk, rows, tk), BF16),
        compiler_params=_params("parallel"),
        name="value_proj_t",
    )(x, w_t)


def _attn_kernel(slopes_ref, lqk_ref, g_ref, q_ref, k_ref, vt_ref, o_ref, *scr, tb, hp, lam_init):
    acc_scr, st_scr, ka_scr = scr[:hp], scr[hp:2 * hp], scr[2 * hp:]
    hg = pl.program_id(1)
    qi = pl.program_id(2)
    s = k_ref.shape[1]
    lq = lqk_ref[...]
    lam = (jnp.exp(jnp.sum(lq[0:1] * lq[1:2], axis=1, keepdims=True))
           - jnp.exp(jnp.sum(lq[2:3] * lq[3:4], axis=1, keepdims=True)) + lam_init)
    slopes = [slopes_ref[hg * hp + hh] for hh in range(hp)]

    @pl.when(qi == 0)
    def _():
        lane = lax.broadcasted_iota(jnp.int32, (s, HEAD_W), 1)
        c = (lax.broadcasted_iota(jnp.int32, (s, HEAD_W), 0) & (tb - 1)).astype(F32)
        for hh in range(hp):
            extra = jnp.where(lane == 0, 1.0, jnp.where(lane == 1, slopes[hh] * c, 0.0))
            ka_scr[hh][:, :HEAD_W] = k_ref[0, :, hh * HEAD_W:(hh + 1) * HEAD_W]
            ka_scr[hh][:, HEAD_W:] = extra.astype(BF16)

    lane = lax.broadcasted_iota(jnp.int32, (tb, HEAD_W), 1)
    lane2 = lax.broadcasted_iota(jnp.int32, (2 * tb, HEAD_W), 1)
    r = (lax.broadcasted_iota(jnp.int32, (2 * tb, HEAD_W), 0) & (tb - 1)).astype(F32)
    masked = ((lax.broadcasted_iota(jnp.int32, (tb, 2 * tb), 1) & (tb - 1))
              < lax.broadcasted_iota(jnp.int32, (tb, 2 * tb), 0))
    qqs = []
    for hh in range(hp):
        qs = q_ref[0, :, hh * HEAD_W:(hh + 1) * HEAD_W].astype(F32) * (HEAD_DIM ** -0.5)
        zero = jnp.zeros_like(qs)
        qq = jnp.concatenate([jnp.where(lane < HEAD_DIM, qs, zero),
                              jnp.where(lane >= HEAD_DIM, qs, zero)], axis=0)
        extra = jnp.where(lane2 == 0, -slopes[hh] * r, jnp.where(lane2 == 1, 1.0, 0.0))
        qqs.append(jnp.concatenate([qq, extra], axis=1).T.astype(BF16))
        acc_scr[hh][...] = jnp.zeros_like(acc_scr[hh])

    def scores(j, hh):
        start = pl.multiple_of(j * tb, tb)
        st_scr[hh][...] = jnp.dot(ka_scr[hh][pl.ds(start, tb), :], qqs[hh],
                                  preferred_element_type=F32)

    def step(j, ms, last):
        shift = (qi - j) * tb
        out = []
        for hh in range(hp):
            off = slopes[hh] * shift.astype(F32)
            st = st_scr[hh][...]
            if last:
                st = jnp.where(masked, -jnp.inf, st)
            m_new = jnp.maximum(ms[hh], jnp.max(st, axis=0, keepdims=True) - off)
            pt = jnp.exp(st - (m_new + off)).astype(BF16)
            a = jnp.exp(ms[hh] - m_new)
            if not last:
                scores(j + 1, hh)
            acc_scr[hh][...] = a * acc_scr[hh][...] + jnp.dot(vt_ref[hh, j], pt,
                                                              preferred_element_type=F32)
            out.append(m_new)
        return tuple(out)

    for hh in range(hp):
        scores(0, hh)
    ms = tuple(jnp.full((1, 2 * tb), -jnp.inf, F32) for _ in range(hp))
    ms = lax.fori_loop(0, qi, functools.partial(step, last=False), ms)
    step(qi, ms, last=True)

    for hh in range(hp):
        acc = acc_scr[hh][...]
        ot = acc[:HEAD_W] / acc[HEAD_W:HEAD_W + 1]
        o = (ot[:, :tb] - lam * ot[:, tb:]).T
        o = o * lax.rsqrt(jnp.mean(o * o, axis=-1, keepdims=True) + RMS_EPS)
        o = o * g_ref[...] * (1.0 - lam_init)
        o_ref[0, :, hh * HEAD_W:(hh + 1) * HEAD_W] = o.astype(o_ref.dtype)


def _diff_attention(qk, vt, slopes, lambda_qk, subln_g, lam_init, tb, hp):
    b, s, _ = qk.shape
    assert tb & (tb - 1) == 0 and tb <= 256
    kern = functools.partial(_attn_kernel, tb=tb, hp=hp, lam_init=lam_init)
    gw = hp * HEAD_W
    vrows = vt.shape[2]
    return pl.pallas_call(
        kern,
        grid=(b, ATTN_HEADS // hp, s // tb),
        in_specs=[pl.BlockSpec(memory_space=pltpu.SMEM),
                  pl.BlockSpec((4, HEAD_DIM), lambda b_, h, i: (0, 0)),
                  pl.BlockSpec((1, HEAD_W), lambda b_, h, i: (0, 0)),
                  pl.BlockSpec((1, tb, gw), lambda b_, h, i: (b_, i, h)),
                  pl.BlockSpec((1, s, gw), lambda b_, h, i: (b_, 0, ATTN_HEADS // hp + h),
                               pipeline_mode=pl.Buffered(1)),
                  pl.BlockSpec((hp, s // tb, vrows, tb), lambda b_, h, i: (h, b_, 0, 0),
                               pipeline_mode=pl.Buffered(1))],
        out_specs=pl.BlockSpec((1, tb, gw), lambda b_, h, i: (b_, i, h)),
        out_shape=jax.ShapeDtypeStruct((b, s, ATTN_HEADS * HEAD_W), BF16),
        scratch_shapes=([pltpu.VMEM((vrows, 2 * tb), F32)] * hp
                        + [pltpu.VMEM((tb, 2 * tb), F32)] * hp
                        + [pltpu.VMEM((s, 2 * HEAD_W), BF16)] * hp),
        compiler_params=_params("parallel", "parallel", "arbitrary"),
        name="diff_attn",
    )(slopes, lambda_qk, subln_g, qk, qk, vt)


def _gelu(x):
    return jax.nn.gelu(x)


def _gelu_tanh(x):
    c = math.sqrt(2.0 / math.pi)
    hx = 0.5 * x
    return hx + hx * jnp.tanh(x * (c + (c * 0.044715) * (x * x)))


def _lru_kernel(xr_ref, gr_ref, cw_ref, cb_ref, gaw_ref, gab_ref, gxw_ref, gxb_ref, lam_ref,
                o_ref, tail_scr, h_scr, a_scr, u_scr, gg_scr, *, ts):
    si = pl.program_id(1)

    @pl.when(si == 0)
    def _():
        tail_scr[...] = jnp.zeros_like(tail_scr)
        h_scr[...] = jnp.zeros_like(h_scr)

    x = xr_ref[0].astype(F32)
    w = x.shape[1]
    tail = tail_scr[...]
    row8 = lax.broadcasted_iota(jnp.int32, (SUBLANES, w), 0)
    cw = cw_ref[...]
    xb = cb_ref[...]
    for d in range(CONV_W - 1, 0, -1):
        xs = pltpu.roll(x, d, 0)
        top = jnp.where(row8 < d, pltpu.roll(tail, d, 0), xs[:SUBLANES])
        xs = jnp.concatenate([top, xs[SUBLANES:]], axis=0)
        xb = xb + xs * cw[CONV_W - 1 - d:CONV_W - d]
    xb = xb + x * cw[CONV_W - 1:CONV_W]
    tail_scr[...] = x[ts - SUBLANES:]

    xbb = xb.astype(BF16)
    bw = w // LRU_BLOCKS

    def gate(w_ref, b_ref):
        parts = [jnp.dot(xbb[:, g * bw:(g + 1) * bw], w_ref[g], preferred_element_type=F32)
                 for g in range(LRU_BLOCKS)]
        return jax.nn.sigmoid(jnp.concatenate(parts, axis=1) + b_ref[...])

    r = gate(gaw_ref, gab_ref)
    i = gate(gxw_ref, gxb_ref)
    lam = lam_ref[...]
    softplus_neg = jnp.maximum(-lam, 0.0) + jnp.log1p(jnp.exp(-jnp.abs(lam)))
    log_a = -LRU_C * r * softplus_neg
    a = jnp.exp(log_a)
    mult = jnp.sqrt(1.0 - jnp.exp(2.0 * log_a))
    rows = lax.broadcasted_iota(jnp.int32, (ts, w), 0)
    mult = jnp.where(jnp.logical_and(rows == 0, si == 0), 1.0, mult)
    a_scr[...] = a
    u_scr[...] = mult * (i * xb)
    gg_scr[...] = _gelu(gr_ref[0].astype(F32))

    def group(g, hprev):
        r0 = pl.multiple_of(g * SUBLANES, SUBLANES)
        ag = a_scr[pl.ds(r0, SUBLANES), :]
        ug = u_scr[pl.ds(r0, SUBLANES), :]
        for d in (1, 2, 4):
            keep = row8 >= d
            ug = jnp.where(keep, ag * pltpu.roll(ug, d, 0) + ug, ug)
            ag = jnp.where(keep, ag * pltpu.roll(ag, d, 0), ag)
        hg = ag * hprev + ug
        o_ref[0, pl.ds(r0, SUBLANES), :] = (hg * gg_scr[pl.ds(r0, SUBLANES), :]).astype(o_ref.dtype)
        return jnp.broadcast_to(hg[SUBLANES - 1:SUBLANES, :], (SUBLANES, w))

    h_scr[...] = lax.fori_loop(0, ts // SUBLANES, group, h_scr[...])


def _rg_lru(rest, conv_w, conv_b, gaw, gab, gxw, gxb, lru_lambda, ts):
    b, s, _ = rest.shape
    w = conv_w.shape[1]
    kern = functools.partial(_lru_kernel, ts=ts)
    vec = pl.BlockSpec((1, w), lambda b_, i: (0, 0))
    blk = pl.BlockSpec(gaw.shape, lambda b_, i: (0, 0, 0))
    return pl.pallas_call(
        kern,
        grid=(b, s // ts),
        in_specs=[pl.BlockSpec((1, ts, w), lambda b_, i: (b_, i, PROJ_LRU_COL)),
                  pl.BlockSpec((1, ts, w), lambda b_, i: (b_, i, PROJ_LRU_COL + 1)),
                  pl.BlockSpec((CONV_W, w), lambda b_, i: (0, 0)),
                  vec, blk, vec, blk, vec, vec],
        out_specs=pl.BlockSpec((1, ts, w), lambda b_, i: (b_, i, 0)),
        out_shape=jax.ShapeDtypeStruct((b, s, w), BF16),
        scratch_shapes=[pltpu.VMEM((SUBLANES, w), F32), pltpu.VMEM((SUBLANES, w), F32),
                        pltpu.VMEM((ts, w), F32), pltpu.VMEM((ts, w), F32),
                        pltpu.VMEM((ts, w), F32)],
        compiler_params=_params("parallel", "arbitrary"),
        name="rg_lru",
    )(rest, rest, conv_w, conv_b, gaw, gab, gxw, gxb, lru_lambda)


def _layer_norm(z, g, b):
    mu = jnp.mean(z, axis=-1, keepdims=True)
    zc = z - mu
    var = jnp.mean(zc * zc, axis=-1, keepdims=True)
    return zc * lax.rsqrt(var + LN_EPS) * g + b


def _merge_kernel(ya_ref, yr_ref, ga_ref, gl_ref, x_ref, wa_ref, wl_ref, wo_ref, g_ref, b_ref,
                  o_ref, ob_ref, *, alpha):
    pa = jnp.dot(ya_ref[...], wa_ref[...], preferred_element_type=F32)
    pr = jnp.dot(yr_ref[...], wl_ref[...], preferred_element_type=F32)
    merged = (jax.nn.sigmoid(ga_ref[...].astype(F32)) * pa
              + jax.nn.sigmoid(gl_ref[...].astype(F32)) * pr)
    mix = jnp.dot(merged.astype(BF16), wo_ref[...], preferred_element_type=F32)
    y = _layer_norm(alpha * x_ref[...] + mix, g_ref[...], b_ref[...])
    o_ref[...] = y
    ob_ref[...] = y.T.astype(BF16)


def _merge_out(ya, yr, rest, x, wa, wl, wo, ln_g, ln_b, alpha, tm):
    t, d = x.shape
    kern = functools.partial(_merge_kernel, alpha=alpha)
    rows = lambda c: pl.BlockSpec((tm, d), lambda i: (i, c))
    full = pl.BlockSpec((d, d), lambda i: (0, 0))
    vec = pl.BlockSpec((1, d), lambda i: (0, 0))
    return pl.pallas_call(
        kern,
        grid=(t // tm,),
        in_specs=[rows(0), rows(0), rows(PROJ_GATE_COL), rows(PROJ_GATE_COL + 1), rows(0),
                  full, full, full, vec, vec],
        out_specs=[rows(0), pl.BlockSpec((d, tm), lambda i: (0, i))],
        out_shape=[jax.ShapeDtypeStruct((t, d), F32), jax.ShapeDtypeStruct((d, t), BF16)],
        compiler_params=_params("parallel"),
        name="merge_out_ln",
    )(ya, yr, rest, rest, x, wa, wl, wo, ln_g, ln_b)


def _cmpx(v, i, j):
    hi = jnp.maximum(v[i], v[j])
    v[j] = jnp.minimum(v[i], v[j])
    v[i] = hi


def _bitonic_merge_desc(v):
    n = len(v)
    j = n // 2
    while j >= 1:
        for i in range(n):
            if i & j == 0:
                _cmpx(v, i, i + j)
        j //= 2


def _bitonic_sort_desc(v):
    n = len(v)
    k = 2
    while k <= n:
        j = k // 2
        while j >= 1:
            for i in range(n):
                l = i ^ j
                if l > i:
                    if i & k == 0:
                        _cmpx(v, i, l)
                    else:
                        _cmpx(v, l, i)
            j //= 2
        k *= 2


def _top_merge(a, b):
    n = len(a)
    v = [jnp.maximum(a[i], b[n - 1 - i]) for i in range(n)]
    _bitonic_merge_desc(v)
    return v


def _prefix_count(b, test):
    assert len(b) == 16
    c8 = test(b[7])
    n = jnp.where(c8, 8.0, 0.0)
    c4 = test(jnp.where(c8, b[11], b[3]))
    n = n + jnp.where(c4, 4.0, 0.0)
    c2 = test(jnp.where(c8, jnp.where(c4, b[13], b[9]), jnp.where(c4, b[5], b[1])))
    n = n + jnp.where(c2, 2.0, 0.0)
    lo = jnp.where(c4, jnp.where(c2, b[6], b[4]), jnp.where(c2, b[2], b[0]))
    hi = jnp.where(c4, jnp.where(c2, b[14], b[12]), jnp.where(c2, b[10], b[8]))
    n = n + jnp.where(test(jnp.where(c8, hi, lo)), 1.0, 0.0)
    return jnp.where(test(b[15]), 16.0, n)


_CAND_PAIRS = [(p, q) for p in range(PEER_TOPK) for q in range(PEER_TOPK)
               if (p + 1) * (q + 1) <= PEER_TOPK]


def _select_kernel(x_ref, wq_ref, sk_ref, rank_ref, bexp_ref, cnt_ref, arow_ref,
                   q_scr, s_scr, pk_scr, t_scr, *, tt):
    qt = jnp.dot(wq_ref[...], x_ref[...], preferred_element_type=F32)
    q_scr[...] = qt.astype(BF16)
    dk = sk_ref.shape[3]
    n_grp = N_KEYS // SUBLANES

    def head(h, _):
        for c in range(2):
            r0 = pl.multiple_of((h * 2 + c) * dk, dk)
            sc = jnp.dot(sk_ref[h, c], q_scr[pl.ds(r0, dk), :], preferred_element_type=F32)
            s_scr[c, h] = sc
            v = [sc[g * SUBLANES:(g + 1) * SUBLANES, :] for g in range(n_grp)]
            _bitonic_sort_desc(v)
            for sh in (4, 2, 1):
                v = _top_merge(v, [pltpu.roll(e, sh, 0) for e in v])
            for p in range(PEER_TOPK):
                pk_scr[c, p, pl.ds(h, 1), :] = v[p][0:1, :]
        return 0

    lax.fori_loop(0, PEER_HEADS, head, 0)

    a = [pk_scr[0, p] for p in range(PEER_TOPK)]
    b = [pk_scr[1, p] for p in range(PEER_TOPK)]
    cands = [a[p] + b[q] for p, q in _CAND_PAIRS]
    neg = jnp.full_like(cands[0], -jnp.inf)
    blocks = cands + [neg] * ((-len(cands)) % PEER_TOPK)
    top = None
    for i0 in range(0, len(blocks), PEER_TOPK):
        blk = blocks[i0:i0 + PEER_TOPK]
        _bitonic_sort_desc(blk)
        top = blk if top is None else _top_merge(top, blk)
    tau = top[PEER_TOPK - 1]
    mx = a[0] + b[0]
    z = jnp.zeros_like(tau)
    for cnd in cands:
        z = z + jnp.where(cnd >= tau, jnp.exp(cnd - mx), 0.0)
    t_scr[0] = tau
    t_scr[1] = a[0] + jnp.log(z)

    def head2(h, _):
        s1 = s_scr[0, h]
        s2 = s_scr[1, h]
        tau_h = t_scr[0, pl.ds(h, 1), :]
        bs = [pk_scr[1, q, pl.ds(h, 1), :] for q in range(PEER_TOPK)]
        rank_ref[h] = _prefix_count(bs, lambda thr: thr > s2).astype(BF16)
        cnt_ref[h] = _prefix_count(bs, lambda thr: s1 + thr >= tau_h)
        arow_ref[h] = jnp.exp(s1 - t_scr[1, pl.ds(h, 1), :])
        bexp_ref[h] = jnp.exp(s2 - pk_scr[1, 0, pl.ds(h, 1), :]).astype(BF16)
        return 0

    lax.fori_loop(0, PEER_HEADS, head2, 0)


def _peer_select(xt, wq_t, sk, tt):
    d, t = xt.shape
    nq = wq_t.shape[0]
    kern = functools.partial(_select_kernel, tt=tt)
    big = pl.BlockSpec((PEER_HEADS, N_KEYS, tt), lambda i: (0, 0, i))
    shape = lambda dt: jax.ShapeDtypeStruct((PEER_HEADS, N_KEYS, t), dt)
    return pl.pallas_call(
        kern,
        grid=(t // tt,),
        in_specs=[pl.BlockSpec((d, tt), lambda i: (0, i)),
                  pl.BlockSpec((nq, d), lambda i: (0, 0)),
                  pl.BlockSpec(sk.shape, lambda i: (0, 0, 0, 0))],
        out_specs=[big, big, big, big],
        out_shape=[shape(BF16), shape(BF16), shape(F32), shape(F32)],
        scratch_shapes=[pltpu.VMEM((nq, tt), BF16),
                        pltpu.VMEM((2, PEER_HEADS, N_KEYS, tt), F32),
                        pltpu.VMEM((2, PEER_TOPK, PEER_HEADS, tt), F32),
                        pltpu.VMEM((2, PEER_HEADS, tt), F32)],
        compiler_params=_params("parallel"),
        name="peer_select",
    )(xt, wq_t, sk)


def _peer_kernel(xt_ref, x_ref, u_ref, vt_ref, rank_ref, bexp_ref, cnt_ref, arow_ref,
                 g_ref, b_ref, o_ref, acc_scr, act0, act1, w0, w1, row_scr, *, tt, nblk, eb, alpha):
    e = pl.program_id(1)

    @pl.when(e == 0)
    def _():
        acc_scr[...] = jnp.zeros_like(acc_scr)

    acts = (act0, act1)
    ws = (w0, w1)
    rows_per_blk = eb // N_KEYS
    half = acc_scr.shape[0] // 2

    def mm1(k):
        acts[k % 2][...] = jnp.dot(u_ref[k * eb:(k + 1) * eb, :], xt_ref[...],
                                   preferred_element_type=F32)

    def mm2(p, part):
        rows = slice(part * half, (part + 1) * half)
        acc_scr[rows, :] += jnp.dot(vt_ref[p, rows, :], ws[p % 2][...], preferred_element_type=F32)

    def gates(k):
        act, w = acts[k % 2], ws[(k // 2) % 2]
        for ii in range(rows_per_blk):
            i = k * rows_per_blk + ii
            for h in range(PEER_HEADS):
                for a, ref in enumerate((cnt_ref, arow_ref)):
                    row_scr[a, h] = jnp.broadcast_to(ref[h, i:i + 1, :], (PACKED_ROWS, tt)).astype(BF16)
            for j0 in range(0, N_KEYS, PACKED_ROWS):
                r = ii * N_KEYS + j0
                rw = (k % 2) * eb + r
                gsum = None
                for h in range(PEER_HEADS):
                    sel = rank_ref[h, j0:j0 + PACKED_ROWS, :] < row_scr[0, h]
                    term = jnp.where(sel, row_scr[1, h] * bexp_ref[h, j0:j0 + PACKED_ROWS, :],
                                     jnp.zeros((), BF16))
                    gsum = term if gsum is None else gsum + term
                w[rw:rw + PACKED_ROWS, :] = _gelu_tanh(act[r:r + PACKED_ROWS, :].astype(BF16)) * gsum

    mm1(0)
    for k in range(nblk):
        if k + 1 < nblk:
            mm1(k + 1)
        if k >= 2:
            mm2(k // 2 - 1, k % 2)
        gates(k)
    mm2(nblk // 2 - 1, 0)
    mm2(nblk // 2 - 1, 1)

    @pl.when(e == pl.num_programs(1) - 1)
    def _():
        ffn = acc_scr[...].T
        o_ref[...] = _layer_norm(alpha * x_ref[...] + ffn, g_ref[...], b_ref[...])


def _peer_mix(xt, x, u, vt, rank, bexp, cnt, arow, ln_g, ln_b, alpha, tt, te, eb):
    t, d = x.shape
    n_exp = u.shape[0]
    nblk = te // eb
    kern = functools.partial(_peer_kernel, tt=tt, nblk=nblk, eb=eb, alpha=alpha)
    big = pl.BlockSpec((PEER_HEADS, N_KEYS, tt), lambda i, e: (0, 0, i))
    rows = pl.BlockSpec((PEER_HEADS, te // N_KEYS, tt), lambda i, e: (0, e, i))
    vec = pl.BlockSpec((1, d), lambda i, e: (0, 0))
    return pl.pallas_call(
        kern,
        grid=(t // tt, n_exp // te),
        in_specs=[pl.BlockSpec((d, tt), lambda i, e: (0, i)),
                  pl.BlockSpec((tt, d), lambda i, e: (i, 0)),
                  pl.BlockSpec((te, d), lambda i, e: (e, 0)),
                  pl.BlockSpec((nblk // 2, d, 2 * eb), lambda i, e: (e, 0, 0)),
                  big, big, rows, rows, vec, vec],
        out_specs=pl.BlockSpec((tt, d), lambda i, e: (i, 0)),
        out_shape=jax.ShapeDtypeStruct((t, d), F32),
        scratch_shapes=[pltpu.VMEM((d, tt), F32),
                        pltpu.VMEM((eb, tt), F32), pltpu.VMEM((eb, tt), F32),
                        pltpu.VMEM((2 * eb, tt), BF16), pltpu.VMEM((2 * eb, tt), BF16),
                        pltpu.VMEM((2, PEER_HEADS, PACKED_ROWS, tt), BF16)],
        compiler_params=_params("parallel", "arbitrary"),
        name="peer_mix_ln",
    )(xt, x, u, vt, rank, bexp, cnt, arow, ln_g, ln_b)


def _tiles(b, s, t):
    return dict(mm_tm=min(2048, t), mm_tn=512, attn_tb=min(256, s), attn_hp=8,
                lru_ts=min(256, s), merge_tm=min(256, t), sel_tt=min(512, t),
                mix_tt=min(512, t), mix_te=2048, mix_eb=256)


def kernel(x, w_in, lambda_qk, subln_g, conv_w, conv_b, gate_a_w, gate_a_b, gate_x_w, gate_x_b,
           lru_lambda, w_br_attn, w_br_lru, w_out, ln1_g, ln1_b, peer_wq, peer_subkeys, peer_u,
           peer_v, ln2_g, ln2_b):
    b, s, d = x.shape
    t = b * s
    depth = w_in.shape[0]
    alpha = (2.0 * depth) ** 0.25
    attn_w = ATTN_HEADS * HEAD_W
    ts = _tiles(b, s, t)
    slopes = jnp.exp2(-8.0 * jnp.arange(1, ATTN_HEADS + 1, dtype=F32) / ATTN_HEADS)
    row = lambda a: a.reshape(1, -1)
    eb = ts["mix_eb"]

    xf = x.reshape(t, d)
    for l in range(depth):
        lam_init = 0.8 - 0.6 * math.exp(-0.3 * l)
        w_l = w_in[l].astype(BF16)
        w_cat = jnp.concatenate([w_l[:, :2 * attn_w], w_l[:, 3 * attn_w:]], axis=1)
        proj = _matmul(xf, w_cat, BF16, ts["mm_tm"], ts["mm_tn"])
        vt = _value_proj_t(xf, w_l[:, 2 * attn_w:3 * attn_w].T, ts["attn_tb"])
        proj3 = proj.reshape(b, s, -1)
        ya = _diff_attention(proj3, vt, slopes, lambda_qk[l], row(subln_g[l]),
                             lam_init, ts["attn_tb"], ts["attn_hp"])
        yr = _rg_lru(proj3, conv_w[l], row(conv_b[l]), gate_a_w[l].astype(BF16),
                     row(gate_a_b[l]), gate_x_w[l].astype(BF16), row(gate_x_b[l]),
                     row(lru_lambda[l]), ts["lru_ts"])
        x1, x1t = _merge_out(ya.reshape(t, attn_w), yr.reshape(t, d), proj, xf,
                             w_br_attn[l].astype(BF16), w_br_lru[l].astype(BF16),
                             w_out[l].astype(BF16), row(ln1_g[l]), row(ln1_b[l]), alpha,
                             ts["merge_tm"])
        rank, bexp, cnt, arow = _peer_select(x1t, peer_wq[l].T.astype(BF16),
                                             peer_subkeys[l].astype(BF16), ts["sel_tt"])
        v_blocks = peer_v[l].astype(BF16).reshape(-1, 2 * eb, d).transpose(0, 2, 1)
        xf = _peer_mix(x1t, x1, peer_u[l].astype(BF16), v_blocks, rank, bexp, cnt, arow,
                       row(ln2_g[l]), row(ln2_b[l]), alpha, ts["mix_tt"], ts["mix_te"], eb)
    return xf.reshape(b, s, d)
```

```python
import functools
import math

import jax
import jax.numpy as jnp
from jax import lax
from jax.experimental import pallas as pl
from jax.experimental.pallas import tpu as pltpu

F32 = jnp.float32
BF16 = jnp.bfloat16

ATTN_HEADS = 8
HEAD_DIM = 64
HEAD_W = 2 * HEAD_DIM
LRU_BLOCKS = 8
LRU_C = 8.0
CONV_W = 4
PEER_HEADS = 8
N_KEYS = 128
PEER_TOPK = 16
LN_EPS = 1e-5
RMS_EPS = 1e-6
PROJ_LRU_COL = 2
PROJ_GATE_COL = 4
SUBLANES = 8
PACKED_ROWS = 16
V_AUG_ROWS = PACKED_ROWS
GATE_ROWS = PACKED_ROWS
VMEM_LIMIT = 56 * 1024 * 1024

_NT = (((1,), (1,)), ((), ()))


def _params(*sem):
    return pltpu.CompilerParams(dimension_semantics=sem, vmem_limit_bytes=VMEM_LIMIT)


MM_ROWS = 256


def _mm_kernel(x_ref, w_ref, o_ref):
    for r in range(0, x_ref.shape[0], MM_ROWS):
        o_ref[r:r + MM_ROWS, :] = jnp.dot(x_ref[r:r + MM_ROWS, :].astype(BF16), w_ref[...],
                                          preferred_element_type=F32).astype(o_ref.dtype)


def _matmul(x, w, out_dtype, tm, tn):
    t, k = x.shape
    n = w.shape[1]
    return pl.pallas_call(
        _mm_kernel,
        grid=(t // tm, n // tn),
        in_specs=[pl.BlockSpec((tm, k), lambda i, j: (i, 0)),
                  pl.BlockSpec((k, tn), lambda i, j: (0, j))],
        out_specs=pl.BlockSpec((tm, tn), lambda i, j: (i, j)),
        out_shape=jax.ShapeDtypeStruct((t, n), out_dtype),
        compiler_params=_params("parallel", "arbitrary"),
        name="in_proj",
    )(x, w)


def _vt_kernel(x_ref, w_ref, o_ref):
    vt = lax.dot_general(w_ref[...], x_ref[...].astype(BF16), _NT,
                         preferred_element_type=F32)
    tk = vt.shape[1]
    vt = vt.astype(o_ref.dtype).reshape(ATTN_HEADS, HEAD_W, tk)
    ones_row = lax.broadcasted_iota(jnp.int32, (ATTN_HEADS, V_AUG_ROWS, tk), 1) == 0
    extra = jnp.where(ones_row, 1.0, 0.0).astype(o_ref.dtype)
    o_ref[...] = jnp.concatenate([vt, extra], axis=1).reshape(o_ref.shape)


def _value_proj_t(x, w_t, tk):
    t, d = x.shape
    rows = HEAD_W + V_AUG_ROWS
    return pl.pallas_call(
        _vt_kernel,
        grid=(t // tk,),
        in_specs=[pl.BlockSpec((tk, d), lambda i: (i, 0)),
                  pl.BlockSpec(w_t.shape, lambda i: (0, 0))],
        out_specs=pl.BlockSpec((ATTN_HEADS, 1, rows, tk), lambda i: (0, i, 0, 0)),
        out_shape=jax.ShapeDtypeStruct((ATTN_HEADS, t // tk, rows, tk), BF16),
        compiler_params=_params("parallel"),
        name="value_proj_t",
    )(x, w_t)


def _attn_kernel(slopes_ref, lqk_ref, g_ref, q_ref, k_ref, vt_ref, o_ref, *scr, tb, hp, lam_init):
    acc_scr, st_scr, ka_scr = scr[:hp], scr[hp:2 * hp], scr[2 * hp:]
    hg = pl.program_id(1)
    qi = pl.program_id(2)
    s = k_ref.shape[1]
    lq = lqk_ref[...]
    lam = (jnp.exp(jnp.sum(lq[0:1] * lq[1:2], axis=1, keepdims=True))
           - jnp.exp(jnp.sum(lq[2:3] * lq[3:4], axis=1, keepdims=True)) + lam_init)
    slopes = [slopes_ref[hg * hp + hh] for hh in range(hp)]

    @pl.when(qi == 0)
    def _():
        lane = lax.broadcasted_iota(jnp.int32, (s, HEAD_W), 1)
        c = (lax.broadcasted_iota(jnp.int32, (s, HEAD_W), 0) & (tb - 1)).astype(F32)
        for hh in range(hp):
            extra = jnp.where(lane == 0, 1.0, jnp.where(lane == 1, slopes[hh] * c, 0.0))
            ka_scr[hh][:, :HEAD_W] = k_ref[0, :, hh * HEAD_W:(hh + 1) * HEAD_W]
            ka_scr[hh][:, HEAD_W:] = extra.astype(BF16)

    lane = lax.broadcasted_iota(jnp.int32, (tb, HEAD_W), 1)
    lane2 = lax.broadcasted_iota(jnp.int32, (2 * tb, HEAD_W), 1)
    r = (lax.broadcasted_iota(jnp.int32, (2 * tb, HEAD_W), 0) & (tb - 1)).astype(F32)
    masked = ((lax.broadcasted_iota(jnp.int32, (tb, 2 * tb), 1) & (tb - 1))
              < lax.broadcasted_iota(jnp.int32, (tb, 2 * tb), 0))
    qqs = []
    for hh in range(hp):
        qs = q_ref[0, :, hh * HEAD_W:(hh + 1) * HEAD_W].astype(F32) * (HEAD_DIM ** -0.5)
        zero = jnp.zeros_like(qs)
        qq = jnp.concatenate([jnp.where(lane < HEAD_DIM, qs, zero),
                              jnp.where(lane >= HEAD_DIM, qs, zero)], axis=0)
        extra = jnp.where(lane2 == 0, -slopes[hh] * r, jnp.where(lane2 == 1, 1.0, 0.0))
        qqs.append(jnp.concatenate([qq, extra], axis=1).T.astype(BF16))
        acc_scr[hh][...] = jnp.zeros_like(acc_scr[hh])

    def scores(j, hh):
        start = pl.multiple_of(j * tb, tb)
        st_scr[hh][...] = jnp.dot(ka_scr[hh][pl.ds(start, tb), :], qqs[hh],
                                  preferred_element_type=F32)

    def step(j, ms, last):
        shift = (qi - j) * tb
        out = []
        for hh in range(hp):
            off = slopes[hh] * shift.astype(F32)
            st = st_scr[hh][...]
            if last:
                st = jnp.where(masked, -jnp.inf, st)
            m_new = jnp.maximum(ms[hh], jnp.max(st, axis=0, keepdims=True) - off)
            pt = jnp.exp(st - (m_new + off)).astype(BF16)
            a = jnp.exp(ms[hh] - m_new)
            if not last:
                scores(j + 1, hh)
            acc_scr[hh][...] = a * acc_scr[hh][...] + jnp.dot(vt_ref[hh, j], pt,
                                                              preferred_element_type=F32)
            out.append(m_new)
        return tuple(out)

    for hh in range(hp):
        scores(0, hh)
    ms = tuple(jnp.full((1, 2 * tb), -jnp.inf, F32) for _ in range(hp))
    ms = lax.fori_loop(0, qi, functools.partial(step, last=False), ms)
    step(qi, ms, last=True)

    for hh in range(hp):
        acc = acc_scr[hh][...]
        ot = acc[:HEAD_W] * (1.0 / acc[HEAD_W:HEAD_W + 1])
        o = (ot[:, :tb] - lam * ot[:, tb:]).T
        o = o * lax.rsqrt(jnp.mean(o * o, axis=-1, keepdims=True) + RMS_EPS)
        o = o * g_ref[...] * (1.0 - lam_init)
        o_ref[0, :, hh * HEAD_W:(hh + 1) * HEAD_W] = o.astype(o_ref.dtype)


def _diff_attention(qk, vt, slopes, lambda_qk, subln_g, lam_init, tb, hp):
    b, s, _ = qk.shape
    assert tb & (tb - 1) == 0 and tb <= 256
    kern = functools.partial(_attn_kernel, tb=tb, hp=hp, lam_init=lam_init)
    gw = hp * HEAD_W
    vrows = vt.shape[2]
    return pl.pallas_call(
        kern,
        grid=(b, ATTN_HEADS // hp, s // tb),
        in_specs=[pl.BlockSpec(memory_space=pltpu.SMEM),
                  pl.BlockSpec((4, HEAD_DIM), lambda b_, h, i: (0, 0)),
                  pl.BlockSpec((1, HEAD_W), lambda b_, h, i: (0, 0)),
                  pl.BlockSpec((1, tb, gw), lambda b_, h, i: (b_, i, h)),
                  pl.BlockSpec((1, s, gw), lambda b_, h, i: (b_, 0, ATTN_HEADS // hp + h),
                               pipeline_mode=pl.Buffered(1)),
                  pl.BlockSpec((hp, s // tb, vrows, tb), lambda b_, h, i: (h, b_, 0, 0),
                               pipeline_mode=pl.Buffered(1))],
        out_specs=pl.BlockSpec((1, tb, gw), lambda b_, h, i: (b_, i, h)),
        out_shape=jax.ShapeDtypeStruct((b, s, ATTN_HEADS * HEAD_W), BF16),
        scratch_shapes=([pltpu.VMEM((vrows, 2 * tb), F32)] * hp
                        + [pltpu.VMEM((tb, 2 * tb), F32)] * hp
                        + [pltpu.VMEM((s, 2 * HEAD_W), BF16)] * hp),
        compiler_params=_params("parallel", "parallel", "arbitrary"),
        name="diff_attn",
    )(slopes, lambda_qk, subln_g, qk, qk, vt)


def _gelu(x):
    return jax.nn.gelu(x)


def _gelu_tanh(x):
    c = math.sqrt(2.0 / math.pi)
    hx = 0.5 * x
    return hx + hx * jnp.tanh(x * (c + (c * 0.044715) * (x * x)))


def _lru_kernel(xr_ref, gr_ref, cw_ref, cb_ref, gaw_ref, gab_ref, gxw_ref, gxb_ref, lam_ref,
                o_ref, tail_scr, h_scr, a_scr, u_scr, gg_scr, *, ts):
    si = pl.program_id(1)

    @pl.when(si == 0)
    def _():
        tail_scr[...] = jnp.zeros_like(tail_scr)
        h_scr[...] = jnp.zeros_like(h_scr)

    x = xr_ref[0].astype(F32)
    w = x.shape[1]
    tail = tail_scr[...]
    row8 = lax.broadcasted_iota(jnp.int32, (SUBLANES, w), 0)
    cw = cw_ref[...]
    xb = cb_ref[...]
    for d in range(CONV_W - 1, 0, -1):
        xs = pltpu.roll(x, d, 0)
        top = jnp.where(row8 < d, pltpu.roll(tail, d, 0), xs[:SUBLANES])
        xs = jnp.concatenate([top, xs[SUBLANES:]], axis=0)
        xb = xb + xs * cw[CONV_W - 1 - d:CONV_W - d]
    xb = xb + x * cw[CONV_W - 1:CONV_W]
    tail_scr[...] = x[ts - SUBLANES:]

    xbb = xb.astype(BF16)
    bw = w // LRU_BLOCKS

    def gate(w_ref, b_ref):
        parts = [jnp.dot(xbb[:, g * bw:(g + 1) * bw], w_ref[g], preferred_element_type=F32)
                 for g in range(LRU_BLOCKS)]
        return jax.nn.sigmoid(jnp.concatenate(parts, axis=1) + b_ref[...])

    r = gate(gaw_ref, gab_ref)
    i = gate(gxw_ref, gxb_ref)
    lam = lam_ref[...]
    softplus_neg = jnp.maximum(-lam, 0.0) + jnp.log1p(jnp.exp(-jnp.abs(lam)))
    log_a = -LRU_C * r * softplus_neg
    a = jnp.exp(log_a)
    mult = jnp.sqrt(1.0 - jnp.exp(2.0 * log_a))
    first = jnp.where(jnp.logical_and(row8 == 0, si == 0), 1.0, mult[:SUBLANES])
    mult = jnp.concatenate([first, mult[SUBLANES:]], axis=0)
    a_scr[...] = a
    u_scr[...] = mult * (i * xb)
    gg_scr[...] = _gelu(gr_ref[0].astype(F32))

    def group(g, hprev):
        r0 = pl.multiple_of(g * SUBLANES, SUBLANES)
        ag = a_scr[pl.ds(r0, SUBLANES), :]
        ug = u_scr[pl.ds(r0, SUBLANES), :]
        for d in (1, 2, 4):
            keep = row8 >= d
            ug = jnp.where(keep, ag * pltpu.roll(ug, d, 0) + ug, ug)
            ag = jnp.where(keep, ag * pltpu.roll(ag, d, 0), ag)
        hg = ag * hprev + ug
        o_ref[0, pl.ds(r0, SUBLANES), :] = (hg * gg_scr[pl.ds(r0, SUBLANES), :]).astype(o_ref.dtype)
        return jnp.broadcast_to(hg[SUBLANES - 1:SUBLANES, :], (SUBLANES, w))

    h_scr[...] = lax.fori_loop(0, ts // SUBLANES, group, h_scr[...])


def _rg_lru(rest, conv_w, conv_b, gaw, gab, gxw, gxb, lru_lambda, ts):
    b, s, _ = rest.shape
    w = conv_w.shape[1]
    kern = functools.partial(_lru_kernel, ts=ts)
    vec = pl.BlockSpec((1, w), lambda b_, i: (0, 0))
    blk = pl.BlockSpec(gaw.shape, lambda b_, i: (0, 0, 0))
    return pl.pallas_call(
        kern,
        grid=(b, s // ts),
        in_specs=[pl.BlockSpec((1, ts, w), lambda b_, i: (b_, i, PROJ_LRU_COL)),
                  pl.BlockSpec((1, ts, w), lambda b_, i: (b_, i, PROJ_LRU_COL + 1)),
                  pl.BlockSpec((CONV_W, w), lambda b_, i: (0, 0)),
                  vec, blk, vec, blk, vec, vec],
        out_specs=pl.BlockSpec((1, ts, w), lambda b_, i: (b_, i, 0)),
        out_shape=jax.ShapeDtypeStruct((b, s, w), BF16),
        scratch_shapes=[pltpu.VMEM((SUBLANES, w), F32), pltpu.VMEM((SUBLANES, w), F32),
                        pltpu.VMEM((ts, w), F32), pltpu.VMEM((ts, w), F32),
                        pltpu.VMEM((ts, w), F32)],
        compiler_params=_params("parallel", "arbitrary"),
        name="rg_lru",
    )(rest, rest, conv_w, conv_b, gaw, gab, gxw, gxb, lru_lambda)


def _layer_norm(z, g, b):
    mu = jnp.mean(z, axis=-1, keepdims=True)
    zc = z - mu
    var = jnp.mean(zc * zc, axis=-1, keepdims=True)
    return zc * lax.rsqrt(var + LN_EPS) * g + b


def _merge_kernel(ya_ref, yr_ref, ga_ref, gl_ref, x_ref, wa_ref, wl_ref, wo_ref, g_ref, b_ref,
                  o_ref, ob_ref, *, alpha):
    pa = jnp.dot(ya_ref[...], wa_ref[...], preferred_element_type=F32)
    pr = jnp.dot(yr_ref[...], wl_ref[...], preferred_element_type=F32)
    merged = (jax.nn.sigmoid(ga_ref[...].astype(F32)) * pa
              + jax.nn.sigmoid(gl_ref[...].astype(F32)) * pr)
    mix = jnp.dot(merged.astype(BF16), wo_ref[...], preferred_element_type=F32)
    y = _layer_norm(alpha * x_ref[...] + mix, g_ref[...], b_ref[...])
    o_ref[...] = y
    ob_ref[...] = y.T.astype(BF16)


def _merge_out(ya, yr, rest, x, wa, wl, wo, ln_g, ln_b, alpha, tm):
    t, d = x.shape
    kern = functools.partial(_merge_kernel, alpha=alpha)
    rows = lambda c: pl.BlockSpec((tm, d), lambda i: (i, c))
    full = pl.BlockSpec((d, d), lambda i: (0, 0))
    vec = pl.BlockSpec((1, d), lambda i: (0, 0))
    return pl.pallas_call(
        kern,
        grid=(t // tm,),
        in_specs=[rows(0), rows(0), rows(PROJ_GATE_COL), rows(PROJ_GATE_COL + 1), rows(0),
                  full, full, full, vec, vec],
        out_specs=[rows(0), pl.BlockSpec((d, tm), lambda i: (0, i))],
        out_shape=[jax.ShapeDtypeStruct((t, d), F32), jax.ShapeDtypeStruct((d, t), BF16)],
        compiler_params=_params("parallel"),
        name="merge_out_ln",
    )(ya, yr, rest, rest, x, wa, wl, wo, ln_g, ln_b)


def _cmpx(v, i, j):
    hi = jnp.maximum(v[i], v[j])
    v[j] = jnp.minimum(v[i], v[j])
    v[i] = hi


def _bitonic_merge_desc(v):
    n = len(v)
    j = n // 2
    while j >= 1:
        for i in range(n):
            if i & j == 0:
                _cmpx(v, i, i + j)
        j //= 2


def _bitonic_sort_desc(v):
    n = len(v)
    k = 2
    while k <= n:
        j = k // 2
        while j >= 1:
            for i in range(n):
                l = i ^ j
                if l > i:
                    if i & k == 0:
                        _cmpx(v, i, l)
                    else:
                        _cmpx(v, l, i)
            j //= 2
        k *= 2


def _top_merge(a, b):
    n = len(a)
    v = [jnp.maximum(a[i], b[n - 1 - i]) for i in range(n)]
    _bitonic_merge_desc(v)
    return v


def _prefix_count(b, test):
    assert len(b) == 16
    c8 = test(b[7])
    n = jnp.where(c8, 8.0, 0.0)
    c4 = test(jnp.where(c8, b[11], b[3]))
    n = n + jnp.where(c4, 4.0, 0.0)
    c2 = test(jnp.where(c8, jnp.where(c4, b[13], b[9]), jnp.where(c4, b[5], b[1])))
    n = n + jnp.where(c2, 2.0, 0.0)
    lo = jnp.where(c4, jnp.where(c2, b[6], b[4]), jnp.where(c2, b[2], b[0]))
    hi = jnp.where(c4, jnp.where(c2, b[14], b[12]), jnp.where(c2, b[10], b[8]))
    n = n + jnp.where(test(jnp.where(c8, hi, lo)), 1.0, 0.0)
    return jnp.where(test(b[15]), 16.0, n)


_CAND_PAIRS = [(p, q) for p in range(PEER_TOPK) for q in range(PEER_TOPK)
               if (p + 1) * (q + 1) <= PEER_TOPK]


def _select_kernel(x_ref, wq_ref, sk_ref, rank_ref, bexp_ref, cnt_ref, arow_ref,
                   q_scr, s_scr, pk_scr, t_scr, *, tt):
    qt = jnp.dot(wq_ref[...], x_ref[...], preferred_element_type=F32)
    q_scr[...] = qt.astype(BF16)
    dk = sk_ref.shape[3]
    n_grp = N_KEYS // SUBLANES

    def head(h, _):
        for c in range(2):
            r0 = pl.multiple_of((h * 2 + c) * dk, dk)
            sc = jnp.dot(sk_ref[h, c], q_scr[pl.ds(r0, dk), :], preferred_element_type=F32)
            s_scr[c, h] = sc
            v = [sc[g * SUBLANES:(g + 1) * SUBLANES, :] for g in range(n_grp)]
            _bitonic_sort_desc(v)
            for sh in (4, 2, 1):
                v = _top_merge(v, [pltpu.roll(e, sh, 0) for e in v])
            for p in range(PEER_TOPK):
                pk_scr[c, p, pl.ds(h, 1), :] = v[p][0:1, :]
        return 0

    lax.fori_loop(0, PEER_HEADS, head, 0)

    a = [pk_scr[0, p] for p in range(PEER_TOPK)]
    b = [pk_scr[1, p] for p in range(PEER_TOPK)]
    cands = [a[p] + b[q] for p, q in _CAND_PAIRS]
    neg = jnp.full_like(cands[0], -jnp.inf)
    blocks = cands + [neg] * ((-len(cands)) % PEER_TOPK)
    top = None
    for i0 in range(0, len(blocks), PEER_TOPK):
        blk = blocks[i0:i0 + PEER_TOPK]
        _bitonic_sort_desc(blk)
        top = blk if top is None else _top_merge(top, blk)
    tau = top[PEER_TOPK - 1]
    mx = a[0] + b[0]
    z = jnp.zeros_like(tau)
    for cnd in cands:
        z = z + jnp.where(cnd >= tau, jnp.exp(cnd - mx), 0.0)
    t_scr[0] = tau
    t_scr[1] = a[0] + jnp.log(z)

    def head2(h, _):
        s1 = s_scr[0, h]
        s2 = s_scr[1, h]
        tau_h = t_scr[0, pl.ds(h, 1), :]
        bs = [pk_scr[1, q, pl.ds(h, 1), :] for q in range(PEER_TOPK)]
        rank_ref[h] = _prefix_count(bs, lambda thr: thr > s2).astype(BF16)
        cnt_ref[h] = _prefix_count(bs, lambda thr: s1 + thr >= tau_h)
        arow_ref[h] = jnp.exp(s1 - t_scr[1, pl.ds(h, 1), :])
        bexp_ref[h] = jnp.exp(s2 - pk_scr[1, 0, pl.ds(h, 1), :]).astype(BF16)
        return 0

    lax.fori_loop(0, PEER_HEADS, head2, 0)


def _peer_select(xt, wq_t, sk, tt):
    d, t = xt.shape
    nq = wq_t.shape[0]
    kern = functools.partial(_select_kernel, tt=tt)
    big = pl.BlockSpec((PEER_HEADS, N_KEYS, tt), lambda i: (0, 0, i))
    shape = lambda dt: jax.ShapeDtypeStruct((PEER_HEADS, N_KEYS, t), dt)
    return pl.pallas_call(
        kern,
        grid=(t // tt,),
        in_specs=[pl.BlockSpec((d, tt), lambda i: (0, i)),
                  pl.BlockSpec((nq, d), lambda i: (0, 0)),
                  pl.BlockSpec(sk.shape, lambda i: (0, 0, 0, 0))],
        out_specs=[big, big, big, big],
        out_shape=[shape(BF16), shape(BF16), shape(F32), shape(F32)],
        scratch_shapes=[pltpu.VMEM((nq, tt), BF16),
                        pltpu.VMEM((2, PEER_HEADS, N_KEYS, tt), F32),
                        pltpu.VMEM((2, PEER_TOPK, PEER_HEADS, tt), F32),
                        pltpu.VMEM((2, PEER_HEADS, tt), F32)],
        compiler_params=_params("parallel"),
        name="peer_select",
    )(xt, wq_t, sk)


def _peer_kernel(xt_ref, x_ref, u_ref, vt_ref, rank_ref, bexp_ref, cnt_ref, arow_ref,
                 g_ref, b_ref, o_ref, acc_scr, act0, act1, w0, w1, row_scr, *, tt, nblk, eb, alpha):
    e = pl.program_id(1)

    @pl.when(e == 0)
    def _():
        acc_scr[...] = jnp.zeros_like(acc_scr)

    acts = (act0, act1)
    ws = (w0, w1)
    rows_per_blk = eb // N_KEYS
    half = acc_scr.shape[0] // 2

    def mm1(k):
        acts[k % 2][...] = jnp.dot(u_ref[k * eb:(k + 1) * eb, :], xt_ref[...],
                                   preferred_element_type=F32)

    def mm2(p, part):
        rows = slice(part * half, (part + 1) * half)
        acc_scr[rows, :] += jnp.dot(vt_ref[p, rows, :], ws[p % 2][...], preferred_element_type=F32)

    def gates(k):
        act, w = acts[k % 2], ws[(k // 2) % 2]
        for ii in range(rows_per_blk):
            i = k * rows_per_blk + ii
            for h in range(PEER_HEADS):
                for a, ref in enumerate((cnt_ref, arow_ref)):
                    row_scr[a, h] = jnp.broadcast_to(ref[h, i:i + 1, :], (GATE_ROWS, tt)).astype(BF16)
            for j0 in range(0, N_KEYS, GATE_ROWS):
                r = ii * N_KEYS + j0
                rw = (k % 2) * eb + r
                gsum = None
                for h in range(PEER_HEADS):
                    sel = rank_ref[h, j0:j0 + GATE_ROWS, :] < row_scr[0, h]
                    term = jnp.where(sel, row_scr[1, h] * bexp_ref[h, j0:j0 + GATE_ROWS, :],
                                     jnp.zeros((), BF16))
                    gsum = term if gsum is None else gsum + term
                w[rw:rw + GATE_ROWS, :] = _gelu_tanh(act[r:r + GATE_ROWS, :].astype(BF16)) * gsum

    mm1(0)
    for k in range(nblk):
        if k + 1 < nblk:
            mm1(k + 1)
        if k >= 2:
            mm2(k // 2 - 1, k % 2)
        gates(k)
    mm2(nblk // 2 - 1, 0)
    mm2(nblk // 2 - 1, 1)

    @pl.when(e == pl.num_programs(1) - 1)
    def _():
        ffn = acc_scr[...].T
        o_ref[...] = _layer_norm(alpha * x_ref[...] + ffn, g_ref[...], b_ref[...])


def _peer_mix(xt, x, u, vt, rank, bexp, cnt, arow, ln_g, ln_b, alpha, tt, te, eb):
    t, d = x.shape
    n_exp = u.shape[0]
    nblk = te // eb
    kern = functools.partial(_peer_kernel, tt=tt, nblk=nblk, eb=eb, alpha=alpha)
    big = pl.BlockSpec((PEER_HEADS, N_KEYS, tt), lambda i, e: (0, 0, i))
    rows = pl.BlockSpec((PEER_HEADS, te // N_KEYS, tt), lambda i, e: (0, e, i))
    vec = pl.BlockSpec((1, d), lambda i, e: (0, 0))
    return pl.pallas_call(
        kern,
        grid=(t // tt, n_exp // te),
        in_specs=[pl.BlockSpec((d, tt), lambda i, e: (0, i)),
                  pl.BlockSpec((tt, d), lambda i, e: (i, 0)),
                  pl.BlockSpec((te, d), lambda i, e: (e, 0)),
                  pl.BlockSpec((nblk // 2, d, 2 * eb), lambda i, e: (e, 0, 0)),
                  big, big, rows, rows, vec, vec],
        out_specs=pl.BlockSpec((tt, d), lambda i, e: (i, 0)),
        out_shape=jax.ShapeDtypeStruct((t, d), F32),
        scratch_shapes=[pltpu.VMEM((d, tt), F32),
                        pltpu.VMEM((eb, tt), F32), pltpu.VMEM((eb, tt), F32),
                        pltpu.VMEM((2 * eb, tt), BF16), pltpu.VMEM((2 * eb, tt), BF16),
                        pltpu.VMEM((2, PEER_HEADS, GATE_ROWS, tt), BF16)],
        compiler_params=_params("parallel", "arbitrary"),
        name="peer_mix_ln",
    )(xt, x, u, vt, rank, bexp, cnt, arow, ln_g, ln_b)


def _tiles(b, s, t):
    return dict(mm_tm=min(2048, t), mm_tn=512, attn_tb=min(256, s), attn_hp=8,
                lru_ts=min(256, s), merge_tm=min(256, t), sel_tt=min(512, t),
                mix_tt=min(512, t), mix_te=2048, mix_eb=256)


def kernel(x, w_in, lambda_qk, subln_g, conv_w, conv_b, gate_a_w, gate_a_b, gate_x_w, gate_x_b,
           lru_lambda, w_br_attn, w_br_lru, w_out, ln1_g, ln1_b, peer_wq, peer_subkeys, peer_u,
           peer_v, ln2_g, ln2_b):
    b, s, d = x.shape
    t = b * s
    depth = w_in.shape[0]
    alpha = (2.0 * depth) ** 0.25
    attn_w = ATTN_HEADS * HEAD_W
    ts = _tiles(b, s, t)
    slopes = jnp.exp2(-8.0 * jnp.arange(1, ATTN_HEADS + 1, dtype=F32) / ATTN_HEADS)
    row = lambda a: a.reshape(1, -1)
    eb = ts["mix_eb"]

    xf = x.reshape(t, d)
    for l in range(depth):
        lam_init = 0.8 - 0.6 * math.exp(-0.3 * l)
        w_l = w_in[l].astype(BF16)
        w_cat = jnp.concatenate([w_l[:, :2 * attn_w], w_l[:, 3 * attn_w:]], axis=1)
        proj = _matmul(xf, w_cat, BF16, ts["mm_tm"], ts["mm_tn"])
        vt = _value_proj_t(xf, w_l[:, 2 * attn_w:3 * attn_w].T, ts["attn_tb"])
        proj3 = proj.reshape(b, s, -1)
        ya = _diff_attention(proj3, vt, slopes, lambda_qk[l], row(subln_g[l]),
                             lam_init, ts["attn_tb"], ts["attn_hp"])
        yr = _rg_lru(proj3, conv_w[l], row(conv_b[l]), gate_a_w[l].astype(BF16),
                     row(gate_a_b[l]), gate_x_w[l].astype(BF16), row(gate_x_b[l]),
                     row(lru_lambda[l]), ts["lru_ts"])
        x1, x1t = _merge_out(ya.reshape(t, attn_w), yr.reshape(t, d), proj, xf,
                             w_br_attn[l].astype(BF16), w_br_lru[l].astype(BF16),
                             w_out[l].astype(BF16), row(ln1_g[l]), row(ln1_b[l]), alpha,
                             ts["merge_tm"])
        rank, bexp, cnt, arow = _peer_select(x1t, peer_wq[l].T.astype(BF16),
                                             peer_subkeys[l].astype(BF16), ts["sel_tt"])
        v_blocks = peer_v[l].astype(BF16).reshape(-1, 2 * eb, d).transpose(0, 2, 1)
        xf = _peer_mix(x1t, x1, peer_u[l].astype(BF16), v_blocks, rank, bexp, cnt, arow,
                       row(ln2_g[l]), row(ln2_b[l]), alpha, ts["mix_tt"], ts["mix_te"], eb)
    return xf.reshape(b, s, d)
```

```python
import functools
import math

import jax
import jax.numpy as jnp
from jax import lax
from jax.experimental import pallas as pl
from jax.experimental.pallas import tpu as pltpu

F32 = jnp.float32
BF16 = jnp.bfloat16

ATTN_HEADS = 8
HEAD_DIM = 64
HEAD_W = 2 * HEAD_DIM
LRU_BLOCKS = 8
LRU_C = 8.0
CONV_W = 4
PEER_HEADS = 8
N_KEYS = 128
PEER_TOPK = 16
LN_EPS = 1e-5
RMS_EPS = 1e-6
PROJ_LRU_COL = 2
PROJ_GATE_COL = 4
SUBLANES = 8
PACKED_ROWS = 16
V_AUG_ROWS = PACKED_ROWS
GATE_ROWS = PACKED_ROWS
MM1_AHEAD = 2
N_ACT_BUFS = MM1_AHEAD + 2
N_W_BUFS = 3
VMEM_LIMIT = 56 * 1024 * 1024

_NT = (((1,), (1,)), ((), ()))


def _params(*sem):
    return pltpu.CompilerParams(dimension_semantics=sem, vmem_limit_bytes=VMEM_LIMIT)


MM_ROWS = 256


def _mm_kernel(x_ref, w_ref, o_ref):
    for r in range(0, x_ref.shape[0], MM_ROWS):
        o_ref[r:r + MM_ROWS, :] = jnp.dot(x_ref[r:r + MM_ROWS, :].astype(BF16), w_ref[...],
                                          preferred_element_type=F32).astype(o_ref.dtype)


def _matmul(x, w, out_dtype, tm, tn):
    t, k = x.shape
    n = w.shape[1]
    return pl.pallas_call(
        _mm_kernel,
        grid=(t // tm, n // tn),
        in_specs=[pl.BlockSpec((tm, k), lambda i, j: (i, 0)),
                  pl.BlockSpec((k, tn), lambda i, j: (0, j))],
        out_specs=pl.BlockSpec((tm, tn), lambda i, j: (i, j)),
        out_shape=jax.ShapeDtypeStruct((t, n), out_dtype),
        compiler_params=_params("parallel", "arbitrary"),
        name="in_proj",
    )(x, w)


def _vt_kernel(x_ref, w_ref, o_ref):
    vt = lax.dot_general(w_ref[...], x_ref[...].astype(BF16), _NT,
                         preferred_element_type=F32)
    tk = vt.shape[1]
    vt = vt.astype(o_ref.dtype).reshape(ATTN_HEADS, HEAD_W, tk)
    ones_row = lax.broadcasted_iota(jnp.int32, (ATTN_HEADS, V_AUG_ROWS, tk), 1) == 0
    extra = jnp.where(ones_row, 1.0, 0.0).astype(o_ref.dtype)
    o_ref[...] = jnp.concatenate([vt, extra], axis=1).reshape(o_ref.shape)


def _value_proj_t(x, w_t, tk):
    t, d = x.shape
    rows = HEAD_W + V_AUG_ROWS
    return pl.pallas_call(
        _vt_kernel,
        grid=(t // tk,),
        in_specs=[pl.BlockSpec((tk, d), lambda i: (i, 0)),
                  pl.BlockSpec(w_t.shape, lambda i: (0, 0))],
        out_specs=pl.BlockSpec((ATTN_HEADS, 1, rows, tk), lambda i: (0, i, 0, 0)),
        out_shape=jax.ShapeDtypeStruct((ATTN_HEADS, t // tk, rows, tk), BF16),
        compiler_params=_params("parallel"),
        name="value_proj_t",
    )(x, w_t)


def _attn_kernel(slopes_ref, lqk_ref, g_ref, q_ref, k_ref, vt_ref, o_ref, *scr, tb, hp, lam_init):
    acc_scr, st_scr, ka_scr = scr[:hp], scr[hp:2 * hp], scr[2 * hp:]
    hg = pl.program_id(1)
    qi = pl.program_id(2)
    s = k_ref.shape[1]
    lq = lqk_ref[...]
    lam = (jnp.exp(jnp.sum(lq[0:1] * lq[1:2], axis=1, keepdims=True))
           - jnp.exp(jnp.sum(lq[2:3] * lq[3:4], axis=1, keepdims=True)) + lam_init)
    slopes = [slopes_ref[hg * hp + hh] for hh in range(hp)]

    @pl.when(qi == 0)
    def _():
        lane = lax.broadcasted_iota(jnp.int32, (s, HEAD_W), 1)
        c = (lax.broadcasted_iota(jnp.int32, (s, HEAD_W), 0) & (tb - 1)).astype(F32)
        for hh in range(hp):
            extra = jnp.where(lane == 0, 1.0, jnp.where(lane == 1, slopes[hh] * c, 0.0))
            ka_scr[hh][:, :HEAD_W] = k_ref[0, :, hh * HEAD_W:(hh + 1) * HEAD_W]
            ka_scr[hh][:, HEAD_W:] = extra.astype(BF16)

    lane = lax.broadcasted_iota(jnp.int32, (tb, HEAD_W), 1)
    lane2 = lax.broadcasted_iota(jnp.int32, (2 * tb, HEAD_W), 1)
    r = (lax.broadcasted_iota(jnp.int32, (2 * tb, HEAD_W), 0) & (tb - 1)).astype(F32)
    masked = ((lax.broadcasted_iota(jnp.int32, (tb, 2 * tb), 1) & (tb - 1))
              < lax.broadcasted_iota(jnp.int32, (tb, 2 * tb), 0))
    qqs = []
    for hh in range(hp):
        qs = q_ref[0, :, hh * HEAD_W:(hh + 1) * HEAD_W].astype(F32) * (HEAD_DIM ** -0.5)
        zero = jnp.zeros_like(qs)
        qq = jnp.concatenate([jnp.where(lane < HEAD_DIM, qs, zero),
                              jnp.where(lane >= HEAD_DIM, qs, zero)], axis=0)
        extra = jnp.where(lane2 == 0, -slopes[hh] * r, jnp.where(lane2 == 1, 1.0, 0.0))
        qqs.append(jnp.concatenate([qq, extra], axis=1).T.astype(BF16))
        acc_scr[hh][...] = jnp.zeros_like(acc_scr[hh])

    def scores(j, hh):
        start = pl.multiple_of(j * tb, tb)
        st_scr[hh][...] = jnp.dot(ka_scr[hh][pl.ds(start, tb), :], qqs[hh],
                                  preferred_element_type=F32)

    def step(j, ms, last):
        shift = (qi - j) * tb
        out = []
        for hh in range(hp):
            off = slopes[hh] * shift.astype(F32)
            st = st_scr[hh][...]
            if last:
                st = jnp.where(masked, -jnp.inf, st)
            m_new = jnp.maximum(ms[hh], jnp.max(st, axis=0, keepdims=True) - off)
            pt = jnp.exp(st - (m_new + off)).astype(BF16)
            a = jnp.exp(ms[hh] - m_new)
            if not last:
                scores(j + 1, hh)
            acc_scr[hh][...] = a * acc_scr[hh][...] + jnp.dot(vt_ref[hh, j], pt,
                                                              preferred_element_type=F32)
            out.append(m_new)
        return tuple(out)

    for hh in range(hp):
        scores(0, hh)
    ms = tuple(jnp.full((1, 2 * tb), -jnp.inf, F32) for _ in range(hp))
    ms = lax.fori_loop(0, qi, functools.partial(step, last=False), ms)
    step(qi, ms, last=True)

    for hh in range(hp):
        acc = acc_scr[hh][...]
        ot = acc[:HEAD_W] * (1.0 / acc[HEAD_W:HEAD_W + 1])
        o = (ot[:, :tb] - lam * ot[:, tb:]).T
        o = o * lax.rsqrt(jnp.mean(o * o, axis=-1, keepdims=True) + RMS_EPS)
        o = o * g_ref[...] * (1.0 - lam_init)
        o_ref[0, :, hh * HEAD_W:(hh + 1) * HEAD_W] = o.astype(o_ref.dtype)


def _diff_attention(qk, vt, slopes, lambda_qk, subln_g, lam_init, tb, hp):
    b, s, _ = qk.shape
    assert tb & (tb - 1) == 0 and tb <= 256
    kern = functools.partial(_attn_kernel, tb=tb, hp=hp, lam_init=lam_init)
    gw = hp * HEAD_W
    vrows = vt.shape[2]
    return pl.pallas_call(
        kern,
        grid=(b, ATTN_HEADS // hp, s // tb),
        in_specs=[pl.BlockSpec(memory_space=pltpu.SMEM),
                  pl.BlockSpec((4, HEAD_DIM), lambda b_, h, i: (0, 0)),
                  pl.BlockSpec((1, HEAD_W), lambda b_, h, i: (0, 0)),
                  pl.BlockSpec((1, tb, gw), lambda b_, h, i: (b_, i, h)),
                  pl.BlockSpec((1, s, gw), lambda b_, h, i: (b_, 0, ATTN_HEADS // hp + h),
                               pipeline_mode=pl.Buffered(1)),
                  pl.BlockSpec((hp, s // tb, vrows, tb), lambda b_, h, i: (h, b_, 0, 0),
                               pipeline_mode=pl.Buffered(1))],
        out_specs=pl.BlockSpec((1, tb, gw), lambda b_, h, i: (b_, i, h)),
        out_shape=jax.ShapeDtypeStruct((b, s, ATTN_HEADS * HEAD_W), BF16),
        scratch_shapes=([pltpu.VMEM((vrows, 2 * tb), F32)] * hp
                        + [pltpu.VMEM((tb, 2 * tb), F32)] * hp
                        + [pltpu.VMEM((s, 2 * HEAD_W), BF16)] * hp),
        compiler_params=_params("parallel", "parallel", "arbitrary"),
        name="diff_attn",
    )(slopes, lambda_qk, subln_g, qk, qk, vt)


def _gelu(x):
    return jax.nn.gelu(x)


def _gelu_tanh(x):
    c = math.sqrt(2.0 / math.pi)
    hx = 0.5 * x
    return hx + hx * jnp.tanh(x * (c + (c * 0.044715) * (x * x)))


def _lru_kernel(xr_ref, gr_ref, cw_ref, cb_ref, gaw_ref, gab_ref, gxw_ref, gxb_ref, lam_ref,
                o_ref, tail_scr, h_scr, a_scr, u_scr, gg_scr, *, ts):
    si = pl.program_id(1)

    @pl.when(si == 0)
    def _():
        tail_scr[...] = jnp.zeros_like(tail_scr)
        h_scr[...] = jnp.zeros_like(h_scr)

    x = xr_ref[0].astype(F32)
    w = x.shape[1]
    tail = tail_scr[...]
    row8 = lax.broadcasted_iota(jnp.int32, (SUBLANES, w), 0)
    cw = cw_ref[...]
    xb = cb_ref[...]
    for d in range(CONV_W - 1, 0, -1):
        xs = pltpu.roll(x, d, 0)
        top = jnp.where(row8 < d, pltpu.roll(tail, d, 0), xs[:SUBLANES])
        xs = jnp.concatenate([top, xs[SUBLANES:]], axis=0)
        xb = xb + xs * cw[CONV_W - 1 - d:CONV_W - d]
    xb = xb + x * cw[CONV_W - 1:CONV_W]
    tail_scr[...] = x[ts - SUBLANES:]

    xbb = xb.astype(BF16)
    bw = w // LRU_BLOCKS

    def gate(w_ref, b_ref):
        parts = [jnp.dot(xbb[:, g * bw:(g + 1) * bw], w_ref[g], preferred_element_type=F32)
                 for g in range(LRU_BLOCKS)]
        return jax.nn.sigmoid(jnp.concatenate(parts, axis=1) + b_ref[...])

    r = gate(gaw_ref, gab_ref)
    i = gate(gxw_ref, gxb_ref)
    lam = lam_ref[...]
    softplus_neg = jnp.maximum(-lam, 0.0) + jnp.log1p(jnp.exp(-jnp.abs(lam)))
    log_a = -LRU_C * r * softplus_neg
    a = jnp.exp(log_a)
    mult = jnp.sqrt(1.0 - jnp.exp(2.0 * log_a))
    first = jnp.where(jnp.logical_and(row8 == 0, si == 0), 1.0, mult[:SUBLANES])
    mult = jnp.concatenate([first, mult[SUBLANES:]], axis=0)
    a_scr[...] = a
    u_scr[...] = mult * (i * xb)
    gg_scr[...] = _gelu(gr_ref[0].astype(F32))

    def group(g, hprev):
        r0 = pl.multiple_of(g * SUBLANES, SUBLANES)
        ag = a_scr[pl.ds(r0, SUBLANES), :]
        ug = u_scr[pl.ds(r0, SUBLANES), :]
        for d in (1, 2, 4):
            keep = row8 >= d
            ug = jnp.where(keep, ag * pltpu.roll(ug, d, 0) + ug, ug)
            ag = jnp.where(keep, ag * pltpu.roll(ag, d, 0), ag)
        hg = ag * hprev + ug
        o_ref[0, pl.ds(r0, SUBLANES), :] = (hg * gg_scr[pl.ds(r0, SUBLANES), :]).astype(o_ref.dtype)
        return jnp.broadcast_to(hg[SUBLANES - 1:SUBLANES, :], (SUBLANES, w))

    h_scr[...] = lax.fori_loop(0, ts // SUBLANES, group, h_scr[...])


def _rg_lru(rest, conv_w, conv_b, gaw, gab, gxw, gxb, lru_lambda, ts):
    b, s, _ = rest.shape
    w = conv_w.shape[1]
    kern = functools.partial(_lru_kernel, ts=ts)
    vec = pl.BlockSpec((1, w), lambda b_, i: (0, 0))
    blk = pl.BlockSpec(gaw.shape, lambda b_, i: (0, 0, 0))
    return pl.pallas_call(
        kern,
        grid=(b, s // ts),
        in_specs=[pl.BlockSpec((1, ts, w), lambda b_, i: (b_, i, PROJ_LRU_COL)),
                  pl.BlockSpec((1, ts, w), lambda b_, i: (b_, i, PROJ_LRU_COL + 1)),
                  pl.BlockSpec((CONV_W, w), lambda b_, i: (0, 0)),
                  vec, blk, vec, blk, vec, vec],
        out_specs=pl.BlockSpec((1, ts, w), lambda b_, i: (b_, i, 0)),
        out_shape=jax.ShapeDtypeStruct((b, s, w), BF16),
        scratch_shapes=[pltpu.VMEM((SUBLANES, w), F32), pltpu.VMEM((SUBLANES, w), F32),
                        pltpu.VMEM((ts, w), F32), pltpu.VMEM((ts, w), F32),
                        pltpu.VMEM((ts, w), F32)],
        compiler_params=_params("parallel", "arbitrary"),
        name="rg_lru",
    )(rest, rest, conv_w, conv_b, gaw, gab, gxw, gxb, lru_lambda)


def _layer_norm(z, g, b):
    mu = jnp.mean(z, axis=-1, keepdims=True)
    zc = z - mu
    var = jnp.mean(zc * zc, axis=-1, keepdims=True)
    return zc * lax.rsqrt(var + LN_EPS) * g + b


def _merge_kernel(ya_ref, yr_ref, ga_ref, gl_ref, x_ref, wa_ref, wl_ref, wo_ref, g_ref, b_ref,
                  o_ref, ob_ref, *, alpha):
    pa = jnp.dot(ya_ref[...], wa_ref[...], preferred_element_type=F32)
    pr = jnp.dot(yr_ref[...], wl_ref[...], preferred_element_type=F32)
    merged = (jax.nn.sigmoid(ga_ref[...].astype(F32)) * pa
              + jax.nn.sigmoid(gl_ref[...].astype(F32)) * pr)
    mix = jnp.dot(merged.astype(BF16), wo_ref[...], preferred_element_type=F32)
    y = _layer_norm(alpha * x_ref[...] + mix, g_ref[...], b_ref[...])
    o_ref[...] = y
    ob_ref[...] = y.T.astype(BF16)


def _merge_out(ya, yr, rest, x, wa, wl, wo, ln_g, ln_b, alpha, tm):
    t, d = x.shape
    kern = functools.partial(_merge_kernel, alpha=alpha)
    rows = lambda c: pl.BlockSpec((tm, d), lambda i: (i, c))
    full = pl.BlockSpec((d, d), lambda i: (0, 0))
    vec = pl.BlockSpec((1, d), lambda i: (0, 0))
    return pl.pallas_call(
        kern,
        grid=(t // tm,),
        in_specs=[rows(0), rows(0), rows(PROJ_GATE_COL), rows(PROJ_GATE_COL + 1), rows(0),
                  full, full, full, vec, vec],
        out_specs=[rows(0), pl.BlockSpec((d, tm), lambda i: (0, i))],
        out_shape=[jax.ShapeDtypeStruct((t, d), F32), jax.ShapeDtypeStruct((d, t), BF16)],
        compiler_params=_params("parallel"),
        name="merge_out_ln",
    )(ya, yr, rest, rest, x, wa, wl, wo, ln_g, ln_b)


def _cmpx(v, i, j):
    hi = jnp.maximum(v[i], v[j])
    v[j] = jnp.minimum(v[i], v[j])
    v[i] = hi


def _bitonic_merge_desc(v):
    n = len(v)
    j = n // 2
    while j >= 1:
        for i in range(n):
            if i & j == 0:
                _cmpx(v, i, i + j)
        j //= 2


def _bitonic_sort_desc(v):
    n = len(v)
    k = 2
    while k <= n:
        j = k // 2
        while j >= 1:
            for i in range(n):
                l = i ^ j
                if l > i:
                    if i & k == 0:
                        _cmpx(v, i, l)
                    else:
                        _cmpx(v, l, i)
            j //= 2
        k *= 2


def _top_merge(a, b):
    n = len(a)
    v = [jnp.maximum(a[i], b[n - 1 - i]) for i in range(n)]
    _bitonic_merge_desc(v)
    return v


def _prefix_count(b, test):
    assert len(b) == 16
    c8 = test(b[7])
    n = jnp.where(c8, 8.0, 0.0)
    c4 = test(jnp.where(c8, b[11], b[3]))
    n = n + jnp.where(c4, 4.0, 0.0)
    c2 = test(jnp.where(c8, jnp.where(c4, b[13], b[9]), jnp.where(c4, b[5], b[1])))
    n = n + jnp.where(c2, 2.0, 0.0)
    lo = jnp.where(c4, jnp.where(c2, b[6], b[4]), jnp.where(c2, b[2], b[0]))
    hi = jnp.where(c4, jnp.where(c2, b[14], b[12]), jnp.where(c2, b[10], b[8]))
    n = n + jnp.where(test(jnp.where(c8, hi, lo)), 1.0, 0.0)
    return jnp.where(test(b[15]), 16.0, n)


_CAND_PAIRS = [(p, q) for p in range(PEER_TOPK) for q in range(PEER_TOPK)
               if (p + 1) * (q + 1) <= PEER_TOPK]


def _select_kernel(x_ref, wq_ref, sk_ref, rank_ref, bexp_ref, cnt_ref, arow_ref,
                   q_scr, s_scr, pk_scr, t_scr, *, tt):
    qt = jnp.dot(wq_ref[...], x_ref[...], preferred_element_type=F32)
    q_scr[...] = qt.astype(BF16)
    dk = sk_ref.shape[3]
    n_grp = N_KEYS // SUBLANES

    def head(h, _):
        for c in range(2):
            r0 = pl.multiple_of((h * 2 + c) * dk, dk)
            sc = jnp.dot(sk_ref[h, c], q_scr[pl.ds(r0, dk), :], preferred_element_type=F32)
            s_scr[c, h] = sc
            v = [sc[g * SUBLANES:(g + 1) * SUBLANES, :] for g in range(n_grp)]
            _bitonic_sort_desc(v)
            for sh in (4, 2, 1):
                v = _top_merge(v, [pltpu.roll(e, sh, 0) for e in v])
            for p in range(PEER_TOPK):
                pk_scr[c, p, pl.ds(h, 1), :] = v[p][0:1, :]
        return 0

    lax.fori_loop(0, PEER_HEADS, head, 0)

    a = [pk_scr[0, p] for p in range(PEER_TOPK)]
    b = [pk_scr[1, p] for p in range(PEER_TOPK)]
    cands = [a[p] + b[q] for p, q in _CAND_PAIRS]
    neg = jnp.full_like(cands[0], -jnp.inf)
    blocks = cands + [neg] * ((-len(cands)) % PEER_TOPK)
    top = None
    for i0 in range(0, len(blocks), PEER_TOPK):
        blk = blocks[i0:i0 + PEER_TOPK]
        _bitonic_sort_desc(blk)
        top = blk if top is None else _top_merge(top, blk)
    tau = top[PEER_TOPK - 1]
    mx = a[0] + b[0]
    z = jnp.zeros_like(tau)
    for cnd in cands:
        z = z + jnp.where(cnd >= tau, jnp.exp(cnd - mx), 0.0)
    t_scr[0] = tau
    t_scr[1] = a[0] + jnp.log(z)

    def head2(h, _):
        s1 = s_scr[0, h]
        s2 = s_scr[1, h]
        tau_h = t_scr[0, pl.ds(h, 1), :]
        bs = [pk_scr[1, q, pl.ds(h, 1), :] for q in range(PEER_TOPK)]
        rank_ref[h] = _prefix_count(bs, lambda thr: thr > s2).astype(BF16)
        cnt_ref[h] = _prefix_count(bs, lambda thr: s1 + thr >= tau_h)
        arow_ref[h] = jnp.exp(s1 - t_scr[1, pl.ds(h, 1), :])
        bexp_ref[h] = jnp.exp(s2 - pk_scr[1, 0, pl.ds(h, 1), :]).astype(BF16)
        return 0

    lax.fori_loop(0, PEER_HEADS, head2, 0)


def _peer_select(xt, wq_t, sk, tt):
    d, t = xt.shape
    nq = wq_t.shape[0]
    kern = functools.partial(_select_kernel, tt=tt)
    big = pl.BlockSpec((PEER_HEADS, N_KEYS, tt), lambda i: (0, 0, i))
    shape = lambda dt: jax.ShapeDtypeStruct((PEER_HEADS, N_KEYS, t), dt)
    return pl.pallas_call(
        kern,
        grid=(t // tt,),
        in_specs=[pl.BlockSpec((d, tt), lambda i: (0, i)),
                  pl.BlockSpec((nq, d), lambda i: (0, 0)),
                  pl.BlockSpec(sk.shape, lambda i: (0, 0, 0, 0))],
        out_specs=[big, big, big, big],
        out_shape=[shape(BF16), shape(BF16), shape(F32), shape(F32)],
        scratch_shapes=[pltpu.VMEM((nq, tt), BF16),
                        pltpu.VMEM((2, PEER_HEADS, N_KEYS, tt), F32),
                        pltpu.VMEM((2, PEER_TOPK, PEER_HEADS, tt), F32),
                        pltpu.VMEM((2, PEER_HEADS, tt), F32)],
        compiler_params=_params("parallel"),
        name="peer_select",
    )(xt, wq_t, sk)


def _peer_kernel(xt_ref, x_ref, u_ref, vt_ref, rank_ref, bexp_ref, cnt_ref, arow_ref,
                 g_ref, b_ref, o_ref, acc_scr, row_scr, *bufs, tt, nblk, eb, alpha):
    e = pl.program_id(1)

    @pl.when(e == 0)
    def _():
        acc_scr[...] = jnp.zeros_like(acc_scr)

    acts, ws = bufs[:N_ACT_BUFS], bufs[N_ACT_BUFS:]
    rows_per_blk = eb // N_KEYS
    half = acc_scr.shape[0] // 2

    def mm1(k):
        acts[k % N_ACT_BUFS][...] = jnp.dot(u_ref[k * eb:(k + 1) * eb, :], xt_ref[...],
                                            preferred_element_type=F32)

    def mm2(p, part):
        rows = slice(part * half, (part + 1) * half)
        acc_scr[rows, :] += jnp.dot(vt_ref[p, rows, :], ws[p % N_W_BUFS][...],
                                    preferred_element_type=F32)

    def gates(k):
        act, w = acts[k % N_ACT_BUFS], ws[(k // 2) % N_W_BUFS]
        for ii in range(rows_per_blk):
            i = k * rows_per_blk + ii
            for h in range(PEER_HEADS):
                for a, ref in enumerate((cnt_ref, arow_ref)):
                    row_scr[a, h] = jnp.broadcast_to(ref[h, i:i + 1, :], (GATE_ROWS, tt)).astype(BF16)
            for j0 in range(0, N_KEYS, GATE_ROWS):
                r = ii * N_KEYS + j0
                rw = (k % 2) * eb + r
                gsum = None
                for h in range(PEER_HEADS):
                    sel = rank_ref[h, j0:j0 + GATE_ROWS, :] < row_scr[0, h]
                    term = jnp.where(sel, row_scr[1, h] * bexp_ref[h, j0:j0 + GATE_ROWS, :],
                                     jnp.zeros((), BF16))
                    gsum = term if gsum is None else gsum + term
                w[rw:rw + GATE_ROWS, :] = _gelu_tanh(act[r:r + GATE_ROWS, :].astype(BF16)) * gsum

    for k in range(MM1_AHEAD):
        mm1(k)
    for k in range(nblk):
        if k + MM1_AHEAD < nblk:
            mm1(k + MM1_AHEAD)
        if k >= 2:
            mm2(k // 2 - 1, k % 2)
        gates(k)
    mm2(nblk // 2 - 1, 0)
    mm2(nblk // 2 - 1, 1)

    @pl.when(e == pl.num_programs(1) - 1)
    def _():
        ffn = acc_scr[...].T
        o_ref[...] = _layer_norm(alpha * x_ref[...] + ffn, g_ref[...], b_ref[...])


def _peer_mix(xt, x, u, vt, rank, bexp, cnt, arow, ln_g, ln_b, alpha, tt, te, eb):
    t, d = x.shape
    n_exp = u.shape[0]
    nblk = te // eb
    kern = functools.partial(_peer_kernel, tt=tt, nblk=nblk, eb=eb, alpha=alpha)
    big = pl.BlockSpec((PEER_HEADS, N_KEYS, tt), lambda i, e: (0, 0, i))
    rows = pl.BlockSpec((PEER_HEADS, te // N_KEYS, tt), lambda i, e: (0, e, i))
    vec = pl.BlockSpec((1, d), lambda i, e: (0, 0))
    return pl.pallas_call(
        kern,
        grid=(t // tt, n_exp // te),
        in_specs=[pl.BlockSpec((d, tt), lambda i, e: (0, i)),
                  pl.BlockSpec((tt, d), lambda i, e: (i, 0)),
                  pl.BlockSpec((te, d), lambda i, e: (e, 0)),
                  pl.BlockSpec((nblk // 2, d, 2 * eb), lambda i, e: (e, 0, 0)),
                  big, big, rows, rows, vec, vec],
        out_specs=pl.BlockSpec((tt, d), lambda i, e: (i, 0)),
        out_shape=jax.ShapeDtypeStruct((t, d), F32),
        scratch_shapes=([pltpu.VMEM((d, tt), F32),
                         pltpu.VMEM((2, PEER_HEADS, GATE_ROWS, tt), BF16)]
                        + [pltpu.VMEM((eb, tt), F32)] * N_ACT_BUFS
                        + [pltpu.VMEM((2 * eb, tt), BF16)] * N_W_BUFS),
        compiler_params=_params("parallel", "arbitrary"),
        name="peer_mix_ln",
    )(xt, x, u, vt, rank, bexp, cnt, arow, ln_g, ln_b)


def _tiles(b, s, t):
    return dict(mm_tm=min(2048, t), mm_tn=512, attn_tb=min(256, s), attn_hp=8,
                lru_ts=min(512, s), merge_tm=min(512, t), sel_tt=min(512, t),
                mix_tt=min(512, t), mix_te=2048, mix_eb=256)


def kernel(x, w_in, lambda_qk, subln_g, conv_w, conv_b, gate_a_w, gate_a_b, gate_x_w, gate_x_b,
           lru_lambda, w_br_attn, w_br_lru, w_out, ln1_g, ln1_b, peer_wq, peer_subkeys, peer_u,
           peer_v, ln2_g, ln2_b):
    b, s, d = x.shape
    t = b * s
    depth = w_in.shape[0]
    alpha = (2.0 * depth) ** 0.25
    attn_w = ATTN_HEADS * HEAD_W
    ts = _tiles(b, s, t)
    slopes = jnp.exp2(-8.0 * jnp.arange(1, ATTN_HEADS + 1, dtype=F32) / ATTN_HEADS)
    row = lambda a: a.reshape(1, -1)
    eb = ts["mix_eb"]

    xf = x.reshape(t, d)
    for l in range(depth):
        lam_init = 0.8 - 0.6 * math.exp(-0.3 * l)
        w_l = w_in[l].astype(BF16)
        w_cat = jnp.concatenate([w_l[:, :2 * attn_w], w_l[:, 3 * attn_w:]], axis=1)
        proj = _matmul(xf, w_cat, BF16, ts["mm_tm"], ts["mm_tn"])
        vt = _value_proj_t(xf, w_l[:, 2 * attn_w:3 * attn_w].T, ts["attn_tb"])
        proj3 = proj.reshape(b, s, -1)
        ya = _diff_attention(proj3, vt, slopes, lambda_qk[l], row(subln_g[l]),
                             lam_init, ts["attn_tb"], ts["attn_hp"])
        yr = _rg_lru(proj3, conv_w[l], row(conv_b[l]), gate_a_w[l].astype(BF16),
                     row(gate_a_b[l]), gate_x_w[l].astype(BF16), row(gate_x_b[l]),
                     row(lru_lambda[l]), ts["lru_ts"])
        x1, x1t = _merge_out(ya.reshape(t, attn_w), yr.reshape(t, d), proj, xf,
                             w_br_attn[l].astype(BF16), w_br_lru[l].astype(BF16),
                             w_out[l].astype(BF16), row(ln1_g[l]), row(ln1_b[l]), alpha,
                             ts["merge_tm"])
        rank, bexp, cnt, arow = _peer_select(x1t, peer_wq[l].T.astype(BF16),
                                             peer_subkeys[l].astype(BF16), ts["sel_tt"])
        v_blocks = peer_v[l].astype(BF16).reshape(-1, 2 * eb, d).transpose(0, 2, 1)
        xf = _peer_mix(x1t, x1, peer_u[l].astype(BF16), v_blocks, rank, bexp, cnt, arow,
                       row(ln2_g[l]), row(ln2_b[l]), alpha, ts["mix_tt"], ts["mix_te"], eb)
    return xf.reshape(b, s, d)
```

```python
import functools
import math

import jax
import jax.numpy as jnp
from jax import lax
from jax.experimental import pallas as pl
from jax.experimental.pallas import tpu as pltpu

F32 = jnp.float32
BF16 = jnp.bfloat16

ATTN_HEADS = 8
HEAD_DIM = 64
HEAD_W = 2 * HEAD_DIM
LRU_BLOCKS = 8
LRU_C = 8.0
CONV_W = 4
PEER_HEADS = 8
N_KEYS = 128
PEER_TOPK = 16
LN_EPS = 1e-5
RMS_EPS = 1e-6
PROJ_LRU_COL = 2
PROJ_GATE_COL = 4
SUBLANES = 8
PACKED_ROWS = 16
V_AUG_ROWS = PACKED_ROWS
GATE_ROWS = PACKED_ROWS
MM1_AHEAD = 2
N_ACT_BUFS = MM1_AHEAD + 2
N_W_BUFS = 3
VMEM_LIMIT = 56 * 1024 * 1024


def _params(*sem):
    return pltpu.CompilerParams(dimension_semantics=sem, vmem_limit_bytes=VMEM_LIMIT)


MM_ROWS = 256


def _mm_kernel(x_ref, w_ref, o_ref):
    for r in range(0, x_ref.shape[0], MM_ROWS):
        o_ref[r:r + MM_ROWS, :] = jnp.dot(x_ref[r:r + MM_ROWS, :].astype(BF16), w_ref[...],
                                          preferred_element_type=F32).astype(o_ref.dtype)


def _matmul(x, w, out_dtype, tm, tn):
    t, k = x.shape
    n = w.shape[1]
    return pl.pallas_call(
        _mm_kernel,
        grid=(t // tm, n // tn),
        in_specs=[pl.BlockSpec((tm, k), lambda i, j: (i, 0)),
                  pl.BlockSpec((k, tn), lambda i, j: (0, j))],
        out_specs=pl.BlockSpec((tm, tn), lambda i, j: (i, j)),
        out_shape=jax.ShapeDtypeStruct((t, n), out_dtype),
        compiler_params=_params("parallel", "arbitrary"),
        name="in_proj",
    )(x, w)


def _vt_kernel(x_ref, w_ref, o_ref):
    vt = jnp.dot(w_ref[...], x_ref[...].T.astype(BF16),
                 preferred_element_type=F32)
    tk = vt.shape[1]
    vt = vt.astype(o_ref.dtype).reshape(ATTN_HEADS, HEAD_W, tk)
    ones_row = lax.broadcasted_iota(jnp.int32, (ATTN_HEADS, V_AUG_ROWS, tk), 1) == 0
    extra = jnp.where(ones_row, 1.0, 0.0).astype(o_ref.dtype)
    o_ref[...] = jnp.concatenate([vt, extra], axis=1).reshape(o_ref.shape)


def _value_proj_t(x, w_t, tk):
    t, d = x.shape
    rows = HEAD_W + V_AUG_ROWS
    return pl.pallas_call(
        _vt_kernel,
        grid=(t // tk,),
        in_specs=[pl.BlockSpec((tk, d), lambda i: (i, 0)),
                  pl.BlockSpec(w_t.shape, lambda i: (0, 0))],
        out_specs=pl.BlockSpec((ATTN_HEADS, 1, rows, tk), lambda i: (0, i, 0, 0)),
        out_shape=jax.ShapeDtypeStruct((ATTN_HEADS, t // tk, rows, tk), BF16),
        compiler_params=_params("parallel"),
        name="value_proj_t",
    )(x, w_t)


def _attn_kernel(slopes_ref, lqk_ref, g_ref, q_ref, k_ref, vt_ref, o_ref, *scr, tb, hp, lam_init):
    acc_scr, st_scr, ka_scr = scr[:hp], scr[hp:2 * hp], scr[2 * hp:]
    hg = pl.program_id(1)
    qi = pl.program_id(2)
    s = k_ref.shape[1]
    lq = lqk_ref[...]
    lam = (jnp.exp(jnp.sum(lq[0:1] * lq[1:2], axis=1, keepdims=True))
           - jnp.exp(jnp.sum(lq[2:3] * lq[3:4], axis=1, keepdims=True)) + lam_init)
    slopes = [slopes_ref[hg * hp + hh] for hh in range(hp)]

    @pl.when(qi == 0)
    def _():
        lane = lax.broadcasted_iota(jnp.int32, (s, HEAD_W), 1)
        c = (lax.broadcasted_iota(jnp.int32, (s, HEAD_W), 0) & (tb - 1)).astype(F32)
        for hh in range(hp):
            extra = jnp.where(lane == 0, 1.0, jnp.where(lane == 1, slopes[hh] * c, 0.0))
            ka_scr[hh][:, :HEAD_W] = k_ref[0, :, hh * HEAD_W:(hh + 1) * HEAD_W]
            ka_scr[hh][:, HEAD_W:] = extra.astype(BF16)

    lane = lax.broadcasted_iota(jnp.int32, (tb, HEAD_W), 1)
    lane2 = lax.broadcasted_iota(jnp.int32, (2 * tb, HEAD_W), 1)
    r = (lax.broadcasted_iota(jnp.int32, (2 * tb, HEAD_W), 0) & (tb - 1)).astype(F32)
    masked = ((lax.broadcasted_iota(jnp.int32, (tb, 2 * tb), 1) & (tb - 1))
              < lax.broadcasted_iota(jnp.int32, (tb, 2 * tb), 0))
    qqs = []
    for hh in range(hp):
        qs = q_ref[0, :, hh * HEAD_W:(hh + 1) * HEAD_W].astype(F32) * (HEAD_DIM ** -0.5)
        zero = jnp.zeros_like(qs)
        qq = jnp.concatenate([jnp.where(lane < HEAD_DIM, qs, zero),
                              jnp.where(lane >= HEAD_DIM, qs, zero)], axis=0)
        extra = jnp.where(lane2 == 0, -slopes[hh] * r, jnp.where(lane2 == 1, 1.0, 0.0))
        qqs.append(jnp.concatenate([qq, extra], axis=1).T.astype(BF16))
        acc_scr[hh][...] = jnp.zeros_like(acc_scr[hh])

    def scores(j, hh):
        start = pl.multiple_of(j * tb, tb)
        st_scr[hh][...] = jnp.dot(ka_scr[hh][pl.ds(start, tb), :], qqs[hh],
                                  preferred_element_type=F32)

    def step(j, ms, last):
        shift = (qi - j) * tb
        out = []
        for hh in range(hp):
            off = slopes[hh] * shift.astype(F32)
            st = st_scr[hh][...]
            if last:
                st = jnp.where(masked, -jnp.inf, st)
            m_new = jnp.maximum(ms[hh], jnp.max(st, axis=0, keepdims=True) - off)
            pt = jnp.exp(st - (m_new + off)).astype(BF16)
            a = jnp.exp(ms[hh] - m_new)
            if not last:
                scores(j + 1, hh)
            acc_scr[hh][...] = a * acc_scr[hh][...] + jnp.dot(vt_ref[hh, j], pt,
                                                              preferred_element_type=F32)
            out.append(m_new)
        return tuple(out)

    for hh in range(hp):
        scores(0, hh)
    ms = tuple(jnp.full((1, 2 * tb), -jnp.inf, F32) for _ in range(hp))
    ms = lax.fori_loop(0, qi, functools.partial(step, last=False), ms)
    step(qi, ms, last=True)

    for hh in range(hp):
        acc = acc_scr[hh][...]
        ot = acc[:HEAD_W] * (1.0 / acc[HEAD_W:HEAD_W + 1])
        o = (ot[:, :tb] - lam * ot[:, tb:]).T
        o = o * lax.rsqrt(jnp.mean(o * o, axis=-1, keepdims=True) + RMS_EPS)
        o = o * g_ref[...] * (1.0 - lam_init)
        o_ref[0, :, hh * HEAD_W:(hh + 1) * HEAD_W] = o.astype(o_ref.dtype)


def _diff_attention(qk, vt, slopes, lambda_qk, subln_g, lam_init, tb, hp):
    b, s, _ = qk.shape
    assert tb & (tb - 1) == 0 and tb <= 256
    kern = functools.partial(_attn_kernel, tb=tb, hp=hp, lam_init=lam_init)
    gw = hp * HEAD_W
    vrows = vt.shape[2]
    return pl.pallas_call(
        kern,
        grid=(b, ATTN_HEADS // hp, s // tb),
        in_specs=[pl.BlockSpec(memory_space=pltpu.SMEM),
                  pl.BlockSpec((4, HEAD_DIM), lambda b_, h, i: (0, 0)),
                  pl.BlockSpec((1, HEAD_W), lambda b_, h, i: (0, 0)),
                  pl.BlockSpec((1, tb, gw), lambda b_, h, i: (b_, i, h)),
                  pl.BlockSpec((1, s, gw), lambda b_, h, i: (b_, 0, ATTN_HEADS // hp + h),
                               pipeline_mode=pl.Buffered(1)),
                  pl.BlockSpec((hp, s // tb, vrows, tb), lambda b_, h, i: (h, b_, 0, 0),
                               pipeline_mode=pl.Buffered(1))],
        out_specs=pl.BlockSpec((1, tb, gw), lambda b_, h, i: (b_, i, h)),
        out_shape=jax.ShapeDtypeStruct((b, s, ATTN_HEADS * HEAD_W), BF16),
        scratch_shapes=([pltpu.VMEM((vrows, 2 * tb), F32)] * hp
                        + [pltpu.VMEM((tb, 2 * tb), F32)] * hp
                        + [pltpu.VMEM((s, 2 * HEAD_W), BF16)] * hp),
        compiler_params=_params("parallel", "parallel", "arbitrary"),
        name="diff_attn",
    )(slopes, lambda_qk, subln_g, qk, qk, vt)


def _gelu(x):
    return jax.nn.gelu(x)


def _gelu_tanh(x):
    c = math.sqrt(2.0 / math.pi)
    hx = 0.5 * x
    return hx + hx * jnp.tanh(x * (c + (c * 0.044715) * (x * x)))


def _lru_kernel(xr_ref, gr_ref, cw_ref, cb_ref, gaw_ref, gab_ref, gxw_ref, gxb_ref, lam_ref,
                o_ref, tail_scr, h_scr, a_scr, u_scr, gg_scr, *, ts):
    si = pl.program_id(1)

    @pl.when(si == 0)
    def _():
        tail_scr[...] = jnp.zeros_like(tail_scr)
        h_scr[...] = jnp.zeros_like(h_scr)

    x = xr_ref[0].astype(F32)
    w = x.shape[1]
    tail = tail_scr[...]
    row8 = lax.broadcasted_iota(jnp.int32, (SUBLANES, w), 0)
    cw = cw_ref[...]
    xb = cb_ref[...]
    for d in range(CONV_W - 1, 0, -1):
        xs = pltpu.roll(x, d, 0)
        top = jnp.where(row8 < d, pltpu.roll(tail, d, 0), xs[:SUBLANES])
        xs = jnp.concatenate([top, xs[SUBLANES:]], axis=0)
        xb = xb + xs * cw[CONV_W - 1 - d:CONV_W - d]
    xb = xb + x * cw[CONV_W - 1:CONV_W]
    tail_scr[...] = x[ts - SUBLANES:]

    xbb = xb.astype(BF16)
    bw = w // LRU_BLOCKS

    def gate(w_ref, b_ref):
        parts = [jnp.dot(xbb[:, g * bw:(g + 1) * bw], w_ref[g], preferred_element_type=F32)
                 for g in range(LRU_BLOCKS)]
        return jax.nn.sigmoid(jnp.concatenate(parts, axis=1) + b_ref[...])

    r = gate(gaw_ref, gab_ref)
    i = gate(gxw_ref, gxb_ref)
    lam = lam_ref[...]
    softplus_neg = jnp.maximum(-lam, 0.0) + jnp.log1p(jnp.exp(-jnp.abs(lam)))
    log_a = -LRU_C * r * softplus_neg
    a = jnp.exp(log_a)
    mult = jnp.sqrt(1.0 - jnp.exp(2.0 * log_a))
    first = jnp.where(jnp.logical_and(row8 == 0, si == 0), 1.0, mult[:SUBLANES])
    mult = jnp.concatenate([first, mult[SUBLANES:]], axis=0)
    a_scr[...] = a
    u_scr[...] = mult * (i * xb)
    gg_scr[...] = _gelu(gr_ref[0].astype(F32))

    def group(g, hprev):
        r0 = pl.multiple_of(g * SUBLANES, SUBLANES)
        ag = a_scr[pl.ds(r0, SUBLANES), :]
        ug = u_scr[pl.ds(r0, SUBLANES), :]
        for d in (1, 2, 4):
            keep = row8 >= d
            ug = jnp.where(keep, ag * pltpu.roll(ug, d, 0) + ug, ug)
            ag = jnp.where(keep, ag * pltpu.roll(ag, d, 0), ag)
        hg = ag * hprev + ug
        o_ref[0, pl.ds(r0, SUBLANES), :] = (hg * gg_scr[pl.ds(r0, SUBLANES), :]).astype(o_ref.dtype)
        return jnp.broadcast_to(hg[SUBLANES - 1:SUBLANES, :], (SUBLANES, w))

    h_scr[...] = lax.fori_loop(0, ts // SUBLANES, group, h_scr[...])


def _rg_lru(rest, conv_w, conv_b, gaw, gab, gxw, gxb, lru_lambda, ts):
    b, s, _ = rest.shape
    w = conv_w.shape[1]
    kern = functools.partial(_lru_kernel, ts=ts)
    vec = pl.BlockSpec((1, w), lambda b_, i: (0, 0))
    blk = pl.BlockSpec(gaw.shape, lambda b_, i: (0, 0, 0))
    return pl.pallas_call(
        kern,
        grid=(b, s // ts),
        in_specs=[pl.BlockSpec((1, ts, w), lambda b_, i: (b_, i, PROJ_LRU_COL)),
                  pl.BlockSpec((1, ts, w), lambda b_, i: (b_, i, PROJ_LRU_COL + 1)),
                  pl.BlockSpec((CONV_W, w), lambda b_, i: (0, 0)),
                  vec, blk, vec, blk, vec, vec],
        out_specs=pl.BlockSpec((1, ts, w), lambda b_, i: (b_, i, 0)),
        out_shape=jax.ShapeDtypeStruct((b, s, w), BF16),
        scratch_shapes=[pltpu.VMEM((SUBLANES, w), F32), pltpu.VMEM((SUBLANES, w), F32),
                        pltpu.VMEM((ts, w), F32), pltpu.VMEM((ts, w), F32),
                        pltpu.VMEM((ts, w), F32)],
        compiler_params=_params("parallel", "arbitrary"),
        name="rg_lru",
    )(rest, rest, conv_w, conv_b, gaw, gab, gxw, gxb, lru_lambda)


def _layer_norm(z, g, b):
    mu = jnp.mean(z, axis=-1, keepdims=True)
    zc = z - mu
    var = jnp.mean(zc * zc, axis=-1, keepdims=True)
    return zc * lax.rsqrt(var + LN_EPS) * g + b


def _merge_kernel(ya_ref, yr_ref, ga_ref, gl_ref, x_ref, wa_ref, wl_ref, wo_ref, g_ref, b_ref,
                  o_ref, ob_ref, *, alpha):
    pa = jnp.dot(ya_ref[...], wa_ref[...], preferred_element_type=F32)
    pr = jnp.dot(yr_ref[...], wl_ref[...], preferred_element_type=F32)
    merged = (jax.nn.sigmoid(ga_ref[...].astype(F32)) * pa
              + jax.nn.sigmoid(gl_ref[...].astype(F32)) * pr)
    mix = jnp.dot(merged.astype(BF16), wo_ref[...], preferred_element_type=F32)
    y = _layer_norm(alpha * x_ref[...] + mix, g_ref[...], b_ref[...])
    o_ref[...] = y
    ob_ref[...] = y.T.astype(BF16)


def _merge_out(ya, yr, rest, x, wa, wl, wo, ln_g, ln_b, alpha, tm):
    t, d = x.shape
    kern = functools.partial(_merge_kernel, alpha=alpha)
    rows = lambda c: pl.BlockSpec((tm, d), lambda i: (i, c))
    full = pl.BlockSpec((d, d), lambda i: (0, 0))
    vec = pl.BlockSpec((1, d), lambda i: (0, 0))
    return pl.pallas_call(
        kern,
        grid=(t // tm,),
        in_specs=[rows(0), rows(0), rows(PROJ_GATE_COL), rows(PROJ_GATE_COL + 1), rows(0),
                  full, full, full, vec, vec],
        out_specs=[rows(0), pl.BlockSpec((d, tm), lambda i: (0, i))],
        out_shape=[jax.ShapeDtypeStruct((t, d), F32), jax.ShapeDtypeStruct((d, t), BF16)],
        compiler_params=_params("parallel"),
        name="merge_out_ln",
    )(ya, yr, rest, rest, x, wa, wl, wo, ln_g, ln_b)


def _cmpx(v, i, j):
    hi = jnp.maximum(v[i], v[j])
    v[j] = jnp.minimum(v[i], v[j])
    v[i] = hi


def _bitonic_merge_desc(v):
    n = len(v)
    j = n // 2
    while j >= 1:
        for i in range(n):
            if i & j == 0:
                _cmpx(v, i, i + j)
        j //= 2


def _bitonic_sort_desc(v):
    n = len(v)
    k = 2
    while k <= n:
        j = k // 2
        while j >= 1:
            for i in range(n):
                l = i ^ j
                if l > i:
                    if i & k == 0:
                        _cmpx(v, i, l)
                    else:
                        _cmpx(v, l, i)
            j //= 2
        k *= 2


def _top_merge(a, b):
    n = len(a)
    v = [jnp.maximum(a[i], b[n - 1 - i]) for i in range(n)]
    _bitonic_merge_desc(v)
    return v


def _prefix_count(b, test):
    assert len(b) == 16
    c8 = test(b[7])
    n = jnp.where(c8, 8.0, 0.0)
    c4 = test(jnp.where(c8, b[11], b[3]))
    n = n + jnp.where(c4, 4.0, 0.0)
    c2 = test(jnp.where(c8, jnp.where(c4, b[13], b[9]), jnp.where(c4, b[5], b[1])))
    n = n + jnp.where(c2, 2.0, 0.0)
    lo = jnp.where(c4, jnp.where(c2, b[6], b[4]), jnp.where(c2, b[2], b[0]))
    hi = jnp.where(c4, jnp.where(c2, b[14], b[12]), jnp.where(c2, b[10], b[8]))
    n = n + jnp.where(test(jnp.where(c8, hi, lo)), 1.0, 0.0)
    return jnp.where(test(b[15]), 16.0, n)


_CAND_PAIRS = [(p, q) for p in range(PEER_TOPK) for q in range(PEER_TOPK)
               if (p + 1) * (q + 1) <= PEER_TOPK]


def _select_kernel(x_ref, wq_ref, sk_ref, rank_ref, bexp_ref, cnt_ref, arow_ref,
                   q_scr, s_scr, pk_scr, t_scr, *, tt):
    qt = jnp.dot(wq_ref[...], x_ref[...], preferred_element_type=F32)
    q_scr[...] = qt.astype(BF16)
    dk = sk_ref.shape[3]
    n_grp = N_KEYS // SUBLANES

    def head(h, _):
        for c in range(2):
            r0 = pl.multiple_of((h * 2 + c) * dk, dk)
            sc = jnp.dot(sk_ref[h, c], q_scr[pl.ds(r0, dk), :], preferred_element_type=F32)
            s_scr[c, h] = sc
            v = [sc[g * SUBLANES:(g + 1) * SUBLANES, :] for g in range(n_grp)]
            _bitonic_sort_desc(v)
            for sh in (4, 2, 1):
                v = _top_merge(v, [pltpu.roll(e, sh, 0) for e in v])
            for p in range(PEER_TOPK):
                pk_scr[c, p, pl.ds(h, 1), :] = v[p][0:1, :]
        return 0

    lax.fori_loop(0, PEER_HEADS, head, 0)

    a = [pk_scr[0, p] for p in range(PEER_TOPK)]
    b = [pk_scr[1, p] for p in range(PEER_TOPK)]
    cands = [a[p] + b[q] for p, q in _CAND_PAIRS]
    neg = jnp.full_like(cands[0], -jnp.inf)
    blocks = cands + [neg] * ((-len(cands)) % PEER_TOPK)
    top = None
    for i0 in range(0, len(blocks), PEER_TOPK):
        blk = blocks[i0:i0 + PEER_TOPK]
        _bitonic_sort_desc(blk)
        top = blk if top is None else _top_merge(top, blk)
    tau = top[PEER_TOPK - 1]
    mx = a[0] + b[0]
    z = jnp.zeros_like(tau)
    for cnd in cands:
        z = z + jnp.where(cnd >= tau, jnp.exp(cnd - mx), 0.0)
    t_scr[0] = tau
    t_scr[1] = a[0] + jnp.log(z)

    def head2(h, _):
        s1 = s_scr[0, h]
        s2 = s_scr[1, h]
        tau_h = t_scr[0, pl.ds(h, 1), :]
        bs = [pk_scr[1, q, pl.ds(h, 1), :] for q in range(PEER_TOPK)]
        rank_ref[h] = _prefix_count(bs, lambda thr: thr > s2).astype(BF16)
        cnt_ref[h] = _prefix_count(bs, lambda thr: s1 + thr >= tau_h)
        arow_ref[h] = jnp.exp(s1 - t_scr[1, pl.ds(h, 1), :])
        bexp_ref[h] = jnp.exp(s2 - pk_scr[1, 0, pl.ds(h, 1), :]).astype(BF16)
        return 0

    lax.fori_loop(0, PEER_HEADS, head2, 0)


def _peer_select(xt, wq_t, sk, tt):
    d, t = xt.shape
    nq = wq_t.shape[0]
    kern = functools.partial(_select_kernel, tt=tt)
    big = pl.BlockSpec((PEER_HEADS, N_KEYS, tt), lambda i: (0, 0, i))
    shape = lambda dt: jax.ShapeDtypeStruct((PEER_HEADS, N_KEYS, t), dt)
    return pl.pallas_call(
        kern,
        grid=(t // tt,),
        in_specs=[pl.BlockSpec((d, tt), lambda i: (0, i)),
                  pl.BlockSpec((nq, d), lambda i: (0, 0)),
                  pl.BlockSpec(sk.shape, lambda i: (0, 0, 0, 0))],
        out_specs=[big, big, big, big],
        out_shape=[shape(BF16), shape(BF16), shape(F32), shape(F32)],
        scratch_shapes=[pltpu.VMEM((nq, tt), BF16),
                        pltpu.VMEM((2, PEER_HEADS, N_KEYS, tt), F32),
                        pltpu.VMEM((2, PEER_TOPK, PEER_HEADS, tt), F32),
                        pltpu.VMEM((2, PEER_HEADS, tt), F32)],
        compiler_params=_params("parallel"),
        name="peer_select",
    )(xt, wq_t, sk)


def _peer_kernel(xt_ref, x_ref, u_ref, vt_ref, rank_ref, bexp_ref, cnt_ref, arow_ref,
                 g_ref, b_ref, o_ref, acc_scr, row_scr, *bufs, tt, nblk, eb, alpha):
    e = pl.program_id(1)

    @pl.when(e == 0)
    def _():
        acc_scr[...] = jnp.zeros_like(acc_scr)

    acts, ws = bufs[:N_ACT_BUFS], bufs[N_ACT_BUFS:]
    rows_per_blk = eb // N_KEYS
    half = acc_scr.shape[0] // 2

    def mm1(k):
        acts[k % N_ACT_BUFS][...] = jnp.dot(u_ref[k * eb:(k + 1) * eb, :], xt_ref[...],
                                            preferred_element_type=F32)

    def mm2(p, part):
        rows = slice(part * half, (part + 1) * half)
        acc_scr[rows, :] += jnp.dot(vt_ref[p, rows, :], ws[p % N_W_BUFS][...],
                                    preferred_element_type=F32)

    def gates(k):
        act, w = acts[k % N_ACT_BUFS], ws[(k // 2) % N_W_BUFS]
        for ii in range(rows_per_blk):
            i = k * rows_per_blk + ii
            for h in range(PEER_HEADS):
                for a, ref in enumerate((cnt_ref, arow_ref)):
                    row_scr[a, h] = jnp.broadcast_to(ref[h, i:i + 1, :], (GATE_ROWS, tt)).astype(BF16)
            for j0 in range(0, N_KEYS, GATE_ROWS):
                r = ii * N_KEYS + j0
                rw = (k % 2) * eb + r
                gsum = None
                for h in range(PEER_HEADS):
                    sel = rank_ref[h, j0:j0 + GATE_ROWS, :] < row_scr[0, h]
                    term = jnp.where(sel, row_scr[1, h] * bexp_ref[h, j0:j0 + GATE_ROWS, :],
                                     jnp.zeros((), BF16))
                    gsum = term if gsum is None else gsum + term
                w[rw:rw + GATE_ROWS, :] = _gelu_tanh(act[r:r + GATE_ROWS, :].astype(BF16)) * gsum

    for k in range(MM1_AHEAD):
        mm1(k)
    for k in range(nblk):
        if k + MM1_AHEAD < nblk:
            mm1(k + MM1_AHEAD)
        if k >= 2:
            mm2(k // 2 - 1, k % 2)
        gates(k)
    mm2(nblk // 2 - 1, 0)
    mm2(nblk // 2 - 1, 1)

    @pl.when(e == pl.num_programs(1) - 1)
    def _():
        ffn = acc_scr[...].T
        o_ref[...] = _layer_norm(alpha * x_ref[...] + ffn, g_ref[...], b_ref[...])


def _peer_mix(xt, x, u, vt, rank, bexp, cnt, arow, ln_g, ln_b, alpha, tt, te, eb):
    t, d = x.shape
    n_exp = u.shape[0]
    nblk = te // eb
    kern = functools.partial(_peer_kernel, tt=tt, nblk=nblk, eb=eb, alpha=alpha)
    big = pl.BlockSpec((PEER_HEADS, N_KEYS, tt), lambda i, e: (0, 0, i))
    rows = pl.BlockSpec((PEER_HEADS, te // N_KEYS, tt), lambda i, e: (0, e, i))
    vec = pl.BlockSpec((1, d), lambda i, e: (0, 0))
    return pl.pallas_call(
        kern,
        grid=(t // tt, n_exp // te),
        in_specs=[pl.BlockSpec((d, tt), lambda i, e: (0, i)),
                  pl.BlockSpec((tt, d), lambda i, e: (i, 0)),
                  pl.BlockSpec((te, d), lambda i, e: (e, 0)),
                  pl.BlockSpec((nblk // 2, d, 2 * eb), lambda i, e: (e, 0, 0)),
                  big, big, rows, rows, vec, vec],
        out_specs=pl.BlockSpec((tt, d), lambda i, e: (i, 0)),
        out_shape=jax.ShapeDtypeStruct((t, d), F32),
        scratch_shapes=([pltpu.VMEM((d, tt), F32),
                         pltpu.VMEM((2, PEER_HEADS, GATE_ROWS, tt), BF16)]
                        + [pltpu.VMEM((eb, tt), F32)] * N_ACT_BUFS
                        + [pltpu.VMEM((2 * eb, tt), BF16)] * N_W_BUFS),
        compiler_params=_params("parallel", "arbitrary"),
        name="peer_mix_ln",
    )(xt, x, u, vt, rank, bexp, cnt, arow, ln_g, ln_b)


def _tiles(b, s, t):
    return dict(mm_tm=min(2048, t), mm_tn=512, attn_tb=min(256, s), attn_hp=8,
                lru_ts=min(512, s), merge_tm=min(512, t), sel_tt=min(512, t),
                mix_tt=min(512, t), mix_te=2048, mix_eb=256)


def kernel(x, w_in, lambda_qk, subln_g, conv_w, conv_b, gate_a_w, gate_a_b, gate_x_w, gate_x_b,
           lru_lambda, w_br_attn, w_br_lru, w_out, ln1_g, ln1_b, peer_wq, peer_subkeys, peer_u,
           peer_v, ln2_g, ln2_b):
    b, s, d = x.shape
    t = b * s
    depth = w_in.shape[0]
    alpha = (2.0 * depth) ** 0.25
    attn_w = ATTN_HEADS * HEAD_W
    ts = _tiles(b, s, t)
    slopes = jnp.exp2(-8.0 * jnp.arange(1, ATTN_HEADS + 1, dtype=F32) / ATTN_HEADS)
    row = lambda a: a.reshape(1, -1)
    eb = ts["mix_eb"]

    xf = x.reshape(t, d)
    for l in range(depth):
        lam_init = 0.8 - 0.6 * math.exp(-0.3 * l)
        w_l = w_in[l].astype(BF16)
        w_cat = jnp.concatenate([w_l[:, :2 * attn_w], w_l[:, 3 * attn_w:]], axis=1)
        proj = _matmul(xf, w_cat, BF16, ts["mm_tm"], ts["mm_tn"])
        vt = _value_proj_t(xf, w_l[:, 2 * attn_w:3 * attn_w].T, ts["attn_tb"])
        proj3 = proj.reshape(b, s, -1)
        ya = _diff_attention(proj3, vt, slopes, lambda_qk[l], row(subln_g[l]),
                             lam_init, ts["attn_tb"], ts["attn_hp"])
        yr = _rg_lru(proj3, conv_w[l], row(conv_b[l]), gate_a_w[l].astype(BF16),
                     row(gate_a_b[l]), gate_x_w[l].astype(BF16), row(gate_x_b[l]),
                     row(lru_lambda[l]), ts["lru_ts"])
        x1, x1t = _merge_out(ya.reshape(t, attn_w), yr.reshape(t, d), proj, xf,
                             w_br_attn[l].astype(BF16), w_br_lru[l].astype(BF16),
                             w_out[l].astype(BF16), row(ln1_g[l]), row(ln1_b[l]), alpha,
                             ts["merge_tm"])
        rank, bexp, cnt, arow = _peer_select(x1t, peer_wq[l].T.astype(BF16),
                                             peer_subkeys[l].astype(BF16), ts["sel_tt"])
        v_blocks = peer_v[l].astype(BF16).reshape(-1, 2 * eb, d).transpose(0, 2, 1)
        xf = _peer_mix(x1t, x1, peer_u[l].astype(BF16), v_blocks, rank, bexp, cnt, arow,
                       row(ln2_g[l]), row(ln2_b[l]), alpha, ts["mix_tt"], ts["mix_te"], eb)
    return xf.reshape(b, s, d)
```

```python
import functools
import math

import jax
import jax.numpy as jnp
from jax import lax
from jax.experimental import pallas as pl
from jax.experimental.pallas import tpu as pltpu

F32 = jnp.float32
BF16 = jnp.bfloat16

ATTN_HEADS = 8
HEAD_DIM = 64
HEAD_W = 2 * HEAD_DIM
LRU_BLOCKS = 8
LRU_C = 8.0
CONV_W = 4
PEER_HEADS = 8
N_KEYS = 128
PEER_TOPK = 16
LN_EPS = 1e-5
RMS_EPS = 1e-6
PROJ_LRU_COL = 2
PROJ_GATE_COL = 4
SUBLANES = 8
PACKED_ROWS = 16
V_AUG_ROWS = PACKED_ROWS
GATE_ROWS = PACKED_ROWS
MM1_AHEAD = 2
N_ACT_BUFS = MM1_AHEAD + 2
N_W_BUFS = 3
VMEM_LIMIT = 56 * 1024 * 1024


def _params(*sem):
    return pltpu.CompilerParams(dimension_semantics=sem, vmem_limit_bytes=VMEM_LIMIT)


MM_ROWS = 256


def _mm_kernel(x_ref, w_ref, o_ref):
    for r in range(0, x_ref.shape[0], MM_ROWS):
        o_ref[r:r + MM_ROWS, :] = jnp.dot(x_ref[r:r + MM_ROWS, :].astype(BF16), w_ref[...],
                                          preferred_element_type=F32).astype(o_ref.dtype)


def _matmul(x, w, out_dtype, tm, tn):
    t, k = x.shape
    n = w.shape[1]
    return pl.pallas_call(
        _mm_kernel,
        grid=(t // tm, n // tn),
        in_specs=[pl.BlockSpec((tm, k), lambda i, j: (i, 0)),
                  pl.BlockSpec((k, tn), lambda i, j: (0, j))],
        out_specs=pl.BlockSpec((tm, tn), lambda i, j: (i, j)),
        out_shape=jax.ShapeDtypeStruct((t, n), out_dtype),
        compiler_params=_params("parallel", "arbitrary"),
        name="in_proj",
    )(x, w)


def _vt_kernel(x_ref, w_ref, o_ref):
    tk = o_ref.shape[3]
    ones_row = lax.broadcasted_iota(jnp.int32, (ATTN_HEADS, V_AUG_ROWS, tk), 1) == 0
    extra = jnp.where(ones_row, 1.0, 0.0).astype(o_ref.dtype)
    for blk in range(o_ref.shape[1]):
        xt = x_ref[blk * tk:(blk + 1) * tk, :].T.astype(BF16)
        vt = jnp.dot(w_ref[...], xt, preferred_element_type=F32)
        vt = vt.astype(o_ref.dtype).reshape(ATTN_HEADS, HEAD_W, tk)
        o_ref[:, blk] = jnp.concatenate([vt, extra], axis=1)


def _value_proj_t(x, w_t, tk, blocks_per_step):
    t, d = x.shape
    rows = HEAD_W + V_AUG_ROWS
    nb = math.gcd(blocks_per_step, t // tk)
    return pl.pallas_call(
        _vt_kernel,
        grid=(t // (tk * nb),),
        in_specs=[pl.BlockSpec((tk * nb, d), lambda i: (i, 0)),
                  pl.BlockSpec(w_t.shape, lambda i: (0, 0))],
        out_specs=pl.BlockSpec((ATTN_HEADS, nb, rows, tk), lambda i: (0, i, 0, 0)),
        out_shape=jax.ShapeDtypeStruct((ATTN_HEADS, t // tk, rows, tk), BF16),
        compiler_params=_params("parallel"),
        name="value_proj_t",
    )(x, w_t)


def _attn_kernel(slopes_ref, lqk_ref, g_ref, q_ref, k_ref, vt_ref, o_ref, *scr, tb, hp, lam_init):
    acc_scr, st_scr, ka_scr = scr[:hp], scr[hp:2 * hp], scr[2 * hp:]
    hg = pl.program_id(1)
    qi = pl.program_id(2)
    s = k_ref.shape[1]
    lq = lqk_ref[...]
    lam = (jnp.exp(jnp.sum(lq[0:1] * lq[1:2], axis=1, keepdims=True))
           - jnp.exp(jnp.sum(lq[2:3] * lq[3:4], axis=1, keepdims=True)) + lam_init)
    slopes = [slopes_ref[hg * hp + hh] for hh in range(hp)]

    @pl.when(qi == 0)
    def _():
        lane = lax.broadcasted_iota(jnp.int32, (s, HEAD_W), 1)
        c = (lax.broadcasted_iota(jnp.int32, (s, HEAD_W), 0) & (tb - 1)).astype(F32)
        for hh in range(hp):
            extra = jnp.where(lane == 0, 1.0, jnp.where(lane == 1, slopes[hh] * c, 0.0))
            ka_scr[hh][:, :HEAD_W] = k_ref[0, :, hh * HEAD_W:(hh + 1) * HEAD_W]
            ka_scr[hh][:, HEAD_W:] = extra.astype(BF16)

    lane = lax.broadcasted_iota(jnp.int32, (tb, HEAD_W), 1)
    lane2 = lax.broadcasted_iota(jnp.int32, (2 * tb, HEAD_W), 1)
    r = (lax.broadcasted_iota(jnp.int32, (2 * tb, HEAD_W), 0) & (tb - 1)).astype(F32)
    masked = ((lax.broadcasted_iota(jnp.int32, (tb, 2 * tb), 1) & (tb - 1))
              < lax.broadcasted_iota(jnp.int32, (tb, 2 * tb), 0))
    qqs = []
    for hh in range(hp):
        qs = q_ref[0, :, hh * HEAD_W:(hh + 1) * HEAD_W].astype(F32) * (HEAD_DIM ** -0.5)
        zero = jnp.zeros_like(qs)
        qq = jnp.concatenate([jnp.where(lane < HEAD_DIM, qs, zero),
                              jnp.where(lane >= HEAD_DIM, qs, zero)], axis=0)
        extra = jnp.where(lane2 == 0, -slopes[hh] * r, jnp.where(lane2 == 1, 1.0, 0.0))
        qqs.append(jnp.concatenate([qq, extra], axis=1).T.astype(BF16))
        acc_scr[hh][...] = jnp.zeros_like(acc_scr[hh])

    def scores(j, hh):
        start = pl.multiple_of(j * tb, tb)
        st_scr[hh][...] = jnp.dot(ka_scr[hh][pl.ds(start, tb), :], qqs[hh],
                                  preferred_element_type=F32)

    def step(j, ms, last):
        shift = (qi - j) * tb
        out = []
        for hh in range(hp):
            off = slopes[hh] * shift.astype(F32)
            st = st_scr[hh][...]
            if last:
                st = jnp.where(masked, -jnp.inf, st)
            m_new = jnp.maximum(ms[hh], jnp.max(st, axis=0, keepdims=True) - off)
            pt = jnp.exp(st - (m_new + off)).astype(BF16)
            a = jnp.exp(ms[hh] - m_new)
            if not last:
                scores(j + 1, hh)
            acc_scr[hh][...] = a * acc_scr[hh][...] + jnp.dot(vt_ref[hh, j], pt,
                                                              preferred_element_type=F32)
            out.append(m_new)
        return tuple(out)

    for hh in range(hp):
        scores(0, hh)
    ms = tuple(jnp.full((1, 2 * tb), -jnp.inf, F32) for _ in range(hp))
    ms = lax.fori_loop(0, qi, functools.partial(step, last=False), ms)
    step(qi, ms, last=True)

    for hh in range(hp):
        acc = acc_scr[hh][...]
        ot = acc[:HEAD_W] * (1.0 / acc[HEAD_W:HEAD_W + 1])
        o = (ot[:, :tb] - lam * ot[:, tb:]).T
        o = o * lax.rsqrt(jnp.mean(o * o, axis=-1, keepdims=True) + RMS_EPS)
        o = o * g_ref[...] * (1.0 - lam_init)
        o_ref[0, :, hh * HEAD_W:(hh + 1) * HEAD_W] = o.astype(o_ref.dtype)


def _diff_attention(qk, vt, slopes, lambda_qk, subln_g, lam_init, tb, hp):
    b, s, _ = qk.shape
    assert tb & (tb - 1) == 0 and tb <= 256
    kern = functools.partial(_attn_kernel, tb=tb, hp=hp, lam_init=lam_init)
    gw = hp * HEAD_W
    vrows = vt.shape[2]
    return pl.pallas_call(
        kern,
        grid=(b, ATTN_HEADS // hp, s // tb),
        in_specs=[pl.BlockSpec(memory_space=pltpu.SMEM),
                  pl.BlockSpec((4, HEAD_DIM), lambda b_, h, i: (0, 0)),
                  pl.BlockSpec((1, HEAD_W), lambda b_, h, i: (0, 0)),
                  pl.BlockSpec((1, tb, gw), lambda b_, h, i: (b_, i, h)),
                  pl.BlockSpec((1, s, gw), lambda b_, h, i: (b_, 0, ATTN_HEADS // hp + h),
                               pipeline_mode=pl.Buffered(1)),
                  pl.BlockSpec((hp, s // tb, vrows, tb), lambda b_, h, i: (h, b_, 0, 0),
                               pipeline_mode=pl.Buffered(1))],
        out_specs=pl.BlockSpec((1, tb, gw), lambda b_, h, i: (b_, i, h)),
        out_shape=jax.ShapeDtypeStruct((b, s, ATTN_HEADS * HEAD_W), BF16),
        scratch_shapes=([pltpu.VMEM((vrows, 2 * tb), F32)] * hp
                        + [pltpu.VMEM((tb, 2 * tb), F32)] * hp
                        + [pltpu.VMEM((s, 2 * HEAD_W), BF16)] * hp),
        compiler_params=_params("parallel", "parallel", "arbitrary"),
        name="diff_attn",
    )(slopes, lambda_qk, subln_g, qk, qk, vt)


def _gelu(x):
    return jax.nn.gelu(x)


def _gelu_tanh(x):
    c = math.sqrt(2.0 / math.pi)
    hx = 0.5 * x
    return hx + hx * jnp.tanh(x * (c + (c * 0.044715) * (x * x)))


def _lru_kernel(xr_ref, gr_ref, cw_ref, cb_ref, gaw_ref, gab_ref, gxw_ref, gxb_ref, lam_ref,
                o_ref, tail_scr, h_scr, a_scr, u_scr, gg_scr, *, ts):
    si = pl.program_id(1)

    @pl.when(si == 0)
    def _():
        tail_scr[...] = jnp.zeros_like(tail_scr)
        h_scr[...] = jnp.zeros_like(h_scr)

    x = xr_ref[0].astype(F32)
    w = x.shape[1]
    tail = tail_scr[...]
    row8 = lax.broadcasted_iota(jnp.int32, (SUBLANES, w), 0)
    cw = cw_ref[...]
    xb = cb_ref[...]
    for d in range(CONV_W - 1, 0, -1):
        xs = pltpu.roll(x, d, 0)
        top = jnp.where(row8 < d, pltpu.roll(tail, d, 0), xs[:SUBLANES])
        xs = jnp.concatenate([top, xs[SUBLANES:]], axis=0)
        xb = xb + xs * cw[CONV_W - 1 - d:CONV_W - d]
    xb = xb + x * cw[CONV_W - 1:CONV_W]
    tail_scr[...] = x[ts - SUBLANES:]

    xbb = xb.astype(BF16)
    bw = w // LRU_BLOCKS

    def gate(w_ref, b_ref):
        parts = [jnp.dot(xbb[:, g * bw:(g + 1) * bw], w_ref[g], preferred_element_type=F32)
                 for g in range(LRU_BLOCKS)]
        return jax.nn.sigmoid(jnp.concatenate(parts, axis=1) + b_ref[...])

    r = gate(gaw_ref, gab_ref)
    i = gate(gxw_ref, gxb_ref)
    lam = lam_ref[...]
    softplus_neg = jnp.maximum(-lam, 0.0) + jnp.log1p(jnp.exp(-jnp.abs(lam)))
    log_a = -LRU_C * r * softplus_neg
    a = jnp.exp(log_a)
    mult = jnp.sqrt(1.0 - jnp.exp(2.0 * log_a))
    first = jnp.where(jnp.logical_and(row8 == 0, si == 0), 1.0, mult[:SUBLANES])
    mult = jnp.concatenate([first, mult[SUBLANES:]], axis=0)
    a_scr[...] = a
    u_scr[...] = mult * (i * xb)
    gg_scr[...] = _gelu(gr_ref[0].astype(F32))

    def group(g, hprev):
        r0 = pl.multiple_of(g * SUBLANES, SUBLANES)
        ag = a_scr[pl.ds(r0, SUBLANES), :]
        ug = u_scr[pl.ds(r0, SUBLANES), :]
        for d in (1, 2, 4):
            keep = row8 >= d
            ug = jnp.where(keep, ag * pltpu.roll(ug, d, 0) + ug, ug)
            ag = jnp.where(keep, ag * pltpu.roll(ag, d, 0), ag)
        hg = ag * hprev + ug
        o_ref[0, pl.ds(r0, SUBLANES), :] = (hg * gg_scr[pl.ds(r0, SUBLANES), :]).astype(o_ref.dtype)
        return jnp.broadcast_to(hg[SUBLANES - 1:SUBLANES, :], (SUBLANES, w))

    h_scr[...] = lax.fori_loop(0, ts // SUBLANES, group, h_scr[...])


def _rg_lru(rest, conv_w, conv_b, gaw, gab, gxw, gxb, lru_lambda, ts):
    b, s, _ = rest.shape
    w = conv_w.shape[1]
    kern = functools.partial(_lru_kernel, ts=ts)
    vec = pl.BlockSpec((1, w), lambda b_, i: (0, 0))
    blk = pl.BlockSpec(gaw.shape, lambda b_, i: (0, 0, 0))
    return pl.pallas_call(
        kern,
        grid=(b, s // ts),
        in_specs=[pl.BlockSpec((1, ts, w), lambda b_, i: (b_, i, PROJ_LRU_COL)),
                  pl.BlockSpec((1, ts, w), lambda b_, i: (b_, i, PROJ_LRU_COL + 1)),
                  pl.BlockSpec((CONV_W, w), lambda b_, i: (0, 0)),
                  vec, blk, vec, blk, vec, vec],
        out_specs=pl.BlockSpec((1, ts, w), lambda b_, i: (b_, i, 0)),
        out_shape=jax.ShapeDtypeStruct((b, s, w), BF16),
        scratch_shapes=[pltpu.VMEM((SUBLANES, w), F32), pltpu.VMEM((SUBLANES, w), F32),
                        pltpu.VMEM((ts, w), F32), pltpu.VMEM((ts, w), F32),
                        pltpu.VMEM((ts, w), F32)],
        compiler_params=_params("parallel", "arbitrary"),
        name="rg_lru",
    )(rest, rest, conv_w, conv_b, gaw, gab, gxw, gxb, lru_lambda)


def _layer_norm(z, g, b):
    mu = jnp.mean(z, axis=-1, keepdims=True)
    zc = z - mu
    var = jnp.mean(zc * zc, axis=-1, keepdims=True)
    return zc * lax.rsqrt(var + LN_EPS) * g + b


def _merge_kernel(ya_ref, yr_ref, ga_ref, gl_ref, x_ref, wa_ref, wl_ref, wo_ref, g_ref, b_ref,
                  o_ref, ob_ref, *, alpha):
    pa = jnp.dot(ya_ref[...], wa_ref[...], preferred_element_type=F32)
    pr = jnp.dot(yr_ref[...], wl_ref[...], preferred_element_type=F32)
    merged = (jax.nn.sigmoid(ga_ref[...].astype(F32)) * pa
              + jax.nn.sigmoid(gl_ref[...].astype(F32)) * pr)
    mix = jnp.dot(merged.astype(BF16), wo_ref[...], preferred_element_type=F32)
    y = _layer_norm(alpha * x_ref[...] + mix, g_ref[...], b_ref[...])
    o_ref[...] = y
    ob_ref[...] = y.T.astype(BF16)


def _merge_out(ya, yr, rest, x, wa, wl, wo, ln_g, ln_b, alpha, tm):
    t, d = x.shape
    kern = functools.partial(_merge_kernel, alpha=alpha)
    rows = lambda c: pl.BlockSpec((tm, d), lambda i: (i, c))
    full = pl.BlockSpec((d, d), lambda i: (0, 0))
    vec = pl.BlockSpec((1, d), lambda i: (0, 0))
    return pl.pallas_call(
        kern,
        grid=(t // tm,),
        in_specs=[rows(0), rows(0), rows(PROJ_GATE_COL), rows(PROJ_GATE_COL + 1), rows(0),
                  full, full, full, vec, vec],
        out_specs=[rows(0), pl.BlockSpec((d, tm), lambda i: (0, i))],
        out_shape=[jax.ShapeDtypeStruct((t, d), F32), jax.ShapeDtypeStruct((d, t), BF16)],
        compiler_params=_params("parallel"),
        name="merge_out_ln",
    )(ya, yr, rest, rest, x, wa, wl, wo, ln_g, ln_b)


def _cmpx(v, i, j):
    hi = jnp.maximum(v[i], v[j])
    v[j] = jnp.minimum(v[i], v[j])
    v[i] = hi


def _bitonic_merge_desc(v):
    n = len(v)
    j = n // 2
    while j >= 1:
        for i in range(n):
            if i & j == 0:
                _cmpx(v, i, i + j)
        j //= 2


def _bitonic_sort_desc(v):
    n = len(v)
    k = 2
    while k <= n:
        j = k // 2
        while j >= 1:
            for i in range(n):
                l = i ^ j
                if l > i:
                    if i & k == 0:
                        _cmpx(v, i, l)
                    else:
                        _cmpx(v, l, i)
            j //= 2
        k *= 2


def _top_merge(a, b):
    n = len(a)
    v = [jnp.maximum(a[i], b[n - 1 - i]) for i in range(n)]
    _bitonic_merge_desc(v)
    return v


def _prefix_count(b, test):
    assert len(b) == 16
    c8 = test(b[7])
    n = jnp.where(c8, 8.0, 0.0)
    c4 = test(jnp.where(c8, b[11], b[3]))
    n = n + jnp.where(c4, 4.0, 0.0)
    c2 = test(jnp.where(c8, jnp.where(c4, b[13], b[9]), jnp.where(c4, b[5], b[1])))
    n = n + jnp.where(c2, 2.0, 0.0)
    lo = jnp.where(c4, jnp.where(c2, b[6], b[4]), jnp.where(c2, b[2], b[0]))
    hi = jnp.where(c4, jnp.where(c2, b[14], b[12]), jnp.where(c2, b[10], b[8]))
    n = n + jnp.where(test(jnp.where(c8, hi, lo)), 1.0, 0.0)
    return jnp.where(test(b[15]), 16.0, n)


_CAND_PAIRS = [(p, q) for p in range(PEER_TOPK) for q in range(PEER_TOPK)
               if (p + 1) * (q + 1) <= PEER_TOPK]


def _select_kernel(x_ref, wq_ref, sk_ref, rank_ref, bexp_ref, cnt_ref, arow_ref,
                   q_scr, s_scr, pk_scr, t_scr, *, tt):
    qt = jnp.dot(wq_ref[...], x_ref[...], preferred_element_type=F32)
    q_scr[...] = qt.astype(BF16)
    dk = sk_ref.shape[3]
    n_grp = N_KEYS // SUBLANES

    def head(h, _):
        for c in range(2):
            r0 = pl.multiple_of((h * 2 + c) * dk, dk)
            sc = jnp.dot(sk_ref[h, c], q_scr[pl.ds(r0, dk), :], preferred_element_type=F32)
            s_scr[c, h] = sc
            v = [sc[g * SUBLANES:(g + 1) * SUBLANES, :] for g in range(n_grp)]
            _bitonic_sort_desc(v)
            for sh in (4, 2, 1):
                v = _top_merge(v, [pltpu.roll(e, sh, 0) for e in v])
            for p in range(PEER_TOPK):
                pk_scr[c, p, pl.ds(h, 1), :] = v[p][0:1, :]
        return 0

    lax.fori_loop(0, PEER_HEADS, head, 0)

    a = [pk_scr[0, p] for p in range(PEER_TOPK)]
    b = [pk_scr[1, p] for p in range(PEER_TOPK)]
    cands = [a[p] + b[q] for p, q in _CAND_PAIRS]
    neg = jnp.full_like(cands[0], -jnp.inf)
    blocks = cands + [neg] * ((-len(cands)) % PEER_TOPK)
    top = None
    for i0 in range(0, len(blocks), PEER_TOPK):
        blk = blocks[i0:i0 + PEER_TOPK]
        _bitonic_sort_desc(blk)
        top = blk if top is None else _top_merge(top, blk)
    tau = top[PEER_TOPK - 1]
    mx = a[0] + b[0]
    z = jnp.zeros_like(tau)
    for cnd in cands:
        z = z + jnp.where(cnd >= tau, jnp.exp(cnd - mx), 0.0)
    t_scr[0] = tau
    t_scr[1] = a[0] + jnp.log(z)

    def head2(h, _):
        s1 = s_scr[0, h]
        s2 = s_scr[1, h]
        tau_h = t_scr[0, pl.ds(h, 1), :]
        bs = [pk_scr[1, q, pl.ds(h, 1), :] for q in range(PEER_TOPK)]
        rank_ref[h] = _prefix_count(bs, lambda thr: thr > s2).astype(BF16)
        cnt_ref[h] = _prefix_count(bs, lambda thr: s1 + thr >= tau_h)
        arow_ref[h] = jnp.exp(s1 - t_scr[1, pl.ds(h, 1), :])
        bexp_ref[h] = jnp.exp(s2 - pk_scr[1, 0, pl.ds(h, 1), :]).astype(BF16)
        return 0

    lax.fori_loop(0, PEER_HEADS, head2, 0)


def _peer_select(xt, wq_t, sk, tt):
    d, t = xt.shape
    nq = wq_t.shape[0]
    kern = functools.partial(_select_kernel, tt=tt)
    big = pl.BlockSpec((PEER_HEADS, N_KEYS, tt), lambda i: (0, 0, i))
    shape = lambda dt: jax.ShapeDtypeStruct((PEER_HEADS, N_KEYS, t), dt)
    return pl.pallas_call(
        kern,
        grid=(t // tt,),
        in_specs=[pl.BlockSpec((d, tt), lambda i: (0, i)),
                  pl.BlockSpec((nq, d), lambda i: (0, 0)),
                  pl.BlockSpec(sk.shape, lambda i: (0, 0, 0, 0))],
        out_specs=[big, big, big, big],
        out_shape=[shape(BF16), shape(BF16), shape(F32), shape(F32)],
        scratch_shapes=[pltpu.VMEM((nq, tt), BF16),
                        pltpu.VMEM((2, PEER_HEADS, N_KEYS, tt), F32),
                        pltpu.VMEM((2, PEER_TOPK, PEER_HEADS, tt), F32),
                        pltpu.VMEM((2, PEER_HEADS, tt), F32)],
        compiler_params=_params("parallel"),
        name="peer_select",
    )(xt, wq_t, sk)


def _peer_kernel(xt_ref, x_ref, u_ref, vt_ref, rank_ref, bexp_ref, cnt_ref, arow_ref,
                 g_ref, b_ref, o_ref, acc_scr, row_scr, *bufs, tt, nblk, eb, alpha):
    e = pl.program_id(1)

    @pl.when(e == 0)
    def _():
        acc_scr[...] = jnp.zeros_like(acc_scr)

    acts, ws = bufs[:N_ACT_BUFS], bufs[N_ACT_BUFS:]
    rows_per_blk = eb // N_KEYS
    half = acc_scr.shape[0] // 2

    def mm1(k):
        acts[k % N_ACT_BUFS][...] = jnp.dot(u_ref[k * eb:(k + 1) * eb, :], xt_ref[...],
                                            preferred_element_type=F32)

    def mm2(p, part):
        rows = slice(part * half, (part + 1) * half)
        acc_scr[rows, :] += jnp.dot(vt_ref[p, rows, :], ws[p % N_W_BUFS][...],
                                    preferred_element_type=F32)

    def gates(k):
        act, w = acts[k % N_ACT_BUFS], ws[(k // 2) % N_W_BUFS]
        for ii in range(rows_per_blk):
            i = k * rows_per_blk + ii
            for h in range(PEER_HEADS):
                for a, ref in enumerate((cnt_ref, arow_ref)):
                    row_scr[a, h] = jnp.broadcast_to(ref[h, i:i + 1, :], (GATE_ROWS, tt)).astype(BF16)
            for j0 in range(0, N_KEYS, GATE_ROWS):
                r = ii * N_KEYS + j0
                rw = (k % 2) * eb + r
                gsum = None
                for h in range(PEER_HEADS):
                    sel = rank_ref[h, j0:j0 + GATE_ROWS, :] < row_scr[0, h]
                    term = jnp.where(sel, row_scr[1, h] * bexp_ref[h, j0:j0 + GATE_ROWS, :],
                                     jnp.zeros((), BF16))
                    gsum = term if gsum is None else gsum + term
                w[rw:rw + GATE_ROWS, :] = _gelu_tanh(act[r:r + GATE_ROWS, :].astype(BF16)) * gsum

    for k in range(MM1_AHEAD):
        mm1(k)
    for k in range(nblk):
        if k + MM1_AHEAD < nblk:
            mm1(k + MM1_AHEAD)
        if k >= 2:
            mm2(k // 2 - 1, k % 2)
        gates(k)
    mm2(nblk // 2 - 1, 0)
    mm2(nblk // 2 - 1, 1)

    @pl.when(e == pl.num_programs(1) - 1)
    def _():
        ffn = acc_scr[...].T
        o_ref[...] = _layer_norm(alpha * x_ref[...] + ffn, g_ref[...], b_ref[...])


def _peer_mix(xt, x, u, vt, rank, bexp, cnt, arow, ln_g, ln_b, alpha, tt, te, eb):
    t, d = x.shape
    n_exp = u.shape[0]
    nblk = te // eb
    kern = functools.partial(_peer_kernel, tt=tt, nblk=nblk, eb=eb, alpha=alpha)
    big = pl.BlockSpec((PEER_HEADS, N_KEYS, tt), lambda i, e: (0, 0, i))
    rows = pl.BlockSpec((PEER_HEADS, te // N_KEYS, tt), lambda i, e: (0, e, i))
    vec = pl.BlockSpec((1, d), lambda i, e: (0, 0))
    return pl.pallas_call(
        kern,
        grid=(t // tt, n_exp // te),
        in_specs=[pl.BlockSpec((d, tt), lambda i, e: (0, i)),
                  pl.BlockSpec((tt, d), lambda i, e: (i, 0)),
                  pl.BlockSpec((te, d), lambda i, e: (e, 0)),
                  pl.BlockSpec((nblk // 2, d, 2 * eb), lambda i, e: (e, 0, 0)),
                  big, big, rows, rows, vec, vec],
        out_specs=pl.BlockSpec((tt, d), lambda i, e: (i, 0)),
        out_shape=jax.ShapeDtypeStruct((t, d), F32),
        scratch_shapes=([pltpu.VMEM((d, tt), F32),
                         pltpu.VMEM((2, PEER_HEADS, GATE_ROWS, tt), BF16)]
                        + [pltpu.VMEM((eb, tt), F32)] * N_ACT_BUFS
                        + [pltpu.VMEM((2 * eb, tt), BF16)] * N_W_BUFS),
        compiler_params=_params("parallel", "arbitrary"),
        name="peer_mix_ln",
    )(xt, x, u, vt, rank, bexp, cnt, arow, ln_g, ln_b)


def _tiles(b, s, t):
    return dict(mm_tm=min(2048, t), mm_tn=512, attn_tb=min(256, s), attn_hp=8, vt_blocks=4,
                lru_ts=min(512, s), merge_tm=min(512, t), sel_tt=min(512, t),
                mix_tt=min(512, t), mix_te=2048, mix_eb=256)


def kernel(x, w_in, lambda_qk, subln_g, conv_w, conv_b, gate_a_w, gate_a_b, gate_x_w, gate_x_b,
           lru_lambda, w_br_attn, w_br_lru, w_out, ln1_g, ln1_b, peer_wq, peer_subkeys, peer_u,
           peer_v, ln2_g, ln2_b):
    b, s, d = x.shape
    t = b * s
    depth = w_in.shape[0]
    alpha = (2.0 * depth) ** 0.25
    attn_w = ATTN_HEADS * HEAD_W
    ts = _tiles(b, s, t)
    slopes = jnp.exp2(-8.0 * jnp.arange(1, ATTN_HEADS + 1, dtype=F32) / ATTN_HEADS)
    row = lambda a: a.reshape(1, -1)
    eb = ts["mix_eb"]

    xf = x.reshape(t, d)
    for l in range(depth):
        lam_init = 0.8 - 0.6 * math.exp(-0.3 * l)
        w_l = w_in[l].astype(BF16)
        w_cat = jnp.concatenate([w_l[:, :2 * attn_w], w_l[:, 3 * attn_w:]], axis=1)
        proj = _matmul(xf, w_cat, BF16, ts["mm_tm"], ts["mm_tn"])
        vt = _value_proj_t(xf, w_l[:, 2 * attn_w:3 * attn_w].T, ts["attn_tb"], ts["vt_blocks"])
        proj3 = proj.reshape(b, s, -1)
        ya = _diff_attention(proj3, vt, slopes, lambda_qk[l], row(subln_g[l]),
                             lam_init, ts["attn_tb"], ts["attn_hp"])
        yr = _rg_lru(proj3, conv_w[l], row(conv_b[l]), gate_a_w[l].astype(BF16),
                     row(gate_a_b[l]), gate_x_w[l].astype(BF16), row(gate_x_b[l]),
                     row(lru_lambda[l]), ts["lru_ts"])
        x1, x1t = _merge_out(ya.reshape(t, attn_w), yr.reshape(t, d), proj, xf,
                             w_br_attn[l].astype(BF16), w_br_lru[l].astype(BF16),
                             w_out[l].astype(BF16), row(ln1_g[l]), row(ln1_b[l]), alpha,
                             ts["merge_tm"])
        rank, bexp, cnt, arow = _peer_select(x1t, peer_wq[l].T.astype(BF16),
                                             peer_subkeys[l].astype(BF16), ts["sel_tt"])
        v_blocks = peer_v[l].astype(BF16).reshape(-1, 2 * eb, d).transpose(0, 2, 1)
        xf = _peer_mix(x1t, x1, peer_u[l].astype(BF16), v_blocks, rank, bexp, cnt, arow,
                       row(ln2_g[l]), row(ln2_b[l]), alpha, ts["mix_tt"], ts["mix_te"], eb)
    return xf.reshape(b, s, d)
```

```python
import functools
import math

import jax
import jax.numpy as jnp
from jax import lax
from jax.experimental import pallas as pl
from jax.experimental.pallas import tpu as pltpu

F32 = jnp.float32
BF16 = jnp.bfloat16

ATTN_HEADS = 8
HEAD_DIM = 64
HEAD_W = 2 * HEAD_DIM
LRU_BLOCKS = 8
LRU_C = 8.0
CONV_W = 4
PEER_HEADS = 8
N_KEYS = 128
PEER_TOPK = 16
LN_EPS = 1e-5
RMS_EPS = 1e-6
PROJ_LRU_COL = 2
PROJ_GATE_COL = 4
SUBLANES = 8
PACKED_ROWS = 16
V_AUG_ROWS = PACKED_ROWS
GATE_ROWS = PACKED_ROWS
MM1_AHEAD = 2
N_ACT_BUFS = MM1_AHEAD + 2
N_W_BUFS = 3
VMEM_LIMIT = 58 * 1024 * 1024


def _params(*sem):
    return pltpu.CompilerParams(dimension_semantics=sem, vmem_limit_bytes=VMEM_LIMIT)


MM_ROWS = 256


def _mm_kernel(x_ref, w_ref, o_ref):
    for r in range(0, x_ref.shape[0], MM_ROWS):
        o_ref[r:r + MM_ROWS, :] = jnp.dot(x_ref[r:r + MM_ROWS, :].astype(BF16), w_ref[...],
                                          preferred_element_type=F32).astype(o_ref.dtype)


def _matmul(x, w, out_dtype, tm, tn):
    t, k = x.shape
    n = w.shape[1]
    return pl.pallas_call(
        _mm_kernel,
        grid=(t // tm, n // tn),
        in_specs=[pl.BlockSpec((tm, k), lambda i, j: (i, 0)),
                  pl.BlockSpec((k, tn), lambda i, j: (0, j))],
        out_specs=pl.BlockSpec((tm, tn), lambda i, j: (i, j)),
        out_shape=jax.ShapeDtypeStruct((t, n), out_dtype),
        compiler_params=_params("parallel", "arbitrary"),
        name="in_proj",
    )(x, w)


def _vt_kernel(x_ref, w_ref, o_ref):
    tk = o_ref.shape[3]
    ones_row = lax.broadcasted_iota(jnp.int32, (ATTN_HEADS, V_AUG_ROWS, tk), 1) == 0
    extra = jnp.where(ones_row, 1.0, 0.0).astype(o_ref.dtype)
    for blk in range(o_ref.shape[1]):
        xt = x_ref[blk * tk:(blk + 1) * tk, :].T.astype(BF16)
        vt = jnp.dot(w_ref[...], xt, preferred_element_type=F32)
        vt = vt.astype(o_ref.dtype).reshape(ATTN_HEADS, HEAD_W, tk)
        o_ref[:, blk] = jnp.concatenate([vt, extra], axis=1)


def _value_proj_t(x, w_t, tk, blocks_per_step):
    t, d = x.shape
    rows = HEAD_W + V_AUG_ROWS
    nb = math.gcd(blocks_per_step, t // tk)
    return pl.pallas_call(
        _vt_kernel,
        grid=(t // (tk * nb),),
        in_specs=[pl.BlockSpec((tk * nb, d), lambda i: (i, 0)),
                  pl.BlockSpec(w_t.shape, lambda i: (0, 0))],
        out_specs=pl.BlockSpec((ATTN_HEADS, nb, rows, tk), lambda i: (0, i, 0, 0)),
        out_shape=jax.ShapeDtypeStruct((ATTN_HEADS, t // tk, rows, tk), BF16),
        compiler_params=_params("parallel"),
        name="value_proj_t",
    )(x, w_t)


def _attn_kernel(slopes_ref, lqk_ref, g_ref, q_ref, k_ref, vt_ref, o_ref, *scr, tb, hp, lam_init):
    acc_scr, st_scr, ka_scr = scr[:hp], scr[hp:2 * hp], scr[2 * hp:]
    hg = pl.program_id(1)
    qi = pl.program_id(2)
    s = k_ref.shape[1]
    lq = lqk_ref[...]
    lam = (jnp.exp(jnp.sum(lq[0:1] * lq[1:2], axis=1, keepdims=True))
           - jnp.exp(jnp.sum(lq[2:3] * lq[3:4], axis=1, keepdims=True)) + lam_init)
    slopes = [slopes_ref[hg * hp + hh] for hh in range(hp)]

    @pl.when(qi == 0)
    def _():
        lane = lax.broadcasted_iota(jnp.int32, (s, HEAD_W), 1)
        c = (lax.broadcasted_iota(jnp.int32, (s, HEAD_W), 0) & (tb - 1)).astype(F32)
        for hh in range(hp):
            extra = jnp.where(lane == 0, 1.0, jnp.where(lane == 1, slopes[hh] * c, 0.0))
            ka_scr[hh][:, :HEAD_W] = k_ref[0, :, hh * HEAD_W:(hh + 1) * HEAD_W]
            ka_scr[hh][:, HEAD_W:] = extra.astype(BF16)

    lane = lax.broadcasted_iota(jnp.int32, (tb, HEAD_W), 1)
    lane2 = lax.broadcasted_iota(jnp.int32, (2 * tb, HEAD_W), 1)
    r = (lax.broadcasted_iota(jnp.int32, (2 * tb, HEAD_W), 0) & (tb - 1)).astype(F32)
    masked = ((lax.broadcasted_iota(jnp.int32, (tb, 2 * tb), 1) & (tb - 1))
              < lax.broadcasted_iota(jnp.int32, (tb, 2 * tb), 0))
    qqs = []
    for hh in range(hp):
        qs = q_ref[0, :, hh * HEAD_W:(hh + 1) * HEAD_W].astype(F32) * (HEAD_DIM ** -0.5)
        zero = jnp.zeros_like(qs)
        qq = jnp.concatenate([jnp.where(lane < HEAD_DIM, qs, zero),
                              jnp.where(lane >= HEAD_DIM, qs, zero)], axis=0)
        extra = jnp.where(lane2 == 0, -slopes[hh] * r, jnp.where(lane2 == 1, 1.0, 0.0))
        qqs.append(jnp.concatenate([qq, extra], axis=1).T.astype(BF16))
        acc_scr[hh][...] = jnp.zeros_like(acc_scr[hh])

    def scores(j, hh):
        start = pl.multiple_of(j * tb, tb)
        st_scr[hh][...] = jnp.dot(ka_scr[hh][pl.ds(start, tb), :], qqs[hh],
                                  preferred_element_type=F32)

    def step(j, ms, last):
        shift = (qi - j) * tb
        out = []
        for hh in range(hp):
            off = slopes[hh] * shift.astype(F32)
            st = st_scr[hh][...]
            if last:
                st = jnp.where(masked, -jnp.inf, st)
            m_new = jnp.maximum(ms[hh], jnp.max(st, axis=0, keepdims=True) - off)
            pt = jnp.exp(st - (m_new + off)).astype(BF16)
            a = jnp.exp(ms[hh] - m_new)
            if not last:
                scores(j + 1, hh)
            acc_scr[hh][...] = a * acc_scr[hh][...] + jnp.dot(vt_ref[hh, j], pt,
                                                              preferred_element_type=F32)
            out.append(m_new)
        return tuple(out)

    for hh in range(hp):
        scores(0, hh)
    ms = tuple(jnp.full((1, 2 * tb), -jnp.inf, F32) for _ in range(hp))
    ms = lax.fori_loop(0, qi, functools.partial(step, last=False), ms)
    step(qi, ms, last=True)

    for hh in range(hp):
        acc = acc_scr[hh][...]
        ot = acc[:HEAD_W] * (1.0 / acc[HEAD_W:HEAD_W + 1])
        o = (ot[:, :tb] - lam * ot[:, tb:]).T
        o = o * lax.rsqrt(jnp.mean(o * o, axis=-1, keepdims=True) + RMS_EPS)
        o = o * g_ref[...] * (1.0 - lam_init)
        o_ref[0, :, hh * HEAD_W:(hh + 1) * HEAD_W] = o.astype(o_ref.dtype)


def _diff_attention(qk, vt, slopes, lambda_qk, subln_g, lam_init, tb, hp):
    b, s, _ = qk.shape
    assert tb & (tb - 1) == 0 and tb <= 256
    kern = functools.partial(_attn_kernel, tb=tb, hp=hp, lam_init=lam_init)
    gw = hp * HEAD_W
    vrows = vt.shape[2]
    return pl.pallas_call(
        kern,
        grid=(b, ATTN_HEADS // hp, s // tb),
        in_specs=[pl.BlockSpec(memory_space=pltpu.SMEM),
                  pl.BlockSpec((4, HEAD_DIM), lambda b_, h, i: (0, 0)),
                  pl.BlockSpec((1, HEAD_W), lambda b_, h, i: (0, 0)),
                  pl.BlockSpec((1, tb, gw), lambda b_, h, i: (b_, i, h)),
                  pl.BlockSpec((1, s, gw), lambda b_, h, i: (b_, 0, ATTN_HEADS // hp + h),
                               pipeline_mode=pl.Buffered(1)),
                  pl.BlockSpec((hp, s // tb, vrows, tb), lambda b_, h, i: (h, b_, 0, 0),
                               pipeline_mode=pl.Buffered(1))],
        out_specs=pl.BlockSpec((1, tb, gw), lambda b_, h, i: (b_, i, h)),
        out_shape=jax.ShapeDtypeStruct((b, s, ATTN_HEADS * HEAD_W), BF16),
        scratch_shapes=([pltpu.VMEM((vrows, 2 * tb), F32)] * hp
                        + [pltpu.VMEM((tb, 2 * tb), F32)] * hp
                        + [pltpu.VMEM((s, 2 * HEAD_W), BF16)] * hp),
        compiler_params=_params("parallel", "parallel", "arbitrary"),
        name="diff_attn",
    )(slopes, lambda_qk, subln_g, qk, qk, vt)


def _gelu(x):
    return jax.nn.gelu(x)


def _gelu_tanh(x):
    c = math.sqrt(2.0 / math.pi)
    hx = 0.5 * x
    return hx + hx * jnp.tanh(x * (c + (c * 0.044715) * (x * x)))


def _lru_kernel(xr_ref, gr_ref, cw_ref, cb_ref, gaw_ref, gab_ref, gxw_ref, gxb_ref, lam_ref,
                o_ref, tail_scr, h_scr, a_scr, u_scr, gg_scr, *, ts):
    si = pl.program_id(1)

    @pl.when(si == 0)
    def _():
        tail_scr[...] = jnp.zeros_like(tail_scr)
        h_scr[...] = jnp.zeros_like(h_scr)

    x = xr_ref[0].astype(F32)
    w = x.shape[1]
    tail = tail_scr[...]
    row8 = lax.broadcasted_iota(jnp.int32, (SUBLANES, w), 0)
    cw = cw_ref[...]
    xb = cb_ref[...]
    for d in range(CONV_W - 1, 0, -1):
        xs = pltpu.roll(x, d, 0)
        top = jnp.where(row8 < d, pltpu.roll(tail, d, 0), xs[:SUBLANES])
        xs = jnp.concatenate([top, xs[SUBLANES:]], axis=0)
        xb = xb + xs * cw[CONV_W - 1 - d:CONV_W - d]
    xb = xb + x * cw[CONV_W - 1:CONV_W]
    tail_scr[...] = x[ts - SUBLANES:]

    xbb = xb.astype(BF16)
    bw = w // LRU_BLOCKS

    def gate(w_ref, b_ref):
        parts = [jnp.dot(xbb[:, g * bw:(g + 1) * bw], w_ref[g], preferred_element_type=F32)
                 for g in range(LRU_BLOCKS)]
        return jax.nn.sigmoid(jnp.concatenate(parts, axis=1) + b_ref[...])

    r = gate(gaw_ref, gab_ref)
    i = gate(gxw_ref, gxb_ref)
    lam = lam_ref[...]
    softplus_neg = jnp.maximum(-lam, 0.0) + jnp.log1p(jnp.exp(-jnp.abs(lam)))
    log_a = -LRU_C * r * softplus_neg
    a = jnp.exp(log_a)
    mult = jnp.sqrt(1.0 - jnp.exp(2.0 * log_a))
    first = jnp.where(jnp.logical_and(row8 == 0, si == 0), 1.0, mult[:SUBLANES])
    mult = jnp.concatenate([first, mult[SUBLANES:]], axis=0)
    a_scr[...] = a
    u_scr[...] = mult * (i * xb)
    gg_scr[...] = _gelu(gr_ref[0].astype(F32))

    def group(g, hprev):
        r0 = pl.multiple_of(g * SUBLANES, SUBLANES)
        ag = a_scr[pl.ds(r0, SUBLANES), :]
        ug = u_scr[pl.ds(r0, SUBLANES), :]
        for d in (1, 2, 4):
            keep = row8 >= d
            ug = jnp.where(keep, ag * pltpu.roll(ug, d, 0) + ug, ug)
            ag = jnp.where(keep, ag * pltpu.roll(ag, d, 0), ag)
        hg = ag * hprev + ug
        o_ref[0, pl.ds(r0, SUBLANES), :] = (hg * gg_scr[pl.ds(r0, SUBLANES), :]).astype(o_ref.dtype)
        return jnp.broadcast_to(hg[SUBLANES - 1:SUBLANES, :], (SUBLANES, w))

    h_scr[...] = lax.fori_loop(0, ts // SUBLANES, group, h_scr[...])


def _rg_lru(rest, conv_w, conv_b, gaw, gab, gxw, gxb, lru_lambda, ts):
    b, s, _ = rest.shape
    w = conv_w.shape[1]
    kern = functools.partial(_lru_kernel, ts=ts)
    vec = pl.BlockSpec((1, w), lambda b_, i: (0, 0))
    blk = pl.BlockSpec(gaw.shape, lambda b_, i: (0, 0, 0))
    return pl.pallas_call(
        kern,
        grid=(b, s // ts),
        in_specs=[pl.BlockSpec((1, ts, w), lambda b_, i: (b_, i, PROJ_LRU_COL)),
                  pl.BlockSpec((1, ts, w), lambda b_, i: (b_, i, PROJ_LRU_COL + 1)),
                  pl.BlockSpec((CONV_W, w), lambda b_, i: (0, 0)),
                  vec, blk, vec, blk, vec, vec],
        out_specs=pl.BlockSpec((1, ts, w), lambda b_, i: (b_, i, 0)),
        out_shape=jax.ShapeDtypeStruct((b, s, w), BF16),
        scratch_shapes=[pltpu.VMEM((SUBLANES, w), F32), pltpu.VMEM((SUBLANES, w), F32),
                        pltpu.VMEM((ts, w), F32), pltpu.VMEM((ts, w), F32),
                        pltpu.VMEM((ts, w), F32)],
        compiler_params=_params("parallel", "arbitrary"),
        name="rg_lru",
    )(rest, rest, conv_w, conv_b, gaw, gab, gxw, gxb, lru_lambda)


def _layer_norm(z, g, b):
    mu = jnp.mean(z, axis=-1, keepdims=True)
    zc = z - mu
    var = jnp.mean(zc * zc, axis=-1, keepdims=True)
    return zc * lax.rsqrt(var + LN_EPS) * g + b


def _merge_kernel(ya_ref, yr_ref, ga_ref, gl_ref, x_ref, wa_ref, wl_ref, wo_ref, g_ref, b_ref,
                  o_ref, ob_ref, *, alpha):
    pa = jnp.dot(ya_ref[...], wa_ref[...], preferred_element_type=F32)
    pr = jnp.dot(yr_ref[...], wl_ref[...], preferred_element_type=F32)
    merged = (jax.nn.sigmoid(ga_ref[...].astype(F32)) * pa
              + jax.nn.sigmoid(gl_ref[...].astype(F32)) * pr)
    mix = jnp.dot(merged.astype(BF16), wo_ref[...], preferred_element_type=F32)
    y = _layer_norm(alpha * x_ref[...] + mix, g_ref[...], b_ref[...])
    o_ref[...] = y
    ob_ref[...] = y.T.astype(BF16)


def _merge_out(ya, yr, rest, x, wa, wl, wo, ln_g, ln_b, alpha, tm):
    t, d = x.shape
    kern = functools.partial(_merge_kernel, alpha=alpha)
    rows = lambda c: pl.BlockSpec((tm, d), lambda i: (i, c))
    full = pl.BlockSpec((d, d), lambda i: (0, 0))
    vec = pl.BlockSpec((1, d), lambda i: (0, 0))
    return pl.pallas_call(
        kern,
        grid=(t // tm,),
        in_specs=[rows(0), rows(0), rows(PROJ_GATE_COL), rows(PROJ_GATE_COL + 1), rows(0),
                  full, full, full, vec, vec],
        out_specs=[rows(0), pl.BlockSpec((d, tm), lambda i: (0, i))],
        out_shape=[jax.ShapeDtypeStruct((t, d), F32), jax.ShapeDtypeStruct((d, t), BF16)],
        compiler_params=_params("parallel"),
        name="merge_out_ln",
    )(ya, yr, rest, rest, x, wa, wl, wo, ln_g, ln_b)


def _cmpx(v, i, j):
    hi = jnp.maximum(v[i], v[j])
    v[j] = jnp.minimum(v[i], v[j])
    v[i] = hi


def _bitonic_merge_desc(v):
    n = len(v)
    j = n // 2
    while j >= 1:
        for i in range(n):
            if i & j == 0:
                _cmpx(v, i, i + j)
        j //= 2


def _bitonic_sort_desc(v):
    n = len(v)
    k = 2
    while k <= n:
        j = k // 2
        while j >= 1:
            for i in range(n):
                l = i ^ j
                if l > i:
                    if i & k == 0:
                        _cmpx(v, i, l)
                    else:
                        _cmpx(v, l, i)
            j //= 2
        k *= 2


def _top_merge(a, b):
    n = len(a)
    v = [jnp.maximum(a[i], b[n - 1 - i]) for i in range(n)]
    _bitonic_merge_desc(v)
    return v


def _prefix_count(b, test):
    assert len(b) == 16
    c8 = test(b[7])
    n = jnp.where(c8, 8.0, 0.0)
    c4 = test(jnp.where(c8, b[11], b[3]))
    n = n + jnp.where(c4, 4.0, 0.0)
    c2 = test(jnp.where(c8, jnp.where(c4, b[13], b[9]), jnp.where(c4, b[5], b[1])))
    n = n + jnp.where(c2, 2.0, 0.0)
    lo = jnp.where(c4, jnp.where(c2, b[6], b[4]), jnp.where(c2, b[2], b[0]))
    hi = jnp.where(c4, jnp.where(c2, b[14], b[12]), jnp.where(c2, b[10], b[8]))
    n = n + jnp.where(test(jnp.where(c8, hi, lo)), 1.0, 0.0)
    return jnp.where(test(b[15]), 16.0, n)


_CAND_PAIRS = [(p, q) for p in range(PEER_TOPK) for q in range(PEER_TOPK)
               if (p + 1) * (q + 1) <= PEER_TOPK]


def _select_kernel(x_ref, wq_ref, sk_ref, rank_ref, bexp_ref, cnt_ref, arow_ref,
                   q_scr, s_scr, pk_scr, t_scr, *, tt):
    qt = jnp.dot(wq_ref[...], x_ref[...], preferred_element_type=F32)
    q_scr[...] = qt.astype(BF16)
    dk = sk_ref.shape[3]
    n_grp = N_KEYS // SUBLANES

    def head(h, _):
        for c in range(2):
            r0 = pl.multiple_of((h * 2 + c) * dk, dk)
            sc = jnp.dot(sk_ref[h, c], q_scr[pl.ds(r0, dk), :], preferred_element_type=F32)
            s_scr[c, h] = sc
            v = [sc[g * SUBLANES:(g + 1) * SUBLANES, :] for g in range(n_grp)]
            _bitonic_sort_desc(v)
            for sh in (4, 2, 1):
                v = _top_merge(v, [pltpu.roll(e, sh, 0) for e in v])
            for p in range(PEER_TOPK):
                pk_scr[c, p, pl.ds(h, 1), :] = v[p][0:1, :]
        return 0

    lax.fori_loop(0, PEER_HEADS, head, 0)

    a = [pk_scr[0, p] for p in range(PEER_TOPK)]
    b = [pk_scr[1, p] for p in range(PEER_TOPK)]
    cands = [a[p] + b[q] for p, q in _CAND_PAIRS]
    neg = jnp.full_like(cands[0], -jnp.inf)
    blocks = cands + [neg] * ((-len(cands)) % PEER_TOPK)
    top = None
    for i0 in range(0, len(blocks), PEER_TOPK):
        blk = blocks[i0:i0 + PEER_TOPK]
        _bitonic_sort_desc(blk)
        top = blk if top is None else _top_merge(top, blk)
    tau = top[PEER_TOPK - 1]
    mx = a[0] + b[0]
    z = jnp.zeros_like(tau)
    for cnd in cands:
        z = z + jnp.where(cnd >= tau, jnp.exp(cnd - mx), 0.0)
    t_scr[0] = tau
    t_scr[1] = a[0] + jnp.log(z)

    def head2(h, _):
        s1 = s_scr[0, h]
        s2 = s_scr[1, h]
        tau_h = t_scr[0, pl.ds(h, 1), :]
        bs = [pk_scr[1, q, pl.ds(h, 1), :] for q in range(PEER_TOPK)]
        rank_ref[h] = _prefix_count(bs, lambda thr: thr > s2).astype(BF16)
        cnt_ref[h] = _prefix_count(bs, lambda thr: s1 + thr >= tau_h)
        arow_ref[h] = jnp.exp(s1 - t_scr[1, pl.ds(h, 1), :])
        bexp_ref[h] = jnp.exp(s2 - pk_scr[1, 0, pl.ds(h, 1), :]).astype(BF16)
        return 0

    lax.fori_loop(0, PEER_HEADS, head2, 0)


def _peer_select(xt, wq_t, sk, tt):
    d, t = xt.shape
    nq = wq_t.shape[0]
    kern = functools.partial(_select_kernel, tt=tt)
    big = pl.BlockSpec((PEER_HEADS, N_KEYS, tt), lambda i: (0, 0, i))
    shape = lambda dt: jax.ShapeDtypeStruct((PEER_HEADS, N_KEYS, t), dt)
    return pl.pallas_call(
        kern,
        grid=(t // tt,),
        in_specs=[pl.BlockSpec((d, tt), lambda i: (0, i)),
                  pl.BlockSpec((nq, d), lambda i: (0, 0)),
                  pl.BlockSpec(sk.shape, lambda i: (0, 0, 0, 0))],
        out_specs=[big, big, big, big],
        out_shape=[shape(BF16), shape(BF16), shape(F32), shape(F32)],
        scratch_shapes=[pltpu.VMEM((nq, tt), BF16),
                        pltpu.VMEM((2, PEER_HEADS, N_KEYS, tt), F32),
                        pltpu.VMEM((2, PEER_TOPK, PEER_HEADS, tt), F32),
                        pltpu.VMEM((2, PEER_HEADS, tt), F32)],
        compiler_params=_params("parallel"),
        name="peer_select",
    )(xt, wq_t, sk)


def _peer_kernel(xt_ref, x_ref, u_ref, vt_ref, rank_ref, bexp_ref, cnt_ref, arow_ref,
                 g_ref, b_ref, o_ref, acc_scr, row_scr, *bufs, tt, nblk, eb, alpha):
    e = pl.program_id(1)

    @pl.when(e == 0)
    def _():
        acc_scr[...] = jnp.zeros_like(acc_scr)

    acts, ws = bufs[:N_ACT_BUFS], bufs[N_ACT_BUFS:]
    rows_per_blk = eb // N_KEYS
    half = acc_scr.shape[0] // 2

    def mm1(k):
        acts[k % N_ACT_BUFS][...] = jnp.dot(u_ref[k * eb:(k + 1) * eb, :], xt_ref[...],
                                            preferred_element_type=F32)

    def mm2(p, part):
        rows = slice(part * half, (part + 1) * half)
        acc_scr[rows, :] += jnp.dot(vt_ref[p, rows, :], ws[p % N_W_BUFS][...],
                                    preferred_element_type=F32)

    def gates(k):
        act, w = acts[k % N_ACT_BUFS], ws[(k // 2) % N_W_BUFS]
        for ii in range(rows_per_blk):
            i = k * rows_per_blk + ii
            for h in range(PEER_HEADS):
                for a, ref in enumerate((cnt_ref, arow_ref)):
                    row_scr[a, h] = jnp.broadcast_to(ref[h, i:i + 1, :], (GATE_ROWS, tt)).astype(BF16)
            for j0 in range(0, N_KEYS, GATE_ROWS):
                r = ii * N_KEYS + j0
                rw = (k % 2) * eb + r
                gsum = None
                for h in range(PEER_HEADS):
                    sel = rank_ref[h, j0:j0 + GATE_ROWS, :] < row_scr[0, h]
                    term = jnp.where(sel, row_scr[1, h] * bexp_ref[h, j0:j0 + GATE_ROWS, :],
                                     jnp.zeros((), BF16))
                    gsum = term if gsum is None else gsum + term
                w[rw:rw + GATE_ROWS, :] = _gelu_tanh(act[r:r + GATE_ROWS, :].astype(BF16)) * gsum

    for k in range(MM1_AHEAD):
        mm1(k)
    for k in range(nblk):
        if k + MM1_AHEAD < nblk:
            mm1(k + MM1_AHEAD)
        if k >= 2:
            mm2(k // 2 - 1, k % 2)
        gates(k)
    mm2(nblk // 2 - 1, 0)
    mm2(nblk // 2 - 1, 1)

    @pl.when(e == pl.num_programs(1) - 1)
    def _():
        ffn = acc_scr[...].T
        o_ref[...] = _layer_norm(alpha * x_ref[...] + ffn, g_ref[...], b_ref[...])


def _peer_mix(xt, x, u, vt, rank, bexp, cnt, arow, ln_g, ln_b, alpha, tt, te, eb):
    t, d = x.shape
    n_exp = u.shape[0]
    nblk = te // eb
    kern = functools.partial(_peer_kernel, tt=tt, nblk=nblk, eb=eb, alpha=alpha)
    big = pl.BlockSpec((PEER_HEADS, N_KEYS, tt), lambda i, e: (0, 0, i))
    rows = pl.BlockSpec((PEER_HEADS, te // N_KEYS, tt), lambda i, e: (0, e, i))
    vec = pl.BlockSpec((1, d), lambda i, e: (0, 0))
    return pl.pallas_call(
        kern,
        grid=(t // tt, n_exp // te),
        in_specs=[pl.BlockSpec((d, tt), lambda i, e: (0, i)),
                  pl.BlockSpec((tt, d), lambda i, e: (i, 0)),
                  pl.BlockSpec((te, d), lambda i, e: (e, 0)),
                  pl.BlockSpec((nblk // 2, d, 2 * eb), lambda i, e: (e, 0, 0)),
                  big, big, rows, rows, vec, vec],
        out_specs=pl.BlockSpec((tt, d), lambda i, e: (i, 0)),
        out_shape=jax.ShapeDtypeStruct((t, d), F32),
        scratch_shapes=([pltpu.VMEM((d, tt), F32),
                         pltpu.VMEM((2, PEER_HEADS, GATE_ROWS, tt), BF16)]
                        + [pltpu.VMEM((eb, tt), F32)] * N_ACT_BUFS
                        + [pltpu.VMEM((2 * eb, tt), BF16)] * N_W_BUFS),
        compiler_params=_params("parallel", "arbitrary"),
        name="peer_mix_ln",
    )(xt, x, u, vt, rank, bexp, cnt, arow, ln_g, ln_b)


def _tiles(b, s, t):
    return dict(mm_tm=min(2048, t), mm_tn=2048, attn_tb=min(256, s), attn_hp=8, vt_blocks=4,
                lru_ts=min(512, s), merge_tm=min(512, t), sel_tt=min(512, t),
                mix_tt=min(512, t), mix_te=4096, mix_eb=256)


def kernel(x, w_in, lambda_qk, subln_g, conv_w, conv_b, gate_a_w, gate_a_b, gate_x_w, gate_x_b,
           lru_lambda, w_br_attn, w_br_lru, w_out, ln1_g, ln1_b, peer_wq, peer_subkeys, peer_u,
           peer_v, ln2_g, ln2_b):
    b, s, d = x.shape
    t = b * s
    depth = w_in.shape[0]
    alpha = (2.0 * depth) ** 0.25
    attn_w = ATTN_HEADS * HEAD_W
    ts = _tiles(b, s, t)
    slopes = jnp.exp2(-8.0 * jnp.arange(1, ATTN_HEADS + 1, dtype=F32) / ATTN_HEADS)
    row = lambda a: a.reshape(1, -1)
    eb = ts["mix_eb"]

    xf = x.reshape(t, d)
    for l in range(depth):
        lam_init = 0.8 - 0.6 * math.exp(-0.3 * l)
        w_l = w_in[l].astype(BF16)
        w_cat = jnp.concatenate([w_l[:, :2 * attn_w], w_l[:, 3 * attn_w:]], axis=1)
        proj = _matmul(xf, w_cat, BF16, ts["mm_tm"], ts["mm_tn"])
        vt = _value_proj_t(xf, w_l[:, 2 * attn_w:3 * attn_w].T, ts["attn_tb"], ts["vt_blocks"])
        proj3 = proj.reshape(b, s, -1)
        ya = _diff_attention(proj3, vt, slopes, lambda_qk[l], row(subln_g[l]),
                             lam_init, ts["attn_tb"], ts["attn_hp"])
        yr = _rg_lru(proj3, conv_w[l], row(conv_b[l]), gate_a_w[l].astype(BF16),
                     row(gate_a_b[l]), gate_x_w[l].astype(BF16), row(gate_x_b[l]),
                     row(lru_lambda[l]), ts["lru_ts"])
        x1, x1t = _merge_out(ya.reshape(t, attn_w), yr.reshape(t, d), proj, xf,
                             w_br_attn[l].astype(BF16), w_br_lru[l].astype(BF16),
                             w_out[l].astype(BF16), row(ln1_g[l]), row(ln1_b[l]), alpha,
                             ts["merge_tm"])
        rank, bexp, cnt, arow = _peer_select(x1t, peer_wq[l].T.astype(BF16),
                                             peer_subkeys[l].astype(BF16), ts["sel_tt"])
        v_blocks = peer_v[l].astype(BF16).reshape(-1, 2 * eb, d).transpose(0, 2, 1)
        xf = _peer_mix(x1t, x1, peer_u[l].astype(BF16), v_blocks, rank, bexp, cnt, arow,
                       row(ln2_g[l]), row(ln2_b[l]), alpha, ts["mix_tt"], ts["mix_te"], eb)
    return xf.reshape(b, s, d)
```

```python
import functools
import math

import jax
import jax.numpy as jnp
from jax import lax
from jax.experimental import pallas as pl
from jax.experimental.pallas import tpu as pltpu

F32 = jnp.float32
BF16 = jnp.bfloat16

ATTN_HEADS = 8
HEAD_DIM = 64
HEAD_W = 2 * HEAD_DIM
LRU_BLOCKS = 8
LRU_C = 8.0
CONV_W = 4
PEER_HEADS = 8
N_KEYS = 128
PEER_TOPK = 16
LN_EPS = 1e-5
RMS_EPS = 1e-6
PROJ_LRU_COL = 2
PROJ_GATE_COL = 4
SUBLANES = 8
PACKED_ROWS = 16
V_AUG_ROWS = PACKED_ROWS
GATE_ROWS = PACKED_ROWS
MM1_AHEAD = 2
N_ACT_BUFS = MM1_AHEAD + 2
N_W_BUFS = 3
VMEM_LIMIT = 56 * 1024 * 1024


def _params(*sem):
    return pltpu.CompilerParams(dimension_semantics=sem, vmem_limit_bytes=VMEM_LIMIT)


MM_ROWS = 256


def _mm_kernel(x_ref, w_ref, o_ref):
    for r in range(0, x_ref.shape[0], MM_ROWS):
        o_ref[r:r + MM_ROWS, :] = jnp.dot(x_ref[r:r + MM_ROWS, :].astype(BF16), w_ref[...],
                                          preferred_element_type=F32).astype(o_ref.dtype)


def _matmul(x, w, out_dtype, tm, tn):
    t, k = x.shape
    n = w.shape[1]
    return pl.pallas_call(
        _mm_kernel,
        grid=(t // tm, n // tn),
        in_specs=[pl.BlockSpec((tm, k), lambda i, j: (i, 0)),
                  pl.BlockSpec((k, tn), lambda i, j: (0, j))],
        out_specs=pl.BlockSpec((tm, tn), lambda i, j: (i, j)),
        out_shape=jax.ShapeDtypeStruct((t, n), out_dtype),
        compiler_params=_params("parallel", "arbitrary"),
        name="in_proj",
    )(x, w)


def _vt_kernel(x_ref, w_ref, o_ref):
    tk = o_ref.shape[3]
    ones_row = lax.broadcasted_iota(jnp.int32, (ATTN_HEADS, V_AUG_ROWS, tk), 1) == 0
    extra = jnp.where(ones_row, 1.0, 0.0).astype(o_ref.dtype)
    for blk in range(o_ref.shape[1]):
        xt = x_ref[blk * tk:(blk + 1) * tk, :].T.astype(BF16)
        vt = jnp.dot(w_ref[...], xt, preferred_element_type=F32)
        vt = vt.astype(o_ref.dtype).reshape(ATTN_HEADS, HEAD_W, tk)
        o_ref[:, blk] = jnp.concatenate([vt, extra], axis=1)


def _value_proj_t(x, w_t, tk, blocks_per_step):
    t, d = x.shape
    rows = HEAD_W + V_AUG_ROWS
    nb = math.gcd(blocks_per_step, t // tk)
    return pl.pallas_call(
        _vt_kernel,
        grid=(t // (tk * nb),),
        in_specs=[pl.BlockSpec((tk * nb, d), lambda i: (i, 0)),
                  pl.BlockSpec(w_t.shape, lambda i: (0, 0))],
        out_specs=pl.BlockSpec((ATTN_HEADS, nb, rows, tk), lambda i: (0, i, 0, 0)),
        out_shape=jax.ShapeDtypeStruct((ATTN_HEADS, t // tk, rows, tk), BF16),
        compiler_params=_params("parallel"),
        name="value_proj_t",
    )(x, w_t)


def _attn_kernel(slopes_ref, lqk_ref, g_ref, q_ref, k_ref, vt_ref, o_ref, *scr, tb, hp, lam_init):
    acc_scr, st_scr, ka_scr = scr[:hp], scr[hp:2 * hp], scr[2 * hp:]
    hg = pl.program_id(1)
    qi = pl.program_id(2)
    s = k_ref.shape[1]
    lq = lqk_ref[...]
    lam = (jnp.exp(jnp.sum(lq[0:1] * lq[1:2], axis=1, keepdims=True))
           - jnp.exp(jnp.sum(lq[2:3] * lq[3:4], axis=1, keepdims=True)) + lam_init)
    slopes = [slopes_ref[hg * hp + hh] for hh in range(hp)]

    @pl.when(qi == 0)
    def _():
        lane = lax.broadcasted_iota(jnp.int32, (s, HEAD_W), 1)
        c = (lax.broadcasted_iota(jnp.int32, (s, HEAD_W), 0) & (tb - 1)).astype(F32)
        for hh in range(hp):
            extra = jnp.where(lane == 0, 1.0, jnp.where(lane == 1, slopes[hh] * c, 0.0))
            ka_scr[hh][:, :HEAD_W] = k_ref[0, :, hh * HEAD_W:(hh + 1) * HEAD_W]
            ka_scr[hh][:, HEAD_W:] = extra.astype(BF16)

    lane = lax.broadcasted_iota(jnp.int32, (tb, HEAD_W), 1)
    lane2 = lax.broadcasted_iota(jnp.int32, (2 * tb, HEAD_W), 1)
    r = (lax.broadcasted_iota(jnp.int32, (2 * tb, HEAD_W), 0) & (tb - 1)).astype(F32)
    masked = ((lax.broadcasted_iota(jnp.int32, (tb, 2 * tb), 1) & (tb - 1))
              < lax.broadcasted_iota(jnp.int32, (tb, 2 * tb), 0))
    qqs = []
    for hh in range(hp):
        qs = q_ref[0, :, hh * HEAD_W:(hh + 1) * HEAD_W].astype(F32) * (HEAD_DIM ** -0.5)
        zero = jnp.zeros_like(qs)
        qq = jnp.concatenate([jnp.where(lane < HEAD_DIM, qs, zero),
                              jnp.where(lane >= HEAD_DIM, qs, zero)], axis=0)
        extra = jnp.where(lane2 == 0, -slopes[hh] * r, jnp.where(lane2 == 1, 1.0, 0.0))
        qqs.append(jnp.concatenate([qq, extra], axis=1).T.astype(BF16))
        acc_scr[hh][...] = jnp.zeros_like(acc_scr[hh])

    def scores(j, hh):
        start = pl.multiple_of(j * tb, tb)
        st_scr[hh][...] = jnp.dot(ka_scr[hh][pl.ds(start, tb), :], qqs[hh],
                                  preferred_element_type=F32)

    def step(j, ms, last):
        shift = (qi - j) * tb
        out = []
        for hh in range(hp):
            off = slopes[hh] * shift.astype(F32)
            st = st_scr[hh][...]
            if last:
                st = jnp.where(masked, -jnp.inf, st)
            m_new = jnp.maximum(ms[hh], jnp.max(st, axis=0, keepdims=True) - off)
            pt = jnp.exp(st - (m_new + off)).astype(BF16)
            a = jnp.exp(ms[hh] - m_new)
            if not last:
                scores(j + 1, hh)
            acc_scr[hh][...] = a * acc_scr[hh][...] + jnp.dot(vt_ref[hh, j], pt,
                                                              preferred_element_type=F32)
            out.append(m_new)
        return tuple(out)

    for hh in range(hp):
        scores(0, hh)
    ms = tuple(jnp.full((1, 2 * tb), -jnp.inf, F32) for _ in range(hp))
    ms = lax.fori_loop(0, qi, functools.partial(step, last=False), ms)
    step(qi, ms, last=True)

    for hh in range(hp):
        acc = acc_scr[hh][...]
        ot = acc[:HEAD_W] * (1.0 / acc[HEAD_W:HEAD_W + 1])
        o = (ot[:, :tb] - lam * ot[:, tb:]).T
        o = o * lax.rsqrt(jnp.mean(o * o, axis=-1, keepdims=True) + RMS_EPS)
        o = o * g_ref[...] * (1.0 - lam_init)
        o_ref[0, :, hh * HEAD_W:(hh + 1) * HEAD_W] = o.astype(o_ref.dtype)


def _diff_attention(qk, vt, slopes, lambda_qk, subln_g, lam_init, tb, hp):
    b, s, _ = qk.shape
    assert tb & (tb - 1) == 0 and tb <= 256
    kern = functools.partial(_attn_kernel, tb=tb, hp=hp, lam_init=lam_init)
    gw = hp * HEAD_W
    vrows = vt.shape[2]
    return pl.pallas_call(
        kern,
        grid=(b, ATTN_HEADS // hp, s // tb),
        in_specs=[pl.BlockSpec(memory_space=pltpu.SMEM),
                  pl.BlockSpec((4, HEAD_DIM), lambda b_, h, i: (0, 0)),
                  pl.BlockSpec((1, HEAD_W), lambda b_, h, i: (0, 0)),
                  pl.BlockSpec((1, tb, gw), lambda b_, h, i: (b_, i, h)),
                  pl.BlockSpec((1, s, gw), lambda b_, h, i: (b_, 0, ATTN_HEADS // hp + h),
                               pipeline_mode=pl.Buffered(1)),
                  pl.BlockSpec((hp, s // tb, vrows, tb), lambda b_, h, i: (h, b_, 0, 0),
                               pipeline_mode=pl.Buffered(1))],
        out_specs=pl.BlockSpec((1, tb, gw), lambda b_, h, i: (b_, i, h)),
        out_shape=jax.ShapeDtypeStruct((b, s, ATTN_HEADS * HEAD_W), BF16),
        scratch_shapes=([pltpu.VMEM((vrows, 2 * tb), F32)] * hp
                        + [pltpu.VMEM((tb, 2 * tb), F32)] * hp
                        + [pltpu.VMEM((s, 2 * HEAD_W), BF16)] * hp),
        compiler_params=_params("parallel", "parallel", "arbitrary"),
        name="diff_attn",
    )(slopes, lambda_qk, subln_g, qk, qk, vt)


def _gelu(x):
    return jax.nn.gelu(x)


def _gelu_tanh(x):
    c = math.sqrt(2.0 / math.pi)
    hx = 0.5 * x
    return hx + hx * jnp.tanh(x * (c + (c * 0.044715) * (x * x)))


def _lru_kernel(xr_ref, gr_ref, cw_ref, cb_ref, gaw_ref, gab_ref, gxw_ref, gxb_ref, lam_ref,
                o_ref, tail_scr, h_scr, a_scr, u_scr, gg_scr, *, ts):
    si = pl.program_id(1)

    @pl.when(si == 0)
    def _():
        tail_scr[...] = jnp.zeros_like(tail_scr)
        h_scr[...] = jnp.zeros_like(h_scr)

    x = xr_ref[0].astype(F32)
    w = x.shape[1]
    tail = tail_scr[...]
    row8 = lax.broadcasted_iota(jnp.int32, (SUBLANES, w), 0)
    cw = cw_ref[...]
    xb = cb_ref[...]
    for d in range(CONV_W - 1, 0, -1):
        xs = pltpu.roll(x, d, 0)
        top = jnp.where(row8 < d, pltpu.roll(tail, d, 0), xs[:SUBLANES])
        xs = jnp.concatenate([top, xs[SUBLANES:]], axis=0)
        xb = xb + xs * cw[CONV_W - 1 - d:CONV_W - d]
    xb = xb + x * cw[CONV_W - 1:CONV_W]
    tail_scr[...] = x[ts - SUBLANES:]

    xbb = xb.astype(BF16)
    bw = w // LRU_BLOCKS

    def gate(w_ref, b_ref):
        parts = [jnp.dot(xbb[:, g * bw:(g + 1) * bw], w_ref[g], preferred_element_type=F32)
                 for g in range(LRU_BLOCKS)]
        return jax.nn.sigmoid(jnp.concatenate(parts, axis=1) + b_ref[...])

    r = gate(gaw_ref, gab_ref)
    i = gate(gxw_ref, gxb_ref)
    lam = lam_ref[...]
    softplus_neg = jnp.maximum(-lam, 0.0) + jnp.log1p(jnp.exp(-jnp.abs(lam)))
    log_a = -LRU_C * r * softplus_neg
    a = jnp.exp(log_a)
    mult = jnp.sqrt(1.0 - jnp.exp(2.0 * log_a))
    first = jnp.where(jnp.logical_and(row8 == 0, si == 0), 1.0, mult[:SUBLANES])
    mult = jnp.concatenate([first, mult[SUBLANES:]], axis=0)
    a_scr[...] = a
    u_scr[...] = mult * (i * xb)
    gg_scr[...] = _gelu(gr_ref[0].astype(F32))

    def group(g, hprev):
        r0 = pl.multiple_of(g * SUBLANES, SUBLANES)
        ag = a_scr[pl.ds(r0, SUBLANES), :]
        ug = u_scr[pl.ds(r0, SUBLANES), :]
        for d in (1, 2, 4):
            keep = row8 >= d
            ug = jnp.where(keep, ag * pltpu.roll(ug, d, 0) + ug, ug)
            ag = jnp.where(keep, ag * pltpu.roll(ag, d, 0), ag)
        hg = ag * hprev + ug
        o_ref[0, pl.ds(r0, SUBLANES), :] = (hg * gg_scr[pl.ds(r0, SUBLANES), :]).astype(o_ref.dtype)
        return jnp.broadcast_to(hg[SUBLANES - 1:SUBLANES, :], (SUBLANES, w))

    h_scr[...] = lax.fori_loop(0, ts // SUBLANES, group, h_scr[...], unroll=4)


def _rg_lru(rest, conv_w, conv_b, gaw, gab, gxw, gxb, lru_lambda, ts):
    b, s, _ = rest.shape
    w = conv_w.shape[1]
    kern = functools.partial(_lru_kernel, ts=ts)
    vec = pl.BlockSpec((1, w), lambda b_, i: (0, 0))
    blk = pl.BlockSpec(gaw.shape, lambda b_, i: (0, 0, 0))
    return pl.pallas_call(
        kern,
        grid=(b, s // ts),
        in_specs=[pl.BlockSpec((1, ts, w), lambda b_, i: (b_, i, PROJ_LRU_COL)),
                  pl.BlockSpec((1, ts, w), lambda b_, i: (b_, i, PROJ_LRU_COL + 1)),
                  pl.BlockSpec((CONV_W, w), lambda b_, i: (0, 0)),
                  vec, blk, vec, blk, vec, vec],
        out_specs=pl.BlockSpec((1, ts, w), lambda b_, i: (b_, i, 0)),
        out_shape=jax.ShapeDtypeStruct((b, s, w), BF16),
        scratch_shapes=[pltpu.VMEM((SUBLANES, w), F32), pltpu.VMEM((SUBLANES, w), F32),
                        pltpu.VMEM((ts, w), F32), pltpu.VMEM((ts, w), F32),
                        pltpu.VMEM((ts, w), F32)],
        compiler_params=_params("parallel", "arbitrary"),
        name="rg_lru",
    )(rest, rest, conv_w, conv_b, gaw, gab, gxw, gxb, lru_lambda)


def _layer_norm(z, g, b):
    mu = jnp.mean(z, axis=-1, keepdims=True)
    zc = z - mu
    var = jnp.mean(zc * zc, axis=-1, keepdims=True)
    return zc * lax.rsqrt(var + LN_EPS) * g + b


def _merge_kernel(ya_ref, yr_ref, ga_ref, gl_ref, x_ref, wa_ref, wl_ref, wo_ref, g_ref, b_ref,
                  o_ref, ob_ref, *, alpha):
    pa = jnp.dot(ya_ref[...], wa_ref[...], preferred_element_type=F32)
    pr = jnp.dot(yr_ref[...], wl_ref[...], preferred_element_type=F32)
    merged = (jax.nn.sigmoid(ga_ref[...].astype(F32)) * pa
              + jax.nn.sigmoid(gl_ref[...].astype(F32)) * pr)
    mix = jnp.dot(merged.astype(BF16), wo_ref[...], preferred_element_type=F32)
    y = _layer_norm(alpha * x_ref[...] + mix, g_ref[...], b_ref[...])
    o_ref[...] = y
    ob_ref[...] = y.T.astype(BF16)


def _merge_out(ya, yr, rest, x, wa, wl, wo, ln_g, ln_b, alpha, tm):
    t, d = x.shape
    kern = functools.partial(_merge_kernel, alpha=alpha)
    rows = lambda c: pl.BlockSpec((tm, d), lambda i: (i, c))
    full = pl.BlockSpec((d, d), lambda i: (0, 0))
    vec = pl.BlockSpec((1, d), lambda i: (0, 0))
    return pl.pallas_call(
        kern,
        grid=(t // tm,),
        in_specs=[rows(0), rows(0), rows(PROJ_GATE_COL), rows(PROJ_GATE_COL + 1), rows(0),
                  full, full, full, vec, vec],
        out_specs=[rows(0), pl.BlockSpec((d, tm), lambda i: (0, i))],
        out_shape=[jax.ShapeDtypeStruct((t, d), F32), jax.ShapeDtypeStruct((d, t), BF16)],
        compiler_params=_params("parallel"),
        name="merge_out_ln",
    )(ya, yr, rest, rest, x, wa, wl, wo, ln_g, ln_b)


def _cmpx(v, i, j):
    hi = jnp.maximum(v[i], v[j])
    v[j] = jnp.minimum(v[i], v[j])
    v[i] = hi


def _bitonic_merge_desc(v):
    n = len(v)
    j = n // 2
    while j >= 1:
        for i in range(n):
            if i & j == 0:
                _cmpx(v, i, i + j)
        j //= 2


def _bitonic_sort_desc(v):
    n = len(v)
    k = 2
    while k <= n:
        j = k // 2
        while j >= 1:
            for i in range(n):
                l = i ^ j
                if l > i:
                    if i & k == 0:
                        _cmpx(v, i, l)
                    else:
                        _cmpx(v, l, i)
            j //= 2
        k *= 2


def _top_merge(a, b):
    n = len(a)
    v = [jnp.maximum(a[i], b[n - 1 - i]) for i in range(n)]
    _bitonic_merge_desc(v)
    return v


def _prefix_count(b, test):
    assert len(b) == 16
    c8 = test(b[7])
    n = jnp.where(c8, 8.0, 0.0)
    c4 = test(jnp.where(c8, b[11], b[3]))
    n = n + jnp.where(c4, 4.0, 0.0)
    c2 = test(jnp.where(c8, jnp.where(c4, b[13], b[9]), jnp.where(c4, b[5], b[1])))
    n = n + jnp.where(c2, 2.0, 0.0)
    lo = jnp.where(c4, jnp.where(c2, b[6], b[4]), jnp.where(c2, b[2], b[0]))
    hi = jnp.where(c4, jnp.where(c2, b[14], b[12]), jnp.where(c2, b[10], b[8]))
    n = n + jnp.where(test(jnp.where(c8, hi, lo)), 1.0, 0.0)
    return jnp.where(test(b[15]), 16.0, n)


_CAND_PAIRS = [(p, q) for p in range(PEER_TOPK) for q in range(PEER_TOPK)
               if (p + 1) * (q + 1) <= PEER_TOPK]


def _select_kernel(x_ref, wq_ref, sk_ref, rank_ref, bexp_ref, cnt_ref, arow_ref,
                   q_scr, s_scr, pk_scr, t_scr, *, tt):
    qt = jnp.dot(wq_ref[...], x_ref[...], preferred_element_type=F32)
    q_scr[...] = qt.astype(BF16)
    dk = sk_ref.shape[3]
    n_grp = N_KEYS // SUBLANES

    def head(h, _):
        for c in range(2):
            r0 = pl.multiple_of((h * 2 + c) * dk, dk)
            sc = jnp.dot(sk_ref[h, c], q_scr[pl.ds(r0, dk), :], preferred_element_type=F32)
            s_scr[c, h] = sc
            v = [sc[g * SUBLANES:(g + 1) * SUBLANES, :] for g in range(n_grp)]
            _bitonic_sort_desc(v)
            for sh in (4, 2, 1):
                v = _top_merge(v, [pltpu.roll(e, sh, 0) for e in v])
            for p in range(PEER_TOPK):
                pk_scr[c, p, pl.ds(h, 1), :] = v[p][0:1, :]
        return 0

    lax.fori_loop(0, PEER_HEADS, head, 0)

    a = [pk_scr[0, p] for p in range(PEER_TOPK)]
    b = [pk_scr[1, p] for p in range(PEER_TOPK)]
    cands = [a[p] + b[q] for p, q in _CAND_PAIRS]
    neg = jnp.full_like(cands[0], -jnp.inf)
    blocks = cands + [neg] * ((-len(cands)) % PEER_TOPK)
    top = None
    for i0 in range(0, len(blocks), PEER_TOPK):
        blk = blocks[i0:i0 + PEER_TOPK]
        _bitonic_sort_desc(blk)
        top = blk if top is None else _top_merge(top, blk)
    tau = top[PEER_TOPK - 1]
    mx = a[0] + b[0]
    z = jnp.zeros_like(tau)
    for cnd in cands:
        z = z + jnp.where(cnd >= tau, jnp.exp(cnd - mx), 0.0)
    t_scr[0] = tau
    t_scr[1] = a[0] + jnp.log(z)

    def head2(h, _):
        s1 = s_scr[0, h]
        s2 = s_scr[1, h]
        tau_h = t_scr[0, pl.ds(h, 1), :]
        bs = [pk_scr[1, q, pl.ds(h, 1), :] for q in range(PEER_TOPK)]
        rank_ref[h] = _prefix_count(bs, lambda thr: thr > s2).astype(BF16)
        cnt_ref[h] = _prefix_count(bs, lambda thr: s1 + thr >= tau_h)
        arow_ref[h] = jnp.exp(s1 - t_scr[1, pl.ds(h, 1), :])
        bexp_ref[h] = jnp.exp(s2 - pk_scr[1, 0, pl.ds(h, 1), :]).astype(BF16)
        return 0

    lax.fori_loop(0, PEER_HEADS, head2, 0)


def _peer_select(xt, wq_t, sk, tt):
    d, t = xt.shape
    nq = wq_t.shape[0]
    kern = functools.partial(_select_kernel, tt=tt)
    big = pl.BlockSpec((PEER_HEADS, N_KEYS, tt), lambda i: (0, 0, i))
    shape = lambda dt: jax.ShapeDtypeStruct((PEER_HEADS, N_KEYS, t), dt)
    return pl.pallas_call(
        kern,
        grid=(t // tt,),
        in_specs=[pl.BlockSpec((d, tt), lambda i: (0, i)),
                  pl.BlockSpec((nq, d), lambda i: (0, 0)),
                  pl.BlockSpec(sk.shape, lambda i: (0, 0, 0, 0))],
        out_specs=[big, big, big, big],
        out_shape=[shape(BF16), shape(BF16), shape(F32), shape(F32)],
        scratch_shapes=[pltpu.VMEM((nq, tt), BF16),
                        pltpu.VMEM((2, PEER_HEADS, N_KEYS, tt), F32),
                        pltpu.VMEM((2, PEER_TOPK, PEER_HEADS, tt), F32),
                        pltpu.VMEM((2, PEER_HEADS, tt), F32)],
        compiler_params=_params("parallel"),
        name="peer_select",
    )(xt, wq_t, sk)


def _peer_kernel(xt_ref, x_ref, u_ref, vt_ref, rank_ref, bexp_ref, cnt_ref, arow_ref,
                 g_ref, b_ref, o_ref, acc_scr, row_scr, *bufs, tt, nblk, eb, alpha):
    e = pl.program_id(1)

    @pl.when(e == 0)
    def _():
        acc_scr[...] = jnp.zeros_like(acc_scr)

    acts, ws = bufs[:N_ACT_BUFS], bufs[N_ACT_BUFS:]
    rows_per_blk = eb // N_KEYS
    half = acc_scr.shape[0] // 2

    def mm1(k):
        acts[k % N_ACT_BUFS][...] = jnp.dot(u_ref[k * eb:(k + 1) * eb, :], xt_ref[...],
                                            preferred_element_type=F32)

    def mm2(p, part):
        rows = slice(part * half, (part + 1) * half)
        acc_scr[rows, :] += jnp.dot(vt_ref[p, rows, :], ws[p % N_W_BUFS][...],
                                    preferred_element_type=F32)

    def gates(k):
        act, w = acts[k % N_ACT_BUFS], ws[(k // 2) % N_W_BUFS]
        for ii in range(rows_per_blk):
            i = k * rows_per_blk + ii
            for h in range(PEER_HEADS):
                for a, ref in enumerate((cnt_ref, arow_ref)):
                    row_scr[a, h] = jnp.broadcast_to(ref[h, i:i + 1, :], (GATE_ROWS, tt)).astype(BF16)
            for j0 in range(0, N_KEYS, GATE_ROWS):
                r = ii * N_KEYS + j0
                rw = (k % 2) * eb + r
                gsum = None
                for h in range(PEER_HEADS):
                    sel = rank_ref[h, j0:j0 + GATE_ROWS, :] < row_scr[0, h]
                    term = jnp.where(sel, row_scr[1, h] * bexp_ref[h, j0:j0 + GATE_ROWS, :],
                                     jnp.zeros((), BF16))
                    gsum = term if gsum is None else gsum + term
                w[rw:rw + GATE_ROWS, :] = _gelu_tanh(act[r:r + GATE_ROWS, :].astype(BF16)) * gsum

    for k in range(MM1_AHEAD):
        mm1(k)
    for k in range(nblk):
        if k + MM1_AHEAD < nblk:
            mm1(k + MM1_AHEAD)
        if k >= 2:
            mm2(k // 2 - 1, k % 2)
        gates(k)
    mm2(nblk // 2 - 1, 0)
    mm2(nblk // 2 - 1, 1)

    @pl.when(e == pl.num_programs(1) - 1)
    def _():
        ffn = acc_scr[...].T
        o_ref[...] = _layer_norm(alpha * x_ref[...] + ffn, g_ref[...], b_ref[...])


def _peer_mix(xt, x, u, vt, rank, bexp, cnt, arow, ln_g, ln_b, alpha, tt, te, eb):
    t, d = x.shape
    n_exp = u.shape[0]
    nblk = te // eb
    kern = functools.partial(_peer_kernel, tt=tt, nblk=nblk, eb=eb, alpha=alpha)
    big = pl.BlockSpec((PEER_HEADS, N_KEYS, tt), lambda i, e: (0, 0, i))
    rows = pl.BlockSpec((PEER_HEADS, te // N_KEYS, tt), lambda i, e: (0, e, i))
    vec = pl.BlockSpec((1, d), lambda i, e: (0, 0))
    return pl.pallas_call(
        kern,
        grid=(t // tt, n_exp // te),
        in_specs=[pl.BlockSpec((d, tt), lambda i, e: (0, i)),
                  pl.BlockSpec((tt, d), lambda i, e: (i, 0)),
                  pl.BlockSpec((te, d), lambda i, e: (e, 0)),
                  pl.BlockSpec((nblk // 2, d, 2 * eb), lambda i, e: (e, 0, 0)),
                  big, big, rows, rows, vec, vec],
        out_specs=pl.BlockSpec((tt, d), lambda i, e: (i, 0)),
        out_shape=jax.ShapeDtypeStruct((t, d), F32),
        scratch_shapes=([pltpu.VMEM((d, tt), F32),
                         pltpu.VMEM((2, PEER_HEADS, GATE_ROWS, tt), BF16)]
                        + [pltpu.VMEM((eb, tt), F32)] * N_ACT_BUFS
                        + [pltpu.VMEM((2 * eb, tt), BF16)] * N_W_BUFS),
        compiler_params=_params("parallel", "arbitrary"),
        name="peer_mix_ln",
    )(xt, x, u, vt, rank, bexp, cnt, arow, ln_g, ln_b)


def _tiles(b, s, t):
    return dict(mm_tm=min(2048, t), mm_tn=1024, attn_tb=min(256, s), attn_hp=8, vt_blocks=4,
                lru_ts=min(512, s), merge_tm=min(512, t), sel_tt=min(512, t),
                mix_tt=min(512, t), mix_te=2048, mix_eb=256)


def kernel(x, w_in, lambda_qk, subln_g, conv_w, conv_b, gate_a_w, gate_a_b, gate_x_w, gate_x_b,
           lru_lambda, w_br_attn, w_br_lru, w_out, ln1_g, ln1_b, peer_wq, peer_subkeys, peer_u,
           peer_v, ln2_g, ln2_b):
    b, s, d = x.shape
    t = b * s
    depth = w_in.shape[0]
    alpha = (2.0 * depth) ** 0.25
    attn_w = ATTN_HEADS * HEAD_W
    ts = _tiles(b, s, t)
    slopes = jnp.exp2(-8.0 * jnp.arange(1, ATTN_HEADS + 1, dtype=F32) / ATTN_HEADS)
    row = lambda a: a.reshape(1, -1)
    eb = ts["mix_eb"]

    xf = x.reshape(t, d)
    for l in range(depth):
        lam_init = 0.8 - 0.6 * math.exp(-0.3 * l)
        w_l = w_in[l].astype(BF16)
        w_cat = jnp.concatenate([w_l[:, :2 * attn_w], w_l[:, 3 * attn_w:]], axis=1)
        proj = _matmul(xf, w_cat, BF16, ts["mm_tm"], ts["mm_tn"])
        vt = _value_proj_t(xf, w_l[:, 2 * attn_w:3 * attn_w].T, ts["attn_tb"], ts["vt_blocks"])
        proj3 = proj.reshape(b, s, -1)
        ya = _diff_attention(proj3, vt, slopes, lambda_qk[l], row(subln_g[l]),
                             lam_init, ts["attn_tb"], ts["attn_hp"])
        yr = _rg_lru(proj3, conv_w[l], row(conv_b[l]), gate_a_w[l].astype(BF16),
                     row(gate_a_b[l]), gate_x_w[l].astype(BF16), row(gate_x_b[l]),
                     row(lru_lambda[l]), ts["lru_ts"])
        x1, x1t = _merge_out(ya.reshape(t, attn_w), yr.reshape(t, d), proj, xf,
                             w_br_attn[l].astype(BF16), w_br_lru[l].astype(BF16),
                             w_out[l].astype(BF16), row(ln1_g[l]), row(ln1_b[l]), alpha,
                             ts["merge_tm"])
        rank, bexp, cnt, arow = _peer_select(x1t, peer_wq[l].T.astype(BF16),
                                             peer_subkeys[l].astype(BF16), ts["sel_tt"])
        v_blocks = peer_v[l].astype(BF16).reshape(-1, 2 * eb, d).transpose(0, 2, 1)
        xf = _peer_mix(x1t, x1, peer_u[l].astype(BF16), v_blocks, rank, bexp, cnt, arow,
                       row(ln2_g[l]), row(ln2_b[l]), alpha, ts["mix_tt"], ts["mix_te"], eb)
    return xf.reshape(b, s, d)
```
